```python
import jax, jax.numpy as jnp
from jax import lax
import numpy as np

D_MODEL = 2048
BATCH = 4
SEQ = 2048
DEPTH = 1
DEC_BATCH = 128
DEC_SEQ = 4
PAST_LEN = 16384
PAGE_SIZE = 128

D_RNN = D_MODEL
RG_BLOCKS = 16
RG_BW = D_RNN // RG_BLOCKS
CONV_W = 4
RG_C = 8.0
HEAD = 64
D_ATT = D_MODEL
N_HEADS = D_ATT // HEAD
R_W = 96
R_A = 96
R_G = 256
GN_EPS = 64e-5
D_FF = 4 * D_MODEL
D_PLE = 256
EPS = 1e-6
OFF_RGX = 0
OFF_RGG = OFF_RGX + D_RNN
OFF_RWKV = OFF_RGG + D_RNN
N_RWKV = 3 * D_ATT + R_W + R_A + R_G
OFF_GATE = OFF_RWKV + N_RWKV
N_IN = OFF_GATE + 2 * D_MODEL

kernel_name = "hawk_rwkv7_parallel_gated_step"


def _rmsnorm(x, g):
    xf = x.astype(jnp.float32)
    y = xf * lax.rsqrt(jnp.mean(xf * xf, axis=-1, keepdims=True) + EPS)
    return (y * g.astype(jnp.float32)).astype(x.dtype)


def _causal_conv(u, buf, w, b):
    T = u.shape[1]
    ext = jnp.concatenate([buf, u], axis=1)
    y = b + sum(ext[:, j:j + T] * w[j] for j in range(CONV_W))
    return y, ext[:, -(CONV_W - 1):]


def _rg_lru(xc, h0, first, wa, ba, wx, bx, lam):
    B, T, _ = xc.shape
    f32 = jnp.float32
    xb = xc.reshape(B, T, RG_BLOCKS, RG_BW)
    r = jax.nn.sigmoid(jnp.einsum('btnc,ncd->btnd', xb, wa) + ba).reshape(B, T, D_RNN).astype(f32)
    i = jax.nn.sigmoid(jnp.einsum('btnc,ncd->btnd', xb, wx) + bx).reshape(B, T, D_RNN).astype(f32)
    log_a = -RG_C * r * jax.nn.softplus(-lam.astype(f32))
    a = jnp.exp(log_a)
    mult = jnp.sqrt(-jnp.expm1(2.0 * log_a))
    mult = jnp.where(first[None, :, None], 1.0, mult)
    b = mult * (i * xc.astype(f32))

    def comb(l, rr):
        return (l[0] * rr[0], rr[0] * l[1] + rr[1])

    a_cum, b_cum = lax.associative_scan(comb, (a, b), axis=1)
    h = b_cum + a_cum * h0.astype(f32)[:, None]
    return h, h[:, -1]


def _rwkv7(z, S0, w0, w2, a0, a2, g2, k_kk, k_a, r_k, gn_w, gn_b):
    B, T, _ = z.shape
    f32 = jnp.float32
    z = z.astype(f32)
    r = z[..., :D_ATT]
    k = z[..., D_ATT:2 * D_ATT]
    v = z[..., 2 * D_ATT:3 * D_ATT]
    o = 3 * D_ATT
    wd = z[..., o:o + R_W]
    ad = z[..., o + R_W:o + R_W + R_A]
    gd = z[..., o + R_W + R_A:]
    w = -jax.nn.softplus(-(w0 + jnp.tanh(wd) @ w2)) - 0.5
    decay = jnp.exp(-jnp.exp(w))
    a = jax.nn.sigmoid(a0 + ad @ a2)
    g = jax.nn.sigmoid(gd) @ g2
    hd = lambda t: t.reshape(B, T, N_HEADS, HEAD)
    kk = hd(k * k_kk)
    kk = kk * lax.rsqrt(jnp.maximum(jnp.sum(kk * kk, axis=-1, keepdims=True), 1e-24))
    k = k * (1.0 + (a - 1.0) * k_a)
    r_h, k_h, v_h, w_h, a_h = hd(r), hd(k), hd(v), hd(decay), hd(a)
    b_h = kk * a_h

    def step(S, inp):
        r_t, k_t, v_t, w_t, kk_t, b_t = inp
        sa = jnp.einsum('bhij,bhj->bhi', S, -kk_t)
        S = (S * w_t[:, :, None, :] + sa[..., None] * b_t[:, :, None, :]
             + v_t[..., None] * k_t[:, :, None, :])
        y = jnp.einsum('bhij,bhj->bhi', S, r_t)
        return S, y

    tm = lambda t: jnp.moveaxis(t, 1, 0)
    S_last, y = lax.scan(step, S0.astype(f32), (tm(r_h), tm(k_h), tm(v_h), tm(w_h), tm(kk), tm(b_h)))
    y = jnp.moveaxis(y, 0, 1)
    mu = jnp.mean(y, axis=-1, keepdims=True)
    var = jnp.mean(jnp.square(y - mu), axis=-1, keepdims=True)
    y = ((y - mu) * lax.rsqrt(var + GN_EPS)).reshape(B, T, D_ATT) * gn_w + gn_b
    bonus = jnp.sum(r_h * k_h * r_k, axis=-1, keepdims=True) * v_h
    out = (y + bonus.reshape(B, T, D_ATT)) * g
    return out, S_last


def _layer(x, p_i, cbuf, h0, S0, xprev, first,
           norm_mix, w_in, conv_w, conv_b, rg_wa, rg_ba, rg_wx, rg_bx, rg_lam, w_rg_o,
           mu_shift, rw_w0, rw_w2, rw_a0, rw_a2, rw_g2, rw_kk, rw_ka, rw_rk, rw_gn_w, rw_gn_b,
           w_rw_o, w_o, norm_ffn, w_up, w_down, norm_ple, w_ple_gate, w_ple):
    dt = x.dtype
    xn = _rmsnorm(x, norm_mix)
    P = xn @ w_in
    prw = P[..., OFF_RWKV:OFF_GATE]
    prev_row = (xprev.astype(dt) @ w_in[:, OFF_RWKV:OFF_GATE])[:, None]
    prw_prev = jnp.concatenate([prev_row, prw[:, :-1]], axis=1)
    z = prw + mu_shift * (prw_prev - prw)
    u = P[..., OFF_RGX:OFF_RGX + D_RNN]
    gate = P[..., OFF_RGG:OFF_RGG + D_RNN]
    uc, new_cbuf = _causal_conv(u, cbuf.astype(dt), conv_w, conv_b)
    h, h_last = _rg_lru(uc, h0, first, rg_wa, rg_ba, rg_wx, rg_bx, rg_lam)
    br_a = (h.astype(dt) * jax.nn.gelu(gate)) @ w_rg_o
    o_b, S_last = _rwkv7(z, S0, rw_w0, rw_w2, rw_a0, rw_a2, rw_g2, rw_kk, rw_ka, rw_rk, rw_gn_w, rw_gn_b)
    br_b = o_b.astype(dt) @ w_rw_o
    g_a = jax.nn.sigmoid(P[..., OFF_GATE:OFF_GATE + D_MODEL])
    g_b = jax.nn.sigmoid(P[..., OFF_GATE + D_MODEL:OFF_GATE + 2 * D_MODEL])
    x = x + (g_a * br_a + g_b * br_b) @ w_o
    hf = jax.nn.relu(_rmsnorm(x, norm_ffn) @ w_up)
    x = x + (hf * hf) @ w_down
    x = x + jax.nn.sigmoid(_rmsnorm(x, norm_ple) @ w_ple_gate) * (p_i.astype(dt) @ w_ple)
    return x, h_last.astype(dt), new_cbuf, S_last.astype(dt), xn[:, -1]


def setup_inputs(seed: int = 0) -> dict:
    key = jax.random.key(seed)
    ks = iter(jax.random.split(key, 64))
    nrm = lambda shape, s: jax.random.normal(next(ks), shape, jnp.float32) * s
    uni = lambda shape, lo, hi: jax.random.uniform(next(ks), shape, jnp.float32, lo, hi)
    L = DEPTH
    a_base = uni((L, D_RNN), 0.9, 0.999) ** (1.0 / RG_C)
    return {
        "x_prompt": nrm((BATCH, SEQ, D_MODEL), 1.0),
        "x_sample": nrm((DEC_BATCH, DEC_SEQ, D_MODEL), 1.0),
        "p_prompt": nrm((DEPTH, BATCH, SEQ, D_PLE), 1.0),
        "p_sample": nrm((DEPTH, DEC_BATCH, DEC_SEQ, D_PLE), 1.0),
        "state_rg_h": nrm((L, DEC_BATCH, D_RNN), 0.5),
        "state_rg_conv": nrm((L, DEC_BATCH, CONV_W - 1, D_RNN), 1.0),
        "state_rwkv": nrm((L, DEC_BATCH, N_HEADS, HEAD, HEAD), 0.5),
        "state_shift": nrm((L, DEC_BATCH, D_MODEL), 1.0),
        "norm_mix": 1.0 + nrm((L, D_MODEL), 0.02),
        "w_in": nrm((L, D_MODEL, N_IN), D_MODEL ** -0.5),
        "conv_w": nrm((L, CONV_W, D_RNN), 0.5),
        "conv_b": nrm((L, D_RNN), 0.02),
        "rg_wa": nrm((L, RG_BLOCKS, RG_BW, RG_BW), RG_BW ** -0.5),
        "rg_ba": nrm((L, RG_BLOCKS, RG_BW), 0.02),
        "rg_wx": nrm((L, RG_BLOCKS, RG_BW, RG_BW), RG_BW ** -0.5),
        "rg_bx": nrm((L, RG_BLOCKS, RG_BW), 0.02),
        "rg_lam": jnp.log(a_base / (1.0 - a_base)),
        "w_rg_o": nrm((L, D_RNN, D_MODEL), D_RNN ** -0.5),
        "mu_shift": uni((L, N_RWKV), 0.0, 1.0),
        "rw_w0": uni((L, D_ATT), -6.0, -1.0),
        "rw_w2": nrm((L, R_W, D_ATT), 0.1 * R_W ** -0.5),
        "rw_a0": nrm((L, D_ATT), 0.1),
        "rw_a2": nrm((L, R_A, D_ATT), 0.5 * R_A ** -0.5),
        "rw_g2": nrm((L, R_G, D_ATT), R_G ** -0.5),
        "rw_kk": 0.85 + nrm((L, D_ATT), 0.02),
        "rw_ka": 1.0 + nrm((L, D_ATT), 0.02),
        "rw_rk": nrm((L, N_HEADS, HEAD), 0.1),
        "rw_gn_w": 1.0 + nrm((L, D_ATT), 0.02),
        "rw_gn_b": nrm((L, D_ATT), 0.02),
        "w_rw_o": nrm((L, D_ATT, D_MODEL), D_ATT ** -0.5),
        "w_o": nrm((L, D_MODEL, D_MODEL), D_MODEL ** -0.5),
        "norm_ffn": 1.0 + nrm((L, D_MODEL), 0.02),
        "w_up": nrm((L, D_MODEL, D_FF), D_MODEL ** -0.5),
        "w_down": nrm((L, D_FF, D_MODEL), D_FF ** -0.5),
        "norm_ple": 1.0 + nrm((L, D_MODEL), 0.02),
        "w_ple_gate": nrm((L, D_MODEL, D_MODEL), D_MODEL ** -0.5),
        "w_ple": nrm((L, D_PLE, D_MODEL), D_PLE ** -0.5),
        "norm_f": 1.0 + nrm((D_MODEL,), 0.02),
    }


def reference(x_prompt, x_sample, p_prompt, p_sample, state_rg_h, state_rg_conv, state_rwkv, state_shift,
              norm_mix, w_in, conv_w, conv_b, rg_wa, rg_ba, rg_wx, rg_bx, rg_lam, w_rg_o,
              mu_shift, rw_w0, rw_w2, rw_a0, rw_a2, rw_g2, rw_kk, rw_ka, rw_rk, rw_gn_w, rw_gn_b,
              w_rw_o, w_o, norm_ffn, w_up, w_down, norm_ple, w_ple_gate, w_ple, norm_f):
    dt = x_prompt.dtype
    Bp, Tp = x_prompt.shape[0], x_prompt.shape[1]
    Ts = x_sample.shape[1]
    first_p = jnp.arange(Tp) == 0
    first_s = jnp.zeros((Ts,), dtype=bool)
    zc = jnp.zeros((Bp, CONV_W - 1, D_RNN), dt)
    zh = jnp.zeros((Bp, D_RNN), dt)
    zS = jnp.zeros((Bp, N_HEADS, HEAD, HEAD), dt)
    zx = jnp.zeros((Bp, D_MODEL), dt)
    hp, cp, sp, xp_, hs, cs, ss, xs_ = [], [], [], [], [], [], [], []
    xp, xs = x_prompt, x_sample
    for i in range(DEPTH):
        wts = (norm_mix[i], w_in[i], conv_w[i], conv_b[i], rg_wa[i], rg_ba[i], rg_wx[i], rg_bx[i],
               rg_lam[i], w_rg_o[i], mu_shift[i], rw_w0[i], rw_w2[i], rw_a0[i], rw_a2[i], rw_g2[i],
               rw_kk[i], rw_ka[i], rw_rk[i], rw_gn_w[i], rw_gn_b[i], w_rw_o[i], w_o[i], norm_ffn[i],
               w_up[i], w_down[i], norm_ple[i], w_ple_gate[i], w_ple[i])
        xp, h1, c1, s1, x1 = _layer(xp, p_prompt[i], zc, zh, zS, zx, first_p, *wts)
        xs, h2, c2, s2, x2 = _layer(xs, p_sample[i], state_rg_conv[i], state_rg_h[i], state_rwkv[i],
                                    state_shift[i], first_s, *wts)
        hp.append(h1); cp.append(c1); sp.append(s1); xp_.append(x1)
        hs.append(h2); cs.append(c2); ss.append(s2); xs_.append(x2)
    y_prompt = _rmsnorm(xp, norm_f)
    y_sample = _rmsnorm(xs, norm_f)
    return (y_prompt, y_sample,
            jnp.stack(hp), jnp.stack(cp), jnp.stack(sp), jnp.stack(xp_),
            jnp.stack(hs), jnp.stack(cs), jnp.stack(ss), jnp.stack(xs_))
```

```python
import functools

import jax
import jax.numpy as jnp
from jax import lax
from jax.experimental import pallas as pl
from jax.experimental.pallas import tpu as pltpu

F32 = jnp.float32
BF16 = jnp.bfloat16
LANES = 128
SUBLANES = 8
HEAD = 64
CHUNK = 64
EPS = 1e-6
GN_EPS = 64e-5
RG_C = 8.0
VMEM_LIMIT = 56 * 1024 * 1024


def _cparams(n_grid, vmem=VMEM_LIMIT):
    return pltpu.CompilerParams(dimension_semantics=("arbitrary",) * n_grid, vmem_limit_bytes=vmem)


def _round_up(x, m):
    return (x + m - 1) // m * m


def _rmsnorm_kernel(x_ref, g_ref, o_ref):
    x = x_ref[...]
    ms = jnp.mean(x * x, axis=-1, keepdims=True)
    o_ref[...] = ((x * lax.rsqrt(ms + EPS)) * g_ref[...]).astype(o_ref.dtype)


def _rmsnorm(x, g, out_dtype, tm):
    m, d = x.shape
    return pl.pallas_call(
        _rmsnorm_kernel,
        grid=(m // tm,),
        in_specs=[pl.BlockSpec((tm, d), lambda i: (i, 0)), pl.BlockSpec((1, d), lambda i: (0, 0))],
        out_specs=pl.BlockSpec((tm, d), lambda i: (i, 0)),
        out_shape=jax.ShapeDtypeStruct((m, d), out_dtype),
        compiler_params=_cparams(1),
        name="rmsnorm",
    )(x, g.reshape(1, d))


def _mm_kernel(*refs, n_pairs, n_extra, epilogue):
    xs = refs[:n_pairs]
    ws = refs[n_pairs:2 * n_pairs]
    extras = refs[2 * n_pairs:2 * n_pairs + n_extra]
    o_ref = refs[2 * n_pairs + n_extra]
    accs = [jnp.dot(x[...], w[...], preferred_element_type=F32) for x, w in zip(xs, ws)]
    o_ref[...] = epilogue(accs, [e[...] for e in extras]).astype(o_ref.dtype)


def _mm(xs, ws, extras, epilogue, out_dtype, n_out, tm, tn, w_col_off=0, name="mm"):
    m = xs[0].shape[0]
    grid = (n_out // tn, m // tm)
    in_specs = [pl.BlockSpec((tm, x.shape[1]), lambda n, i: (i, 0)) for x in xs]
    in_specs += [pl.BlockSpec((w.shape[0], tn), lambda n, i: (0, n + w_col_off)) for w in ws]
    args = list(xs) + list(ws)
    for kind, arr, off in extras:
        if kind == "tile":
            in_specs.append(pl.BlockSpec((tm, tn), lambda n, i, off=off: (i, n + off)))
        elif kind == "row":
            in_specs.append(pl.BlockSpec((1, tn), lambda n, i, off=off: (0, n + off)))
        else:
            raise ValueError(kind)
        args.append(arr)
    kern = functools.partial(_mm_kernel, n_pairs=len(xs), n_extra=len(extras), epilogue=epilogue)
    return pl.pallas_call(
        kern,
        grid=grid,
        in_specs=in_specs,
        out_specs=pl.BlockSpec((tm, tn), lambda n, i: (i, n)),
        out_shape=jax.ShapeDtypeStruct((m, n_out), out_dtype),
        compiler_params=_cparams(2),
        name=name,
    )(*args)


def _epi_plain(accs, extras):
    return accs[0]


def _epi_merge(accs, extras):
    ga, gb = extras
    return jax.nn.sigmoid(ga) * accs[0] + jax.nn.sigmoid(gb) * accs[1]


def _epi_residual(accs, extras):
    return extras[0] + accs[0]


def _epi_relu2(accs, extras):
    h = jnp.maximum(accs[0], 0.0)
    return h * h


def _epi_ple_final(accs, extras):
    x, gf = extras
    xo = x + jax.nn.sigmoid(accs[0]) * accs[1]
    ms = jnp.mean(xo * xo, axis=-1, keepdims=True)
    return (xo * lax.rsqrt(ms + EPS)) * gf


def _softplus(x):
    return jnp.maximum(x, 0.0) + jnp.log1p(jnp.exp(-jnp.abs(x)))


def _split_bf16(x, parts):
    out = []
    rem = x
    for _ in range(parts):
        p = rem.astype(BF16)
        out.append(p)
        rem = rem - p.astype(F32)
    return out


def _dot_exact_lhs(a_bf16, x, parts=3):
    acc = None
    for p in _split_bf16(x, parts):
        t = jnp.dot(a_bf16, p, preferred_element_type=F32)
        acc = t if acc is None else acc + t
    return acc


def _dot_exact_rhs(x, b_bf16, parts=2):
    acc = None
    for p in _split_bf16(x, parts):
        t = jnp.dot(p, b_bf16, preferred_element_type=F32)
        acc = t if acc is None else acc + t
    return acc


def _dot_3pass(a, b):
    a_hi, a_lo = _split_bf16(a, 2)
    b_hi, b_lo = _split_bf16(b, 2)
    return (jnp.dot(a_hi, b_hi, preferred_element_type=F32)
            + jnp.dot(a_hi, b_lo, preferred_element_type=F32)
            + jnp.dot(a_lo, b_hi, preferred_element_type=F32))


def _head_ones():
    r = lax.broadcasted_iota(jnp.int32, (LANES, LANES), 0) // HEAD
    c = lax.broadcasted_iota(jnp.int32, (LANES, LANES), 1) // HEAD
    return (r == c).astype(BF16)


def _rwkv_prep(zr, zk, zv, zwd, zad, zgd, w0, w2, a0, a2, g2, kkw, kaw, rkw, ones_bd):
    wlog = -_softplus(-(w0 + jnp.dot(jnp.tanh(zwd).astype(BF16), w2, preferred_element_type=F32))) - 0.5
    lw = -jnp.exp(wlog)
    a = jax.nn.sigmoid(a0 + jnp.dot(zad.astype(BF16), a2, preferred_element_type=F32))
    g = jnp.dot(jax.nn.sigmoid(zgd).astype(BF16), g2, preferred_element_type=F32)
    kk = zk * kkw
    ss = _dot_exact_rhs(kk * kk, ones_bd)
    kk = kk * lax.rsqrt(jnp.maximum(ss, 1e-24))
    k2 = zk * (1.0 + (a - 1.0) * kaw)
    beta = kk * a
    bonus = _dot_exact_rhs(zr * k2 * rkw, ones_bd) * zv
    return lw, g, kk, k2, beta, bonus


def _rwkv_post(y, bonus, g, gnw, gnb, ones_bd):
    mu = _dot_exact_rhs(y, ones_bd) * (1.0 / HEAD)
    d = y - mu
    var = _dot_exact_rhs(d * d, ones_bd) * (1.0 / HEAD)
    yn = (d * lax.rsqrt(var + GN_EPS)) * gnw + gnb
    return (yn + bonus) * g


def _rg_gates(xc, wa, ba, wx, bx, sp):
    xcb = xc.astype(BF16)
    r = jax.nn.sigmoid(jnp.dot(xcb, wa, preferred_element_type=F32) + ba)
    i = jax.nn.sigmoid(jnp.dot(xcb, wx, preferred_element_type=F32) + bx)
    log_a = (-RG_C * r) * sp
    a = jnp.exp(log_a)
    mult = jnp.sqrt(-jnp.tanh(log_a) * (a * a + 1.0))
    return a, mult, i * xc


def _rg_prompt_kernel(u_ref, gt_ref, cw_ref, cb_ref, wa_ref, ba_ref, wx_ref, bx_ref, lam_ref,
                      hg_ref, hl_ref, cs_ref, us_ref, a_ref, b_ref, *, t_len, rc):
    cwid = u_ref.shape[1]
    nb = cwid // LANES
    us_ref[0:SUBLANES, :] = jnp.zeros((SUBLANES, cwid), F32)
    us_ref[SUBLANES:, :] = u_ref[...]
    sp = _softplus(-lam_ref[...])
    w0 = cw_ref[0:1, :]
    w1 = cw_ref[1:2, :]
    w2 = cw_ref[2:3, :]
    w3 = cw_ref[3:4, :]
    cb = cb_ref[...]
    n_ch = t_len // rc

    def phase1(c, carry):
        r0 = pl.multiple_of(c * rc, rc)
        e = us_ref[pl.ds(r0, rc + SUBLANES), :]
        u0 = e[SUBLANES:]
        u1 = pltpu.roll(e, 1, 0)[SUBLANES:]
        u2 = pltpu.roll(e, 2, 0)[SUBLANES:]
        u3 = pltpu.roll(e, 3, 0)[SUBLANES:]
        xc = cb + (u3 * w0 + u2 * w1 + u1 * w2 + u0 * w3)
        row = lax.broadcasted_iota(jnp.int32, (rc, LANES), 0) + r0
        for n in range(nb):
            sl = slice(n * LANES, (n + 1) * LANES)
            a, mult, ix = _rg_gates(xc[:, sl], wa_ref[n], ba_ref[n], wx_ref[n], bx_ref[n], sp[:, sl])
            mult = jnp.where(row == 0, 1.0, mult)
            a_ref[pl.ds(r0, rc), sl] = a
            b_ref[pl.ds(r0, rc), sl] = mult * ix
        return carry

    lax.fori_loop(0, n_ch, phase1, 0)

    rowi = lax.broadcasted_iota(jnp.int32, (SUBLANES, cwid), 0)

    def scan(i, h_prev):
        r0 = pl.multiple_of(i * SUBLANES, SUBLANES)
        a = a_ref[pl.ds(r0, SUBLANES), :]
        b = b_ref[pl.ds(r0, SUBLANES), :]
        for d in (1, 2, 4):
            a_sh = jnp.where(rowi < d, 1.0, pltpu.roll(a, d, 0))
            b_sh = jnp.where(rowi < d, 0.0, pltpu.roll(b, d, 0))
            b = a * b_sh + b
            a = a * a_sh
        h = b + a * h_prev
        b_ref[pl.ds(r0, SUBLANES), :] = h
        return h[SUBLANES - 1:SUBLANES, :]

    h_last = lax.fori_loop(0, t_len // SUBLANES, scan, jnp.zeros((1, cwid), F32))
    hl_ref[0] = h_last
    cs_ref[0] = u_ref[pl.ds(t_len - SUBLANES, SUBLANES), :]

    def phase3(c, carry):
        r0 = pl.multiple_of(c * rc, rc)
        h = b_ref[pl.ds(r0, rc), :]
        gate = gt_ref[pl.ds(r0, rc), :]
        hg_ref[pl.ds(r0, rc), :] = (h * jax.nn.gelu(gate)).astype(hg_ref.dtype)
        return carry

    lax.fori_loop(0, n_ch, phase3, 0)


def _rg_prompt(p, conv_w, conv_b, wa, ba, wx, bx, lam, *, bp, tp, d, m_tot, cb_u, cb_g, cwid):
    nb = cwid // LANES
    nblk = d // cwid
    rc = 256 if tp % 256 == 0 else tp
    kern = functools.partial(_rg_prompt_kernel, t_len=tp, rc=rc)
    return pl.pallas_call(
        kern,
        grid=(bp, nblk),
        in_specs=[
            pl.BlockSpec((tp, cwid), lambda b, n: (b, cb_u * LANES // cwid + n)),
            pl.BlockSpec((tp, cwid), lambda b, n: (b, cb_g * LANES // cwid + n)),
            pl.BlockSpec((4, cwid), lambda b, n: (0, n)),
            pl.BlockSpec((1, cwid), lambda b, n: (0, n)),
            pl.BlockSpec((nb, LANES, LANES), lambda b, n: (n, 0, 0)),
            pl.BlockSpec((nb, 1, LANES), lambda b, n: (n, 0, 0)),
            pl.BlockSpec((nb, LANES, LANES), lambda b, n: (n, 0, 0)),
            pl.BlockSpec((nb, 1, LANES), lambda b, n: (n, 0, 0)),
            pl.BlockSpec((1, cwid), lambda b, n: (0, n)),
        ],
        out_specs=[
            pl.BlockSpec((tp, cwid), lambda b, n: (b, n)),
            pl.BlockSpec((1, 1, cwid), lambda b, n: (b, 0, n)),
            pl.BlockSpec((1, SUBLANES, cwid), lambda b, n: (b, 0, n)),
        ],
        out_shape=[
            jax.ShapeDtypeStruct((m_tot, d), BF16),
            jax.ShapeDtypeStruct((bp, 1, d), F32),
            jax.ShapeDtypeStruct((bp, SUBLANES, d), F32),
        ],
        scratch_shapes=[
            pltpu.VMEM((tp + SUBLANES, cwid), F32),
            pltpu.VMEM((tp, cwid), F32),
            pltpu.VMEM((tp, cwid), F32),
        ],
        compiler_params=_cparams(2),
        name="rg_prompt",
    )(p, p, conv_w, conv_b, wa, ba, wx, bx, lam)


def _rg_sample_kernel(u_ref, gt_ref, cbuf_ref, h0_ref, cw_ref, cb_ref, wa_ref, ba_ref, wx_ref, bx_ref, lam_ref,
                      hg_in_ref, hg_ref, hl_ref, cs_ref, *, ts, bs):
    del hg_in_ref
    cwid = u_ref.shape[1]
    nb = cwid // LANES
    n_tap = cw_ref.shape[0]
    sp = _softplus(-lam_ref[...])
    cb = cb_ref[...]
    ext = [cbuf_ref[j] for j in range(n_tap - 1)] + [u_ref[t * bs:(t + 1) * bs, :] for t in range(ts)]
    for j in range(n_tap - 1):
        cs_ref[j] = ext[len(ext) - (n_tap - 1) + j]
    h = h0_ref[...]
    for t in range(ts):
        conv = ext[t] * cw_ref[0:1, :]
        for j in range(1, n_tap):
            conv = conv + ext[t + j] * cw_ref[j:j + 1, :]
        xc = cb + conv
        pieces = []
        for n in range(nb):
            sl = slice(n * LANES, (n + 1) * LANES)
            a, mult, ix = _rg_gates(xc[:, sl], wa_ref[n], ba_ref[n], wx_ref[n], bx_ref[n], sp[:, sl])
            pieces.append(a * h[:, sl] + mult * ix)
        h = pieces[0] if nb == 1 else jnp.concatenate(pieces, axis=1)
        gate = gt_ref[t * bs:(t + 1) * bs, :]
        hg_ref[t * bs:(t + 1) * bs, :] = (h * jax.nn.gelu(gate)).astype(hg_ref.dtype)
    hl_ref[...] = h


def _rg_sample(p, hg, cbuf_t, h0, conv_w, conv_b, wa, ba, wx, bx, lam, *, ts, bs, d, mp, cb_u, cb_g, cwid):
    nb = cwid // LANES
    nblk = d // cwid
    rows = ts * bs
    rblk = mp // rows
    kern = functools.partial(_rg_sample_kernel, ts=ts, bs=bs)
    n_tap = conv_w.shape[0]
    return pl.pallas_call(
        kern,
        grid=(nblk,),
        in_specs=[
            pl.BlockSpec((rows, cwid), lambda n: (rblk, cb_u * LANES // cwid + n)),
            pl.BlockSpec((rows, cwid), lambda n: (rblk, cb_g * LANES // cwid + n)),
            pl.BlockSpec((n_tap - 1, bs, cwid), lambda n: (0, 0, n)),
            pl.BlockSpec((bs, cwid), lambda n: (0, n)),
            pl.BlockSpec((n_tap, cwid), lambda n: (0, n)),
            pl.BlockSpec((1, cwid), lambda n: (0, n)),
            pl.BlockSpec((nb, LANES, LANES), lambda n: (n, 0, 0)),
            pl.BlockSpec((nb, 1, LANES), lambda n: (n, 0, 0)),
            pl.BlockSpec((nb, LANES, LANES), lambda n: (n, 0, 0)),
            pl.BlockSpec((nb, 1, LANES), lambda n: (n, 0, 0)),
            pl.BlockSpec((1, cwid), lambda n: (0, n)),
            pl.BlockSpec(memory_space=pl.ANY),
        ],
        out_specs=[
            pl.BlockSpec((rows, cwid), lambda n: (rblk, n)),
            pl.BlockSpec((bs, cwid), lambda n: (0, n)),
            pl.BlockSpec((n_tap - 1, bs, cwid), lambda n: (0, 0, n)),
        ],
        out_shape=[
            jax.ShapeDtypeStruct(hg.shape, hg.dtype),
            jax.ShapeDtypeStruct((bs, d), F32),
            jax.ShapeDtypeStruct((n_tap - 1, bs, d), F32),
        ],
        input_output_aliases={11: 0},
        compiler_params=_cparams(1),
        name="rg_sample",
    )(p, p, cbuf_t, h0, conv_w, conv_b, wa, ba, wx, bx, lam, hg)


def _shift_rows(x, prev_row, rowi):
    return jnp.where(rowi == 0, prev_row, pltpu.roll(x, 1, 0))


def _solve_unit_lower(n_mat, rhs, x_ref):
    c = n_mat.shape[0]
    x_ref[...] = jnp.zeros(x_ref.shape, F32)
    for blk in range(c // SUBLANES):
        lo = blk * SUBLANES
        nrow = n_mat[lo:lo + SUBLANES, :]
        r = rhs[lo:lo + SUBLANES, :]
        if blk > 0:
            r = r - jnp.dot(nrow, x_ref[...], preferred_element_type=F32)
        for j in range(SUBLANES - 1):
            r = r - nrow[:, lo + j:lo + j + 1] * r[j:j + 1, :]
        x_ref[lo:lo + SUBLANES, :] = r
    return x_ref[...]


def _rwkv_prompt_kernel(pr_ref, pk_ref, pv_ref, pwd_ref, pad_ref, pgd_ref,
                        mur_ref, muk_ref, muv_ref, muwd_ref, muad_ref, mugd_ref,
                        w0_ref, w2_ref, a0_ref, a2_ref, g2_ref, kkw_ref, kaw_ref, rkw_ref, gnw_ref, gnb_ref,
                        ob_ref, s_ref,
                        r_s, lw_s, k_s, v_s, kap_s, bet_s, g_s, bon_s, y_s, x_s, *, t_len, rc):
    ones_bd = _head_ones()
    n_ch = t_len // rc
    rowi = lax.broadcasted_iota(jnp.int32, (rc, 1), 0)

    def phase1(c, carry):
        r0 = pl.multiple_of(c * rc, rc)
        refs = (pr_ref, pk_ref, pv_ref, pwd_ref, pad_ref, pgd_ref)
        mus = (mur_ref, muk_ref, muv_ref, muwd_ref, muad_ref, mugd_ref)
        zs = []
        lasts = []
        for ref, mu, prev in zip(refs, mus, carry):
            p = ref[pl.ds(r0, rc), :]
            pp = _shift_rows(p, prev, rowi)
            zs.append(p + mu[...] * (pp - p))
            lasts.append(p[rc - 1:rc, :])
        zr, zk, zv, zwd, zad, zgd = zs
        lw, g, kk, k2, beta, bonus = _rwkv_prep(
            zr, zk, zv, zwd, zad, zgd, w0_ref[...], w2_ref[...], a0_ref[...], a2_ref[...], g2_ref[...],
            kkw_ref[...], kaw_ref[...], rkw_ref[...], ones_bd)
        rows = pl.ds(r0, rc)
        r_s[rows, :] = zr
        lw_s[rows, :] = lw
        k_s[rows, :] = k2
        v_s[rows, :] = zv
        kap_s[rows, :] = kk
        bet_s[rows, :] = beta
        g_s[rows, :] = g
        bon_s[rows, :] = bonus
        return tuple(lasts)

    init = tuple(jnp.zeros((1, ref.shape[1]), F32) for ref in (pr_ref, pk_ref, pv_ref, pwd_ref, pad_ref, pgd_ref))
    lax.fori_loop(0, n_ch, phase1, init)

    c_len = CHUNK
    ri = lax.broadcasted_iota(jnp.int32, (c_len, c_len), 0)
    ci = lax.broadcasted_iota(jnp.int32, (c_len, c_len), 1)
    strict = ri > ci
    incl = ri >= ci
    l_cum = incl.astype(BF16)
    eye = ri == ci
    zpad = jnp.zeros((LANES - c_len, LANES), F32)
    zblk = jnp.zeros((c_len, HEAD), F32)

    def chunk(c, hs):
        rows = pl.ds(pl.multiple_of(c * c_len, c_len), c_len)
        r = r_s[rows, :]
        lw = lw_s[rows, :]
        k2 = k_s[rows, :]
        v = v_s[rows, :]
        kap = kap_s[rows, :]
        bet = bet_s[rows, :]
        gcum = _dot_exact_lhs(l_cum, lw)
        g_end = gcum[c_len - 1:c_len, :]
        e_neg = jnp.exp(-gcum)
        kt = kap * jnp.exp(gcum - lw)
        rt = r * jnp.exp(gcum)
        bt = bet * e_neg
        ks = k2 * e_neg
        e_hat = jnp.exp(g_end - gcum)
        bh_t = jnp.concatenate([bet * e_hat, zpad], axis=0).T
        kh_t = jnp.concatenate([k2 * e_hat, zpad], axis=0).T
        gam = jnp.exp(g_end)
        new_hs = []
        ys = []
        for h in range(2):
            sl = slice(h * HEAD, (h + 1) * HEAD)
            lhs = jnp.concatenate([kt[:, sl], rt[:, sl]], axis=0)
            rhs_t = jnp.concatenate([bt[:, sl], ks[:, sl]], axis=0)
            aa = lax.dot_general(lhs, rhs_t, (((1,), (1,)), ((), ())), preferred_element_type=F32)
            n_mat = jnp.where(strict, aa[0:c_len, 0:c_len], 0.0)
            a_ak = jnp.where(strict, aa[0:c_len, c_len:], 0.0)
            a_rb = jnp.where(incl, aa[c_len:, 0:c_len], 0.0)
            a_rk = jnp.where(incl, aa[c_len:, c_len:], 0.0)
            vh = v[:, sl]
            av = jnp.dot(a_ak, vh, preferred_element_type=F32)
            rhs = jnp.concatenate([kt[:, sl], av], axis=1)
            x = -_solve_unit_lower(n_mat, rhs, x_s.at[h])
            rhs_big = jnp.concatenate([x, jnp.concatenate([zblk, vh], axis=1)], axis=0)
            lhs_big = jnp.concatenate([
                jnp.concatenate([a_rb, a_rk], axis=1),
                jnp.concatenate([bh_t[sl, 0:c_len], kh_t[sl, 0:c_len]], axis=1)], axis=0)
            out = jnp.dot(lhs_big, rhs_big, preferred_element_type=F32)
            rp = out[0:c_len, 0:HEAD] + rt[:, sl]
            y0 = out[0:c_len, HEAD:]
            m_mat = out[c_len:, 0:HEAD] + jnp.where(eye, gam[:, sl], 0.0)
            b_mat = out[c_len:, HEAD:]
            res = _dot_3pass(jnp.concatenate([rp, m_mat], axis=0), hs[h])
            ys.append(res[0:c_len] + y0)
            new_hs.append(res[c_len:] + b_mat)
        y_s[rows, :] = jnp.concatenate(ys, axis=1)
        return tuple(new_hs)

    h0 = (jnp.zeros((HEAD, HEAD), F32), jnp.zeros((HEAD, HEAD), F32))
    hs = lax.fori_loop(0, t_len // c_len, chunk, h0)
    hh = jnp.concatenate([jnp.concatenate(list(hs), axis=1), jnp.zeros((LANES - HEAD, LANES), F32)], axis=0).T
    s_ref[0, 0] = hh[0:HEAD, 0:HEAD]
    s_ref[0, 1] = hh[HEAD:, 0:HEAD]

    def phase3(c, carry):
        rows = pl.ds(pl.multiple_of(c * rc, rc), rc)
        o = _rwkv_post(y_s[rows, :], bon_s[rows, :], g_s[rows, :], gnw_ref[...], gnb_ref[...], ones_bd)
        ob_ref[rows, :] = o.astype(ob_ref.dtype)
        return carry

    lax.fori_loop(0, n_ch, phase3, 0)


def _rwkv_param_specs(idx, pw, pa, pg):
    return [
        pl.BlockSpec((1, LANES), idx(lambda hp: (0, hp))),
        pl.BlockSpec((pw * LANES, LANES), idx(lambda hp: (0, hp))),
        pl.BlockSpec((1, LANES), idx(lambda hp: (0, hp))),
        pl.BlockSpec((pa * LANES, LANES), idx(lambda hp: (0, hp))),
        pl.BlockSpec((pg * LANES, LANES), idx(lambda hp: (0, hp))),
        pl.BlockSpec((1, LANES), idx(lambda hp: (0, hp))),
        pl.BlockSpec((1, LANES), idx(lambda hp: (0, hp))),
        pl.BlockSpec((1, LANES), idx(lambda hp: (0, hp))),
        pl.BlockSpec((1, LANES), idx(lambda hp: (0, hp))),
        pl.BlockSpec((1, LANES), idx(lambda hp: (0, hp))),
    ]


def _rwkv_prompt(p, mu, params, *, bp, tp, d, m_tot, lay):
    n_hp = d // LANES
    pw, pa, pg = lay["pw"], lay["pa"], lay["pg"]
    rc = 256 if tp % 256 == 0 else tp

    def idx2(f):
        return lambda b, hp: f(hp)

    def col(cb, width=None):
        if width is None:
            return lambda b, hp: (b, cb + hp)
        return lambda b, hp: (b, cb // width)

    def mucol(cb, width=None):
        if width is None:
            return lambda b, hp: (0, cb + hp)
        return lambda b, hp: (0, cb // width)

    in_specs = [
        pl.BlockSpec((tp, LANES), col(lay["r"])),
        pl.BlockSpec((tp, LANES), col(lay["k"])),
        pl.BlockSpec((tp, LANES), col(lay["v"])),
        pl.BlockSpec((tp, pw * LANES), col(lay["wd"], pw)),
        pl.BlockSpec((tp, pa * LANES), col(lay["ad"], pa)),
        pl.BlockSpec((tp, pg * LANES), col(lay["gd"], pg)),
        pl.BlockSpec((1, LANES), mucol(lay["r"])),
        pl.BlockSpec((1, LANES), mucol(lay["k"])),
        pl.BlockSpec((1, LANES), mucol(lay["v"])),
        pl.BlockSpec((1, pw * LANES), mucol(lay["wd"], pw)),
        pl.BlockSpec((1, pa * LANES), mucol(lay["ad"], pa)),
        pl.BlockSpec((1, pg * LANES), mucol(lay["gd"], pg)),
    ] + _rwkv_param_specs(idx2, pw, pa, pg)
    n_heads = d // HEAD
    kern = functools.partial(_rwkv_prompt_kernel, t_len=tp, rc=rc)
    vm = pltpu.VMEM((tp, LANES), F32)
    return pl.pallas_call(
        kern,
        grid=(bp, n_hp),
        in_specs=in_specs,
        out_specs=[
            pl.BlockSpec((tp, LANES), lambda b, hp: (b, hp)),
            pl.BlockSpec((1, 2, HEAD, HEAD), lambda b, hp: (b, hp, 0, 0)),
        ],
        out_shape=[
            jax.ShapeDtypeStruct((m_tot, d), BF16),
            jax.ShapeDtypeStruct((bp, n_heads, HEAD, HEAD), F32),
        ],
        scratch_shapes=[vm] * 9 + [pltpu.VMEM((2, CHUNK, LANES), F32)],
        compiler_params=_cparams(2),
        name="rwkv_prompt",
    )(p, p, p, p, p, p, mu, mu, mu, mu, mu, mu, *params)


def _rwkv_sample_kernel(pr_ref, pk_ref, pv_ref, pwd_ref, pad_ref, pgd_ref,
                        qr_ref, qk_ref, qv_ref, qwd_ref, qad_ref, qgd_ref,
                        mur_ref, muk_ref, muv_ref, muwd_ref, muad_ref, mugd_ref,
                        w0_ref, w2_ref, a0_ref, a2_ref, g2_ref, kkw_ref, kaw_ref, rkw_ref, gnw_ref, gnb_ref,
                        st_ref, ob_in_ref, ob_ref, so_ref, y_s, *, ts, bs):
    del ob_in_ref
    ones_bd = _head_ones()
    refs = (pr_ref, pk_ref, pv_ref, pwd_ref, pad_ref, pgd_ref)
    prevs = (qr_ref, qk_ref, qv_ref, qwd_ref, qad_ref, qgd_ref)
    mus = (mur_ref, muk_ref, muv_ref, muwd_ref, muad_ref, mugd_ref)
    zs = []
    for ref, q, mu in zip(refs, prevs, mus):
        p = ref[...]
        pp = jnp.concatenate([q[...], p[0:(ts - 1) * bs, :]], axis=0)
        zs.append(p + mu[...] * (pp - p))
    zr, zk, zv, zwd, zad, zgd = zs
    lw, g, kk, k2, beta, bonus = _rwkv_prep(
        zr, zk, zv, zwd, zad, zgd, w0_ref[...], w2_ref[...], a0_ref[...], a2_ref[...], g2_ref[...],
        kkw_ref[...], kaw_ref[...], rkw_ref[...], ones_bd)
    w_dec = jnp.exp(lw)

    for t in range(ts):
        rows = slice(t * bs, (t + 1) * bs)
        wt, kkt, bt, kt, rt, vt = (a[rows, :].T for a in (w_dec, kk, beta, k2, zr, zv))
        for h in range(2):
            lo = h * HEAD
            hd = slice(lo, lo + HEAD)
            s = (st_ref if t == 0 else so_ref)[h]
            sa = -jnp.sum(s * kkt[hd][None], axis=1, keepdims=True)
            s = s * wt[hd][None] + sa * bt[hd][None] + vt[hd][:, None, :] * kt[hd][None]
            so_ref[h] = s
            y_s[t, hd, :] = jnp.sum(s * rt[hd][None], axis=1)
    for t in range(ts):
        yt = y_s[t].T
        rows = slice(t * bs, (t + 1) * bs)
        o = _rwkv_post(yt, bonus[rows], g[rows], gnw_ref[...], gnb_ref[...], ones_bd)
        ob_ref[rows, :] = o.astype(ob_ref.dtype)


def _rwkv_sample(p, q, mu, params, st, ob, *, ts, bs, d, mp, lay):
    n_hp = d // LANES
    pw, pa, pg = lay["pw"], lay["pa"], lay["pg"]
    rows = ts * bs
    rblk = mp // rows
    off = lay["r"]

    def idx1(f):
        return lambda hp: f(hp)

    def col(cb, rb, width=None, shift=0):
        if width is None:
            return lambda hp: (rb, cb - shift + hp)
        return lambda hp: (rb, (cb - shift) // width)

    in_specs = [
        pl.BlockSpec((rows, LANES), col(lay["r"], rblk)),
        pl.BlockSpec((rows, LANES), col(lay["k"], rblk)),
        pl.BlockSpec((rows, LANES), col(lay["v"], rblk)),
        pl.BlockSpec((rows, pw * LANES), col(lay["wd"], rblk, pw)),
        pl.BlockSpec((rows, pa * LANES), col(lay["ad"], rblk, pa)),
        pl.BlockSpec((rows, pg * LANES), col(lay["gd"], rblk, pg)),
        pl.BlockSpec((bs, LANES), col(lay["r"], 0, None, off)),
        pl.BlockSpec((bs, LANES), col(lay["k"], 0, None, off)),
        pl.BlockSpec((bs, LANES), col(lay["v"], 0, None, off)),
        pl.BlockSpec((bs, pw * LANES), col(lay["wd"], 0, pw, off)),
        pl.BlockSpec((bs, pa * LANES), col(lay["ad"], 0, pa, off)),
        pl.BlockSpec((bs, pg * LANES), col(lay["gd"], 0, pg, off)),
        pl.BlockSpec((1, LANES), col(lay["r"], 0)),
        pl.BlockSpec((1, LANES), col(lay["k"], 0)),
        pl.BlockSpec((1, LANES), col(lay["v"], 0)),
        pl.BlockSpec((1, pw * LANES), col(lay["wd"], 0, pw)),
        pl.BlockSpec((1, pa * LANES), col(lay["ad"], 0, pa)),
        pl.BlockSpec((1, pg * LANES), col(lay["gd"], 0, pg)),
    ] + _rwkv_param_specs(idx1, pw, pa, pg) + [
        pl.BlockSpec((2, HEAD, HEAD, bs), lambda hp: (hp, 0, 0, 0)),
        pl.BlockSpec(memory_space=pl.ANY),
    ]
    kern = functools.partial(_rwkv_sample_kernel, ts=ts, bs=bs)
    n_in = len(in_specs)
    return pl.pallas_call(
        kern,
        grid=(n_hp,),
        in_specs=in_specs,
        out_specs=[
            pl.BlockSpec((rows, LANES), lambda hp: (rblk, hp)),
            pl.BlockSpec((2, HEAD, HEAD, bs), lambda hp: (hp, 0, 0, 0)),
        ],
        out_shape=[
            jax.ShapeDtypeStruct(ob.shape, ob.dtype),
            jax.ShapeDtypeStruct(st.shape, F32),
        ],
        scratch_shapes=[pltpu.VMEM((ts, LANES, bs), F32)],
        input_output_aliases={n_in - 1: 0},
        compiler_params=_cparams(1),
        name="rwkv_sample",
    )(p, p, p, p, p, p, q, q, q, q, q, q, mu, mu, mu, mu, mu, mu, *params, st, ob)


def _pad_cols(a, width):
    return jnp.pad(a, ((0, 0), (0, width - a.shape[1])))


def _pad_rows(a, height):
    return jnp.pad(a, ((0, height - a.shape[0]), (0, 0)))


def kernel(x_prompt, x_sample, p_prompt, p_sample, state_rg_h, state_rg_conv, state_rwkv, state_shift,
           norm_mix, w_in, conv_w, conv_b, rg_wa, rg_ba, rg_wx, rg_bx, rg_lam, w_rg_o,
           mu_shift, rw_w0, rw_w2, rw_a0, rw_a2, rw_g2, rw_kk, rw_ka, rw_rk, rw_gn_w, rw_gn_b,
           w_rw_o, w_o, norm_ffn, w_up, w_down, norm_ple, w_ple_gate, w_ple, norm_f):
    bp, tp, d = x_prompt.shape
    bs, ts, _ = x_sample.shape
    depth = w_in.shape[0]
    n_heads, head = rw_rk.shape[1], rw_rk.shape[2]
    r_w, r_a, r_g = rw_w2.shape[1], rw_a2.shape[1], rw_g2.shape[1]
    d_ple = w_ple.shape[1]
    d_ff = w_up.shape[2]
    n_tap = conv_w.shape[1]
    assert head == HEAD and n_heads * HEAD == d and d % LANES == 0 and bs == LANES
    assert rg_wa.shape[2] == LANES and tp % CHUNK == 0 and n_tap == 4
    mp, ms = bp * tp, bs * ts
    m_tot = mp + ms
    nb = d // LANES
    pw, pa, pg = (_round_up(r, LANES) // LANES for r in (r_w, r_a, r_g))
    lay = dict(u=0, g=nb, r=2 * nb, k=3 * nb, v=4 * nb, wd=5 * nb, ad=5 * nb + pw, gd=5 * nb + pw + pa,
               ga=5 * nb + pw + pa + pg, gb=6 * nb + pw + pa + pg, pw=pw, pa=pa, pg=pg)
    n_in = (7 * nb + pw + pa + pg) * LANES
    tm = 512
    tn = 512
    assert m_tot % tm == 0 and mp % (ts * bs) == 0 and n_in % tn == 0 and lay["gd"] % pg == 0
    assert (lay["r"] * LANES) % tn == 0 and ((lay["ga"] - lay["r"]) * LANES) % tn == 0 and (nb * LANES) % tn == 0

    xp = x_prompt.reshape(mp, d)
    xs = jnp.transpose(x_sample, (1, 0, 2)).reshape(ms, d)
    x = jnp.concatenate([xp, xs], axis=0)
    pp = p_prompt.reshape(depth, mp, d_ple)
    ps = jnp.transpose(p_sample, (0, 2, 1, 3)).reshape(depth, ms, d_ple)
    p_all = jnp.concatenate([pp, ps], axis=1).astype(BF16)

    hp_l, cp_l, sp_l, xp_l, hs_l, cs_l, ss_l, xs_l = [], [], [], [], [], [], [], []
    y = None
    for i in range(depth):
        o_rw = 2 * d
        o_g = o_rw + 3 * d + r_w + r_a + r_g
        wi = w_in[i]

        def regroup(a):
            return jnp.concatenate([
                a[:, :o_rw + 3 * d],
                _pad_cols(a[:, o_rw + 3 * d:o_rw + 3 * d + r_w], pw * LANES),
                _pad_cols(a[:, o_rw + 3 * d + r_w:o_rw + 3 * d + r_w + r_a], pa * LANES),
                _pad_cols(a[:, o_rw + 3 * d + r_w + r_a:o_g], pg * LANES),
                a[:, o_g:]], axis=1)

        w_in_r = regroup(wi).astype(BF16)
        mu_r = regroup(jnp.concatenate(
            [jnp.zeros((1, o_rw), F32), mu_shift[i][None], jnp.zeros((1, 2 * d), F32)], axis=1))
        rw_params = (
            rw_w0[i][None], _pad_rows(rw_w2[i], pw * LANES).astype(BF16),
            rw_a0[i][None], _pad_rows(rw_a2[i], pa * LANES).astype(BF16),
            _pad_rows(rw_g2[i], pg * LANES).astype(BF16),
            rw_kk[i][None], rw_ka[i][None], rw_rk[i].reshape(1, d), rw_gn_w[i][None], rw_gn_b[i][None])
        rg_params = (conv_w[i], conv_b[i][None], rg_wa[i].astype(BF16), rg_ba[i][:, None, :],
                     rg_wx[i].astype(BF16), rg_bx[i][:, None, :], rg_lam[i][None])

        xn = _rmsnorm(x, norm_mix[i], BF16, tm)
        x_last = jnp.concatenate([xp.reshape(bp, tp, d)[:, -1], xs[(ts - 1) * bs:]], axis=0) if i == 0 else None
        if x_last is None:
            x_last = jnp.concatenate([x[:mp].reshape(bp, tp, d)[:, -1], x[mp + (ts - 1) * bs:]], axis=0)
        n_last = _round_up(bp + bs, SUBLANES)
        xn_last = _rmsnorm(_pad_rows(x_last, n_last), norm_mix[i], F32, n_last)
        proj = _mm([xn], [w_in_r], [], _epi_plain, F32, n_in, tm, tn, name="in_proj")
        n_rw = (lay["ga"] - lay["r"]) * LANES
        prev = _mm([state_shift[i].astype(BF16)], [w_in_r], [], _epi_plain, F32, n_rw, bs, tn,
                   w_col_off=lay["r"] * LANES // tn, name="prev_proj")

        cwid = 2 * LANES if nb % 2 == 0 else LANES
        hg, h_p, c_p = _rg_prompt(proj, *rg_params, bp=bp, tp=tp, d=d, m_tot=m_tot,
                                  cb_u=lay["u"], cb_g=lay["g"], cwid=cwid)
        cbuf_t = jnp.transpose(state_rg_conv[i], (1, 0, 2))
        hg, h_s, c_s = _rg_sample(proj, hg, cbuf_t, state_rg_h[i], *rg_params, ts=ts, bs=bs, d=d, mp=mp,
                                  cb_u=lay["u"], cb_g=lay["g"], cwid=cwid)

        ob, s_p = _rwkv_prompt(proj, mu_r, rw_params, bp=bp, tp=tp, d=d, m_tot=m_tot, lay=lay)
        st = jnp.transpose(state_rwkv[i], (1, 2, 3, 0))
        ob, st_new = _rwkv_sample(proj, prev, mu_r, rw_params, st, ob, ts=ts, bs=bs, d=d, mp=mp, lay=lay)
        s_s = jnp.transpose(st_new, (3, 0, 1, 2))

        merged = _mm([hg, ob], [w_rg_o[i].astype(BF16), w_rw_o[i].astype(BF16)],
                     [("tile", proj, lay["ga"] * LANES // tn), ("tile", proj, lay["gb"] * LANES // tn)],
                     _epi_merge, BF16, d, tm, tn, name="merge")
        x1 = _mm([merged], [w_o[i].astype(BF16)], [("tile", x, 0)], _epi_residual, F32, d, tm, tn, name="out_proj")
        xn2 = _rmsnorm(x1, norm_ffn[i], BF16, tm)
        hf = _mm([xn2], [w_up[i].astype(BF16)], [], _epi_relu2, BF16, d_ff, tm, tn, name="mlp_up")
        x2 = _mm([hf], [w_down[i].astype(BF16)], [("tile", x1, 0)], _epi_residual, F32, d, tm, tn, name="mlp_down")
        xn3 = _rmsnorm(x2, norm_ple[i], BF16, tm)
        last = i == depth - 1
        gf = norm_f[None] if last else jnp.ones((1, d), F32)
        if last:
            y = _mm([xn3, p_all[i]], [w_ple_gate[i].astype(BF16), w_ple[i].astype(BF16)],
                    [("tile", x2, 0), ("row", gf, 0)], _epi_ple_final, F32, d, 256, d, name="ple_final")
        else:
            x = _mm([xn3, p_all[i]], [w_ple_gate[i].astype(BF16), w_ple[i].astype(BF16)],
                    [("tile", x2, 0)], lambda accs, ex: ex[0] + jax.nn.sigmoid(accs[0]) * accs[1],
                    F32, d, 256, d, name="ple")

        hp_l.append(h_p[:, 0])
        cp_l.append(c_p[:, SUBLANES - (n_tap - 1):])
        sp_l.append(s_p)
        xp_l.append(xn_last[:bp])
        hs_l.append(h_s)
        cs_l.append(jnp.transpose(c_s, (1, 0, 2)))
        ss_l.append(s_s)
        xs_l.append(xn_last[bp:bp + bs])

    y_prompt = y[:mp].reshape(bp, tp, d)
    y_sample = jnp.transpose(y[mp:].reshape(ts, bs, d), (1, 0, 2))
    return (y_prompt, y_sample,
            jnp.stack(hp_l), jnp.stack(cp_l), jnp.stack(sp_l), jnp.stack(xp_l),
            jnp.stack(hs_l), jnp.stack(cs_l), jnp.stack(ss_l), jnp.stack(xs_l))
```

```python
import functools

import jax
import jax.numpy as jnp
from jax import lax
from jax.experimental import pallas as pl
from jax.experimental.pallas import tpu as pltpu

F32 = jnp.float32
BF16 = jnp.bfloat16
LANES = 128
SUBLANES = 8
HEAD = 64
CHUNK = 64
EPS = 1e-6
GN_EPS = 64e-5
RG_C = 8.0
VMEM_LIMIT = 56 * 1024 * 1024


def _cparams(n_grid, vmem=VMEM_LIMIT):
    return pltpu.CompilerParams(dimension_semantics=("arbitrary",) * n_grid, vmem_limit_bytes=vmem)


def _round_up(x, m):
    return (x + m - 1) // m * m


def _rmsnorm_kernel(x_ref, g_ref, o_ref):
    x = x_ref[...]
    ms = jnp.mean(x * x, axis=-1, keepdims=True)
    o_ref[...] = ((x * lax.rsqrt(ms + EPS)) * g_ref[...]).astype(o_ref.dtype)


def _rmsnorm(x, g, out_dtype, tm):
    m, d = x.shape
    return pl.pallas_call(
        _rmsnorm_kernel,
        grid=(m // tm,),
        in_specs=[pl.BlockSpec((tm, d), lambda i: (i, 0)), pl.BlockSpec((1, d), lambda i: (0, 0))],
        out_specs=pl.BlockSpec((tm, d), lambda i: (i, 0)),
        out_shape=jax.ShapeDtypeStruct((m, d), out_dtype),
        compiler_params=_cparams(1),
        name="rmsnorm",
    )(x, g.reshape(1, d))


def _mm_kernel(*refs, n_pairs, n_extra, epilogue):
    xs = refs[:n_pairs]
    ws = refs[n_pairs:2 * n_pairs]
    extras = refs[2 * n_pairs:2 * n_pairs + n_extra]
    o_ref = refs[2 * n_pairs + n_extra]
    accs = [jnp.dot(x[...], w[...], preferred_element_type=F32) for x, w in zip(xs, ws)]
    o_ref[...] = epilogue(accs, [e[...] for e in extras]).astype(o_ref.dtype)


def _mm(xs, ws, extras, epilogue, out_dtype, n_out, tm, tn, w_col_off=0, single_buffer_w=False, name="mm"):
    m = xs[0].shape[0]
    grid = (n_out // tn, m // tm)
    in_specs = [pl.BlockSpec((tm, x.shape[1]), lambda n, i: (i, 0)) for x in xs]
    w_mode = dict(pipeline_mode=pl.Buffered(1)) if single_buffer_w else {}
    in_specs += [pl.BlockSpec((w.shape[0], tn), lambda n, i: (0, n + w_col_off), **w_mode) for w in ws]
    args = list(xs) + list(ws)
    for kind, arr, off in extras:
        if kind == "tile":
            in_specs.append(pl.BlockSpec((tm, tn), lambda n, i, off=off: (i, n + off)))
        elif kind == "row":
            in_specs.append(pl.BlockSpec((1, tn), lambda n, i, off=off: (0, n + off)))
        else:
            raise ValueError(kind)
        args.append(arr)
    kern = functools.partial(_mm_kernel, n_pairs=len(xs), n_extra=len(extras), epilogue=epilogue)
    return pl.pallas_call(
        kern,
        grid=grid,
        in_specs=in_specs,
        out_specs=pl.BlockSpec((tm, tn), lambda n, i: (i, n)),
        out_shape=jax.ShapeDtypeStruct((m, n_out), out_dtype),
        compiler_params=_cparams(2),
        name=name,
    )(*args)


def _epi_plain(accs, extras):
    return accs[0]


def _epi_merge(accs, extras):
    ga, gb = extras
    return jax.nn.sigmoid(ga) * accs[0] + jax.nn.sigmoid(gb) * accs[1]


def _epi_residual(accs, extras):
    return extras[0] + accs[0]


def _epi_relu2(accs, extras):
    h = jnp.maximum(accs[0], 0.0)
    return h * h


def _epi_ple_final(accs, extras):
    x, gf = extras
    xo = x + jax.nn.sigmoid(accs[0]) * accs[1]
    ms = jnp.mean(xo * xo, axis=-1, keepdims=True)
    return (xo * lax.rsqrt(ms + EPS)) * gf


def _softplus(x):
    return jnp.maximum(x, 0.0) + jnp.log1p(jnp.exp(-jnp.abs(x)))


def _split_bf16(x, parts):
    out = []
    rem = x
    for _ in range(parts):
        p = rem.astype(BF16)
        out.append(p)
        rem = rem - p.astype(F32)
    return out


def _dot_exact_lhs(a_bf16, x, parts=3):
    acc = None
    for p in _split_bf16(x, parts):
        t = jnp.dot(a_bf16, p, preferred_element_type=F32)
        acc = t if acc is None else acc + t
    return acc


def _dot_exact_rhs(x, b_bf16, parts=2):
    acc = None
    for p in _split_bf16(x, parts):
        t = jnp.dot(p, b_bf16, preferred_element_type=F32)
        acc = t if acc is None else acc + t
    return acc


def _dot_3pass(a, b):
    a_hi, a_lo = _split_bf16(a, 2)
    b_hi, b_lo = _split_bf16(b, 2)
    return (jnp.dot(a_hi, b_hi, preferred_element_type=F32)
            + jnp.dot(a_hi, b_lo, preferred_element_type=F32)
            + jnp.dot(a_lo, b_hi, preferred_element_type=F32))


def _head_ones():
    r = lax.broadcasted_iota(jnp.int32, (LANES, LANES), 0) // HEAD
    c = lax.broadcasted_iota(jnp.int32, (LANES, LANES), 1) // HEAD
    return (r == c).astype(BF16)


def _rwkv_prep(zr, zk, zv, tw, za, sg, w0, w2, a0, a2, g2, kkw, kaw, rkw, ones_bd):
    wlog = -_softplus(-(w0 + jnp.dot(tw, w2, preferred_element_type=F32))) - 0.5
    lw = -jnp.exp(wlog)
    a = jax.nn.sigmoid(a0 + jnp.dot(za, a2, preferred_element_type=F32))
    g = jnp.dot(sg, g2, preferred_element_type=F32)
    kk = zk * kkw
    ss = _dot_exact_rhs(kk * kk, ones_bd)
    kk = kk * lax.rsqrt(jnp.maximum(ss, 1e-24))
    k2 = zk * (1.0 + (a - 1.0) * kaw)
    beta = kk * a
    bonus = _dot_exact_rhs(zr * k2 * rkw, ones_bd) * zv
    return lw, g, kk, k2, beta, bonus


def _rwkv_post(y, bonus, g, gnw, gnb, ones_bd):
    mu = _dot_exact_rhs(y, ones_bd) * (1.0 / HEAD)
    d = y - mu
    var = _dot_exact_rhs(d * d, ones_bd) * (1.0 / HEAD)
    yn = (d * lax.rsqrt(var + GN_EPS)) * gnw + gnb
    return (yn + bonus) * g


def _rg_gates(xc, wa, ba, wx, bx, sp):
    xcb = xc.astype(BF16)
    r = jax.nn.sigmoid(jnp.dot(xcb, wa, preferred_element_type=F32) + ba)
    i = jax.nn.sigmoid(jnp.dot(xcb, wx, preferred_element_type=F32) + bx)
    log_a = (-RG_C * r) * sp
    a = jnp.exp(log_a)
    mult = jnp.sqrt(-jnp.tanh(log_a) * (a * a + 1.0))
    return a, mult, i * xc


def _rg_prompt_kernel(u_ref, gt_ref, cw_ref, cb_ref, wa_ref, ba_ref, wx_ref, bx_ref, lam_ref,
                      hg_ref, hl_ref, cs_ref, us_ref, a_ref, b_ref, *, t_len, rc):
    cwid = u_ref.shape[1]
    nb = cwid // LANES
    us_ref[0:SUBLANES, :] = jnp.zeros((SUBLANES, cwid), F32)
    us_ref[SUBLANES:, :] = u_ref[...]
    sp = _softplus(-lam_ref[...])
    w0 = cw_ref[0:1, :]
    w1 = cw_ref[1:2, :]
    w2 = cw_ref[2:3, :]
    w3 = cw_ref[3:4, :]
    cb = cb_ref[...]
    n_ch = t_len // rc

    def phase1(c, carry):
        r0 = pl.multiple_of(c * rc, rc)
        e = us_ref[pl.ds(r0, rc + SUBLANES), :]
        u0 = e[SUBLANES:]
        u1 = pltpu.roll(e, 1, 0)[SUBLANES:]
        u2 = pltpu.roll(e, 2, 0)[SUBLANES:]
        u3 = pltpu.roll(e, 3, 0)[SUBLANES:]
        xc = cb + (u3 * w0 + u2 * w1 + u1 * w2 + u0 * w3)
        row = lax.broadcasted_iota(jnp.int32, (rc, LANES), 0) + r0
        for n in range(nb):
            sl = slice(n * LANES, (n + 1) * LANES)
            a, mult, ix = _rg_gates(xc[:, sl], wa_ref[n], ba_ref[n], wx_ref[n], bx_ref[n], sp[:, sl])
            mult = jnp.where(row == 0, 1.0, mult)
            a_ref[n, pl.ds(r0, rc), :] = a
            b_ref[n, pl.ds(r0, rc), :] = mult * ix
        return carry

    lax.fori_loop(0, n_ch, phase1, 0)

    rowi = lax.broadcasted_iota(jnp.int32, (SUBLANES, LANES), 0)

    def scan(i, h_prev):
        r0 = pl.multiple_of(i * SUBLANES, SUBLANES)
        out = []
        for n in range(nb):
            a = a_ref[n, pl.ds(r0, SUBLANES), :]
            b = b_ref[n, pl.ds(r0, SUBLANES), :]
            for d in (1, 2, 4):
                a_sh = jnp.where(rowi < d, 1.0, pltpu.roll(a, d, 0))
                b_sh = jnp.where(rowi < d, 0.0, pltpu.roll(b, d, 0))
                b = a * b_sh + b
                a = a * a_sh
            h = b + a * h_prev[n]
            b_ref[n, pl.ds(r0, SUBLANES), :] = h
            out.append(h[SUBLANES - 1:SUBLANES, :])
        return tuple(out)

    h_last = lax.fori_loop(0, t_len // SUBLANES, scan, tuple(jnp.zeros((1, LANES), F32) for _ in range(nb)),
                           unroll=4)
    hl_ref[0] = jnp.concatenate(list(h_last), axis=1)
    cs_ref[0] = u_ref[pl.ds(t_len - SUBLANES, SUBLANES), :]

    def phase3(c, carry):
        r0 = pl.multiple_of(c * rc, rc)
        for n in range(nb):
            sl = slice(n * LANES, (n + 1) * LANES)
            gate = gt_ref[pl.ds(r0, rc), sl]
            hg_ref[pl.ds(r0, rc), sl] = (b_ref[n, pl.ds(r0, rc), :] * jax.nn.gelu(gate)).astype(hg_ref.dtype)
        return carry

    lax.fori_loop(0, n_ch, phase3, 0)


def _rg_prompt(p, conv_w, conv_b, wa, ba, wx, bx, lam, *, bp, tp, d, m_tot, cb_u, cb_g, cwid):
    nb = cwid // LANES
    nblk = d // cwid
    rc = 256 if tp % 256 == 0 else tp
    kern = functools.partial(_rg_prompt_kernel, t_len=tp, rc=rc)
    return pl.pallas_call(
        kern,
        grid=(bp, nblk),
        in_specs=[
            pl.BlockSpec((tp, cwid), lambda b, n: (b, cb_u * LANES // cwid + n)),
            pl.BlockSpec((tp, cwid), lambda b, n: (b, cb_g * LANES // cwid + n)),
            pl.BlockSpec((4, cwid), lambda b, n: (0, n)),
            pl.BlockSpec((1, cwid), lambda b, n: (0, n)),
            pl.BlockSpec((nb, LANES, LANES), lambda b, n: (n, 0, 0)),
            pl.BlockSpec((nb, 1, LANES), lambda b, n: (n, 0, 0)),
            pl.BlockSpec((nb, LANES, LANES), lambda b, n: (n, 0, 0)),
            pl.BlockSpec((nb, 1, LANES), lambda b, n: (n, 0, 0)),
            pl.BlockSpec((1, cwid), lambda b, n: (0, n)),
        ],
        out_specs=[
            pl.BlockSpec((tp, cwid), lambda b, n: (b, n)),
            pl.BlockSpec((1, 1, cwid), lambda b, n: (b, 0, n)),
            pl.BlockSpec((1, SUBLANES, cwid), lambda b, n: (b, 0, n)),
        ],
        out_shape=[
            jax.ShapeDtypeStruct((m_tot, d), BF16),
            jax.ShapeDtypeStruct((bp, 1, d), F32),
            jax.ShapeDtypeStruct((bp, SUBLANES, d), F32),
        ],
        scratch_shapes=[
            pltpu.VMEM((tp + SUBLANES, cwid), F32),
            pltpu.VMEM((nb, tp, LANES), F32),
            pltpu.VMEM((nb, tp, LANES), F32),
        ],
        compiler_params=_cparams(2),
        name="rg_prompt",
    )(p, p, conv_w, conv_b, wa, ba, wx, bx, lam)


def _rg_sample_kernel(u_ref, gt_ref, cbuf_ref, h0_ref, cw_ref, cb_ref, wa_ref, ba_ref, wx_ref, bx_ref, lam_ref,
                      hg_in_ref, hg_ref, hl_ref, cs_ref, *, ts, bs):
    del hg_in_ref
    cwid = u_ref.shape[1]
    nb = cwid // LANES
    n_tap = cw_ref.shape[0]
    sp = _softplus(-lam_ref[...])
    cb = cb_ref[...]
    ext = [cbuf_ref[j] for j in range(n_tap - 1)] + [u_ref[t * bs:(t + 1) * bs, :] for t in range(ts)]
    for j in range(n_tap - 1):
        cs_ref[j] = ext[len(ext) - (n_tap - 1) + j]
    h = h0_ref[...]
    for t in range(ts):
        conv = ext[t] * cw_ref[0:1, :]
        for j in range(1, n_tap):
            conv = conv + ext[t + j] * cw_ref[j:j + 1, :]
        xc = cb + conv
        pieces = []
        for n in range(nb):
            sl = slice(n * LANES, (n + 1) * LANES)
            a, mult, ix = _rg_gates(xc[:, sl], wa_ref[n], ba_ref[n], wx_ref[n], bx_ref[n], sp[:, sl])
            pieces.append(a * h[:, sl] + mult * ix)
        h = pieces[0] if nb == 1 else jnp.concatenate(pieces, axis=1)
        gate = gt_ref[t * bs:(t + 1) * bs, :]
        hg_ref[t * bs:(t + 1) * bs, :] = (h * jax.nn.gelu(gate)).astype(hg_ref.dtype)
    hl_ref[...] = h


def _rg_sample(p, hg, cbuf_t, h0, conv_w, conv_b, wa, ba, wx, bx, lam, *, ts, bs, d, mp, cb_u, cb_g, cwid):
    nb = cwid // LANES
    nblk = d // cwid
    rows = ts * bs
    rblk = mp // rows
    kern = functools.partial(_rg_sample_kernel, ts=ts, bs=bs)
    n_tap = conv_w.shape[0]
    return pl.pallas_call(
        kern,
        grid=(nblk,),
        in_specs=[
            pl.BlockSpec((rows, cwid), lambda n: (rblk, cb_u * LANES // cwid + n)),
            pl.BlockSpec((rows, cwid), lambda n: (rblk, cb_g * LANES // cwid + n)),
            pl.BlockSpec((n_tap - 1, bs, cwid), lambda n: (0, 0, n)),
            pl.BlockSpec((bs, cwid), lambda n: (0, n)),
            pl.BlockSpec((n_tap, cwid), lambda n: (0, n)),
            pl.BlockSpec((1, cwid), lambda n: (0, n)),
            pl.BlockSpec((nb, LANES, LANES), lambda n: (n, 0, 0)),
            pl.BlockSpec((nb, 1, LANES), lambda n: (n, 0, 0)),
            pl.BlockSpec((nb, LANES, LANES), lambda n: (n, 0, 0)),
            pl.BlockSpec((nb, 1, LANES), lambda n: (n, 0, 0)),
            pl.BlockSpec((1, cwid), lambda n: (0, n)),
            pl.BlockSpec(memory_space=pl.ANY),
        ],
        out_specs=[
            pl.BlockSpec((rows, cwid), lambda n: (rblk, n)),
            pl.BlockSpec((bs, cwid), lambda n: (0, n)),
            pl.BlockSpec((n_tap - 1, bs, cwid), lambda n: (0, 0, n)),
        ],
        out_shape=[
            jax.ShapeDtypeStruct(hg.shape, hg.dtype),
            jax.ShapeDtypeStruct((bs, d), F32),
            jax.ShapeDtypeStruct((n_tap - 1, bs, d), F32),
        ],
        input_output_aliases={11: 0},
        compiler_params=_cparams(1),
        name="rg_sample",
    )(p, p, cbuf_t, h0, conv_w, conv_b, wa, ba, wx, bx, lam, hg)


def _shift_rows(x, prev_row, rowi):
    return jnp.where(rowi == 0, prev_row, pltpu.roll(x, 1, 0))


def _head_split(z, lane_head):
    return jnp.concatenate([jnp.where(lane_head == 0, z, 0.0), jnp.where(lane_head == 1, z, 0.0)], axis=0)


def _solve_unit_lower(n_mats, rhss, lane_head):
    n_sys = len(n_mats)
    c = n_mats[0].shape[0]
    width = rhss[0].shape[1]
    done = [[] for _ in range(n_sys)]
    for blk in range(c // SUBLANES):
        lo = blk * SUBLANES
        rs = []
        for i in range(n_sys):
            r = rhss[i][lo:lo + SUBLANES, :]
            if blk > 0:
                x_prev = jnp.concatenate(done[i] + [jnp.zeros((c - lo, width), F32)], axis=0)
                r = r - jnp.dot(n_mats[i][lo:lo + SUBLANES, :], _head_split(x_prev, lane_head),
                                preferred_element_type=F32)
            rs.append(r)
        for j in range(SUBLANES - 1):
            for i in range(n_sys):
                nrow = n_mats[i][lo:lo + SUBLANES, :]
                mult = jnp.where(lane_head == 0, nrow[:, lo + j:lo + j + 1], nrow[:, HEAD + lo + j:HEAD + lo + j + 1])
                rs[i] = rs[i] - mult * rs[i][j:j + 1, :]
        for i in range(n_sys):
            done[i].append(rs[i])
    return [jnp.concatenate(b, axis=0) for b in done]


def _rwkv_prompt_kernel(pr_ref, pk_ref, pv_ref, pwd_ref, pad_ref, pgd_ref,
                        mur_ref, muk_ref, muv_ref, muwd_ref, muad_ref, mugd_ref,
                        w0_ref, w2_ref, a0_ref, a2_ref, g2_ref, kkw_ref, kaw_ref, rkw_ref, gnw_ref, gnb_ref,
                        ob_ref, s_ref,
                        r_s, lw_s, k_s, v_s, kap_s, bet_s, g_s, bon_s, y_s, tw_s, za_s, sg_s, lm_s, yb_s,
                        *, t_len, rc, group):
    ones_bd = _head_ones()
    n_ch = t_len // rc
    rowi = lax.broadcasted_iota(jnp.int32, (rc, 1), 0)

    def shifted(refs, mus, carry, r0):
        zs, lasts = [], []
        for ref, mu, prev in zip(refs, mus, carry):
            p = ref[pl.ds(r0, rc), :]
            zs.append(p + mu[...] * (_shift_rows(p, prev, rowi) - p))
            lasts.append(p[rc - 1:rc, :])
        return zs, tuple(lasts)

    @pl.when(pl.program_id(1) == 0)
    def _():
        lora_refs = (pwd_ref, pad_ref, pgd_ref)

        def phase0(c, carry):
            r0 = pl.multiple_of(c * rc, rc)
            (zwd, zad, zgd), lasts = shifted(lora_refs, (muwd_ref, muad_ref, mugd_ref), carry, r0)
            rows = pl.ds(r0, rc)
            tw_s[rows, :] = jnp.tanh(zwd).astype(BF16)
            za_s[rows, :] = zad.astype(BF16)
            sg_s[rows, :] = jax.nn.sigmoid(zgd).astype(BF16)
            return lasts

        lax.fori_loop(0, n_ch, phase0, tuple(jnp.zeros((1, ref.shape[1]), F32) for ref in lora_refs))

    def phase1(c, carry):
        r0 = pl.multiple_of(c * rc, rc)
        (zr, zk, zv), lasts = shifted((pr_ref, pk_ref, pv_ref), (mur_ref, muk_ref, muv_ref), carry, r0)
        rows = pl.ds(r0, rc)
        lw, g, kk, k2, beta, bonus = _rwkv_prep(
            zr, zk, zv, tw_s[rows, :], za_s[rows, :], sg_s[rows, :],
            w0_ref[...], w2_ref[...], a0_ref[...], a2_ref[...], g2_ref[...],
            kkw_ref[...], kaw_ref[...], rkw_ref[...], ones_bd)
        r_s[rows, :] = zr
        lw_s[rows, :] = lw
        k_s[rows, :] = k2
        v_s[rows, :] = zv
        kap_s[rows, :] = kk
        bet_s[rows, :] = beta
        g_s[rows, :] = g
        bon_s[rows, :] = bonus
        return lasts

    lax.fori_loop(0, n_ch, phase1, tuple(jnp.zeros((1, LANES), F32) for _ in range(3)))

    c_len = CHUNK
    ti = lax.broadcasted_iota(jnp.int32, (c_len, LANES), 0)
    si = lax.broadcasted_iota(jnp.int32, (c_len, LANES), 1) % HEAD
    strict = ti > si
    incl = ti >= si
    lri = lax.broadcasted_iota(jnp.int32, (c_len, c_len), 0)
    lci = lax.broadcasted_iota(jnp.int32, (c_len, c_len), 1)
    l_cum = (lri >= lci).astype(BF16)
    lane_head = lax.broadcasted_iota(jnp.int32, (1, LANES), 1) // HEAD
    lane_head2 = jnp.concatenate([lane_head, lane_head], axis=1)
    bri = lax.broadcasted_iota(jnp.int32, (LANES, LANES), 0)
    bci = lax.broadcasted_iota(jnp.int32, (LANES, LANES), 1)
    same_head = (bri // HEAD) == (bci // HEAD)
    same_head2 = jnp.concatenate([same_head, same_head], axis=1)
    eye = bri == bci
    zpad = jnp.zeros((LANES - c_len, LANES), F32)
    zc = jnp.zeros((c_len, LANES), F32)
    nt_dims = (((1,), (1,)), ((), ()))

    def precompute(gi, carry):
        chunks = [gi * group + cc for cc in range(group)]
        rows = [pl.ds(pl.multiple_of(c * c_len, c_len), c_len) for c in chunks]
        lws = [lw_s[rw, :] for rw in rows]
        gcums = [_dot_exact_lhs(l_cum, lw) for lw in lws]
        g_ends = [g[c_len - 1:c_len, :] for g in gcums]
        kts = [kap_s[rw, :] * jnp.exp(g - lw) for rw, g, lw in zip(rows, gcums, lws)]
        rts = [r_s[rw, :] * jnp.exp(g) for rw, g in zip(rows, gcums)]
        e_negs = [jnp.exp(-g) for g in gcums]
        lhss = [jnp.concatenate([kt, rt], axis=0) for kt, rt in zip(kts, rts)]
        o_bs = [lax.dot_general(lhs, _head_split(bet_s[rw, :] * en, lane_head), nt_dims, preferred_element_type=F32)
                for lhs, rw, en in zip(lhss, rows, e_negs)]
        o_ks = [lax.dot_general(lhs, _head_split(k_s[rw, :] * en, lane_head), nt_dims, preferred_element_type=F32)
                for lhs, rw, en in zip(lhss, rows, e_negs)]
        n_mats = [jnp.where(strict, o[0:c_len], 0.0) for o in o_bs]
        v_bds = [_head_split(v_s[rw, :], lane_head) for rw in rows]
        avs = [jnp.dot(jnp.where(strict, o[0:c_len], 0.0), v_bd, preferred_element_type=F32)
               for o, v_bd in zip(o_ks, v_bds)]
        rhss = [jnp.concatenate([kt, av], axis=1) for kt, av in zip(kts, avs)]
        a_rs = [jnp.concatenate([jnp.where(incl, ob[c_len:], 0.0), jnp.where(incl, ok[c_len:], 0.0)], axis=1)
                for ob, ok in zip(o_bs, o_ks)]
        e_hats = [jnp.exp(ge - g) for ge, g in zip(g_ends, gcums)]
        bk_ts = [jnp.concatenate([jnp.concatenate([bet_s[rw, :] * eh, zpad], axis=0).T[:, 0:c_len],
                                  jnp.concatenate([k_s[rw, :] * eh, zpad], axis=0).T[:, 0:c_len]], axis=1)
                 for rw, eh in zip(rows, e_hats)]
        xs = _solve_unit_lower(n_mats, rhss, lane_head2)
        wus = [-x for x in xs]
        tops = [jnp.dot(a_r, jnp.concatenate([_head_split(wu, lane_head2),
                                              jnp.concatenate([jnp.zeros((LANES, LANES), F32), v_bd], axis=1)],
                                             axis=0), preferred_element_type=F32)
                for a_r, wu, v_bd in zip(a_rs, wus, v_bds)]
        bots = [jnp.dot(bk_t, jnp.concatenate([wu, jnp.concatenate([zc, v_s[rw, :]], axis=1)], axis=0),
                        preferred_element_type=F32)
                for bk_t, wu, rw in zip(bk_ts, wus, rows)]
        for c, top, bot, rt, ge in zip(chunks, tops, bots, rts, g_ends):
            bot = jnp.where(same_head2, bot, 0.0)
            base = pl.multiple_of(c * (c_len + LANES), SUBLANES)
            lm_s[pl.ds(base, c_len), :] = top[:, 0:LANES] + rt
            lm_s[pl.ds(base + c_len, LANES), :] = bot[:, 0:LANES] + jnp.where(eye, jnp.exp(ge), 0.0)
            yb_s[pl.ds(base, c_len), :] = top[:, LANES:]
            yb_s[pl.ds(base + c_len, LANES), :] = bot[:, LANES:]
        return carry

    lax.fori_loop(0, t_len // (c_len * group), precompute, 0)

    def advance(c, h_bd):
        base = pl.multiple_of(c * (c_len + LANES), SUBLANES)
        res = _dot_3pass(lm_s[pl.ds(base, c_len + LANES), :], h_bd) + yb_s[pl.ds(base, c_len + LANES), :]
        y_s[pl.ds(pl.multiple_of(c * c_len, c_len), c_len), :] = res[0:c_len]
        return res[c_len:]

    h_bd = lax.fori_loop(0, t_len // c_len, advance, jnp.zeros((LANES, LANES), F32))
    s_bd = h_bd.T
    s_ref[0, 0] = s_bd[0:HEAD, 0:HEAD]
    s_ref[0, 1] = s_bd[HEAD:, HEAD:]

    def phase3(c, carry):
        rows = pl.ds(pl.multiple_of(c * rc, rc), rc)
        o = _rwkv_post(y_s[rows, :], bon_s[rows, :], g_s[rows, :], gnw_ref[...], gnb_ref[...], ones_bd)
        ob_ref[rows, :] = o.astype(ob_ref.dtype)
        return carry

    lax.fori_loop(0, n_ch, phase3, 0)


def _rwkv_param_specs(idx, pw, pa, pg):
    return [
        pl.BlockSpec((1, LANES), idx(lambda hp: (0, hp))),
        pl.BlockSpec((pw * LANES, LANES), idx(lambda hp: (0, hp))),
        pl.BlockSpec((1, LANES), idx(lambda hp: (0, hp))),
        pl.BlockSpec((pa * LANES, LANES), idx(lambda hp: (0, hp))),
        pl.BlockSpec((pg * LANES, LANES), idx(lambda hp: (0, hp))),
        pl.BlockSpec((1, LANES), idx(lambda hp: (0, hp))),
        pl.BlockSpec((1, LANES), idx(lambda hp: (0, hp))),
        pl.BlockSpec((1, LANES), idx(lambda hp: (0, hp))),
        pl.BlockSpec((1, LANES), idx(lambda hp: (0, hp))),
        pl.BlockSpec((1, LANES), idx(lambda hp: (0, hp))),
    ]


def _rwkv_prompt(p, mu, params, *, bp, tp, d, m_tot, lay):
    n_hp = d // LANES
    pw, pa, pg = lay["pw"], lay["pa"], lay["pg"]
    rc = 256 if tp % 256 == 0 else tp

    def idx2(f):
        return lambda b, hp: f(hp)

    def col(cb, width=None):
        if width is None:
            return lambda b, hp: (b, cb + hp)
        return lambda b, hp: (b, cb // width)

    def mucol(cb, width=None):
        if width is None:
            return lambda b, hp: (0, cb + hp)
        return lambda b, hp: (0, cb // width)

    in_specs = [
        pl.BlockSpec((tp, LANES), col(lay["r"])),
        pl.BlockSpec((tp, LANES), col(lay["k"])),
        pl.BlockSpec((tp, LANES), col(lay["v"])),
        pl.BlockSpec((tp, pw * LANES), col(lay["wd"], pw)),
        pl.BlockSpec((tp, pa * LANES), col(lay["ad"], pa)),
        pl.BlockSpec((tp, pg * LANES), col(lay["gd"], pg)),
        pl.BlockSpec((1, LANES), mucol(lay["r"])),
        pl.BlockSpec((1, LANES), mucol(lay["k"])),
        pl.BlockSpec((1, LANES), mucol(lay["v"])),
        pl.BlockSpec((1, pw * LANES), mucol(lay["wd"], pw)),
        pl.BlockSpec((1, pa * LANES), mucol(lay["ad"], pa)),
        pl.BlockSpec((1, pg * LANES), mucol(lay["gd"], pg)),
    ] + _rwkv_param_specs(idx2, pw, pa, pg)
    n_heads = d // HEAD
    n_chunks = tp // CHUNK
    group = next(g for g in (8, 4, 2, 1) if n_chunks % g == 0)
    kern = functools.partial(_rwkv_prompt_kernel, t_len=tp, rc=rc, group=group)
    vm = pltpu.VMEM((tp, LANES), F32)
    return pl.pallas_call(
        kern,
        grid=(bp, n_hp),
        in_specs=in_specs,
        out_specs=[
            pl.BlockSpec((tp, LANES), lambda b, hp: (b, hp)),
            pl.BlockSpec((1, 2, HEAD, HEAD), lambda b, hp: (b, hp, 0, 0)),
        ],
        out_shape=[
            jax.ShapeDtypeStruct((m_tot, d), BF16),
            jax.ShapeDtypeStruct((bp, n_heads, HEAD, HEAD), F32),
        ],
        scratch_shapes=[vm] * 9 + [
            pltpu.VMEM((tp, pw * LANES), BF16),
            pltpu.VMEM((tp, pa * LANES), BF16),
            pltpu.VMEM((tp, pg * LANES), BF16),
            pltpu.VMEM((n_chunks * (CHUNK + LANES), LANES), F32),
            pltpu.VMEM((n_chunks * (CHUNK + LANES), LANES), F32),
        ],
        compiler_params=_cparams(2),
        name="rwkv_prompt",
    )(p, p, p, p, p, p, mu, mu, mu, mu, mu, mu, *params)


def _rwkv_sample_kernel(pr_ref, pk_ref, pv_ref, pwd_ref, pad_ref, pgd_ref,
                        qr_ref, qk_ref, qv_ref, qwd_ref, qad_ref, qgd_ref,
                        mur_ref, muk_ref, muv_ref, muwd_ref, muad_ref, mugd_ref,
                        w0_ref, w2_ref, a0_ref, a2_ref, g2_ref, kkw_ref, kaw_ref, rkw_ref, gnw_ref, gnb_ref,
                        st_ref, ob_in_ref, ob_ref, so_ref, y_s, *, ts, bs):
    del ob_in_ref
    ones_bd = _head_ones()
    refs = (pr_ref, pk_ref, pv_ref, pwd_ref, pad_ref, pgd_ref)
    prevs = (qr_ref, qk_ref, qv_ref, qwd_ref, qad_ref, qgd_ref)
    mus = (mur_ref, muk_ref, muv_ref, muwd_ref, muad_ref, mugd_ref)
    zs = []
    for ref, q, mu in zip(refs, prevs, mus):
        p = ref[...]
        pp = jnp.concatenate([q[...], p[0:(ts - 1) * bs, :]], axis=0)
        zs.append(p + mu[...] * (pp - p))
    zr, zk, zv, zwd, zad, zgd = zs
    lw, g, kk, k2, beta, bonus = _rwkv_prep(
        zr, zk, zv, jnp.tanh(zwd).astype(BF16), zad.astype(BF16), jax.nn.sigmoid(zgd).astype(BF16),
        w0_ref[...], w2_ref[...], a0_ref[...], a2_ref[...], g2_ref[...],
        kkw_ref[...], kaw_ref[...], rkw_ref[...], ones_bd)
    w_dec = jnp.exp(lw)

    for t in range(ts):
        rows = slice(t * bs, (t + 1) * bs)
        wt, kkt, bt, kt, rt, vt = (a[rows, :].T for a in (w_dec, kk, beta, k2, zr, zv))
        for h in range(2):
            lo = h * HEAD
            hd = slice(lo, lo + HEAD)
            s = (st_ref if t == 0 else so_ref)[h]
            sa = -jnp.sum(s * kkt[hd][None], axis=1, keepdims=True)
            s = s * wt[hd][None] + sa * bt[hd][None] + vt[hd][:, None, :] * kt[hd][None]
            so_ref[h] = s
            y_s[t, hd, :] = jnp.sum(s * rt[hd][None], axis=1)
    for t in range(ts):
        yt = y_s[t].T
        rows = slice(t * bs, (t + 1) * bs)
        o = _rwkv_post(yt, bonus[rows], g[rows], gnw_ref[...], gnb_ref[...], ones_bd)
        ob_ref[rows, :] = o.astype(ob_ref.dtype)


def _rwkv_sample(p, q, mu, params, st, ob, *, ts, bs, d, mp, lay):
    n_hp = d // LANES
    pw, pa, pg = lay["pw"], lay["pa"], lay["pg"]
    rows = ts * bs
    rblk = mp // rows
    off = lay["r"]

    def idx1(f):
        return lambda hp: f(hp)

    def col(cb, rb, width=None, shift=0):
        if width is None:
            return lambda hp: (rb, cb - shift + hp)
        return lambda hp: (rb, (cb - shift) // width)

    in_specs = [
        pl.BlockSpec((rows, LANES), col(lay["r"], rblk)),
        pl.BlockSpec((rows, LANES), col(lay["k"], rblk)),
        pl.BlockSpec((rows, LANES), col(lay["v"], rblk)),
        pl.BlockSpec((rows, pw * LANES), col(lay["wd"], rblk, pw)),
        pl.BlockSpec((rows, pa * LANES), col(lay["ad"], rblk, pa)),
        pl.BlockSpec((rows, pg * LANES), col(lay["gd"], rblk, pg)),
        pl.BlockSpec((bs, LANES), col(lay["r"], 0, None, off)),
        pl.BlockSpec((bs, LANES), col(lay["k"], 0, None, off)),
        pl.BlockSpec((bs, LANES), col(lay["v"], 0, None, off)),
        pl.BlockSpec((bs, pw * LANES), col(lay["wd"], 0, pw, off)),
        pl.BlockSpec((bs, pa * LANES), col(lay["ad"], 0, pa, off)),
        pl.BlockSpec((bs, pg * LANES), col(lay["gd"], 0, pg, off)),
        pl.BlockSpec((1, LANES), col(lay["r"], 0)),
        pl.BlockSpec((1, LANES), col(lay["k"], 0)),
        pl.BlockSpec((1, LANES), col(lay["v"], 0)),
        pl.BlockSpec((1, pw * LANES), col(lay["wd"], 0, pw)),
        pl.BlockSpec((1, pa * LANES), col(lay["ad"], 0, pa)),
        pl.BlockSpec((1, pg * LANES), col(lay["gd"], 0, pg)),
    ] + _rwkv_param_specs(idx1, pw, pa, pg) + [
        pl.BlockSpec((2, HEAD, HEAD, bs), lambda hp: (hp, 0, 0, 0)),
        pl.BlockSpec(memory_space=pl.ANY),
    ]
    kern = functools.partial(_rwkv_sample_kernel, ts=ts, bs=bs)
    n_in = len(in_specs)
    return pl.pallas_call(
        kern,
        grid=(n_hp,),
        in_specs=in_specs,
        out_specs=[
            pl.BlockSpec((rows, LANES), lambda hp: (rblk, hp)),
            pl.BlockSpec((2, HEAD, HEAD, bs), lambda hp: (hp, 0, 0, 0)),
        ],
        out_shape=[
            jax.ShapeDtypeStruct(ob.shape, ob.dtype),
            jax.ShapeDtypeStruct(st.shape, F32),
        ],
        scratch_shapes=[pltpu.VMEM((ts, LANES, bs), F32)],
        input_output_aliases={n_in - 1: 0},
        compiler_params=_cparams(1),
        name="rwkv_sample",
    )(p, p, p, p, p, p, q, q, q, q, q, q, mu, mu, mu, mu, mu, mu, *params, st, ob)


def _pad_cols(a, width):
    return jnp.pad(a, ((0, 0), (0, width - a.shape[1])))


def _pad_rows(a, height):
    return jnp.pad(a, ((0, height - a.shape[0]), (0, 0)))


def kernel(x_prompt, x_sample, p_prompt, p_sample, state_rg_h, state_rg_conv, state_rwkv, state_shift,
           norm_mix, w_in, conv_w, conv_b, rg_wa, rg_ba, rg_wx, rg_bx, rg_lam, w_rg_o,
           mu_shift, rw_w0, rw_w2, rw_a0, rw_a2, rw_g2, rw_kk, rw_ka, rw_rk, rw_gn_w, rw_gn_b,
           w_rw_o, w_o, norm_ffn, w_up, w_down, norm_ple, w_ple_gate, w_ple, norm_f):
    bp, tp, d = x_prompt.shape
    bs, ts, _ = x_sample.shape
    depth = w_in.shape[0]
    n_heads, head = rw_rk.shape[1], rw_rk.shape[2]
    r_w, r_a, r_g = rw_w2.shape[1], rw_a2.shape[1], rw_g2.shape[1]
    d_ple = w_ple.shape[1]
    d_ff = w_up.shape[2]
    n_tap = conv_w.shape[1]
    assert head == HEAD and n_heads * HEAD == d and d % LANES == 0 and bs == LANES
    assert rg_wa.shape[2] == LANES and tp % CHUNK == 0 and n_tap == 4
    mp, ms = bp * tp, bs * ts
    m_tot = mp + ms
    nb = d // LANES
    pw, pa, pg = (_round_up(r, LANES) // LANES for r in (r_w, r_a, r_g))
    lay = dict(u=0, g=nb, r=2 * nb, k=3 * nb, v=4 * nb, wd=5 * nb, ad=5 * nb + pw, gd=5 * nb + pw + pa,
               ga=5 * nb + pw + pa + pg, gb=6 * nb + pw + pa + pg, pw=pw, pa=pa, pg=pg)
    n_in = (7 * nb + pw + pa + pg) * LANES
    tm = 512
    tn = 512
    assert m_tot % tm == 0 and mp % (ts * bs) == 0 and n_in % tn == 0 and lay["gd"] % pg == 0
    assert (lay["r"] * LANES) % tn == 0 and ((lay["ga"] - lay["r"]) * LANES) % tn == 0 and (nb * LANES) % tn == 0

    xp = x_prompt.reshape(mp, d)
    xs = jnp.transpose(x_sample, (1, 0, 2)).reshape(ms, d)
    x = jnp.concatenate([xp, xs], axis=0)
    pp = p_prompt.reshape(depth, mp, d_ple)
    ps = jnp.transpose(p_sample, (0, 2, 1, 3)).reshape(depth, ms, d_ple)
    p_all = jnp.concatenate([pp, ps], axis=1).astype(BF16)

    hp_l, cp_l, sp_l, xp_l, hs_l, cs_l, ss_l, xs_l = [], [], [], [], [], [], [], []
    y = None
    for i in range(depth):
        o_rw = 2 * d
        o_g = o_rw + 3 * d + r_w + r_a + r_g
        wi = w_in[i]

        def regroup(a):
            return jnp.concatenate([
                a[:, :o_rw + 3 * d],
                _pad_cols(a[:, o_rw + 3 * d:o_rw + 3 * d + r_w], pw * LANES),
                _pad_cols(a[:, o_rw + 3 * d + r_w:o_rw + 3 * d + r_w + r_a], pa * LANES),
                _pad_cols(a[:, o_rw + 3 * d + r_w + r_a:o_g], pg * LANES),
                a[:, o_g:]], axis=1)

        w_in_r = regroup(wi).astype(BF16)
        mu_r = regroup(jnp.concatenate(
            [jnp.zeros((1, o_rw), F32), mu_shift[i][None], jnp.zeros((1, 2 * d), F32)], axis=1))
        rw_params = (
            rw_w0[i][None], _pad_rows(rw_w2[i], pw * LANES).astype(BF16),
            rw_a0[i][None], _pad_rows(rw_a2[i], pa * LANES).astype(BF16),
            _pad_rows(rw_g2[i], pg * LANES).astype(BF16),
            rw_kk[i][None], rw_ka[i][None], rw_rk[i].reshape(1, d), rw_gn_w[i][None], rw_gn_b[i][None])
        rg_params = (conv_w[i], conv_b[i][None], rg_wa[i].astype(BF16), rg_ba[i][:, None, :],
                     rg_wx[i].astype(BF16), rg_bx[i][:, None, :], rg_lam[i][None])

        xn = _rmsnorm(x, norm_mix[i], BF16, tm)
        x_last = jnp.concatenate([xp.reshape(bp, tp, d)[:, -1], xs[(ts - 1) * bs:]], axis=0) if i == 0 else None
        if x_last is None:
            x_last = jnp.concatenate([x[:mp].reshape(bp, tp, d)[:, -1], x[mp + (ts - 1) * bs:]], axis=0)
        n_last = _round_up(bp + bs, SUBLANES)
        xn_last = _rmsnorm(_pad_rows(x_last, n_last), norm_mix[i], F32, n_last)
        tn_in = n_in // 4 if (n_in // 4) % LANES == 0 and n_in % 4 == 0 else tn
        proj = _mm([xn], [w_in_r], [], _epi_plain, F32, n_in, tm, tn_in, single_buffer_w=True, name="in_proj")
        n_rw = (lay["ga"] - lay["r"]) * LANES
        prev = _mm([state_shift[i].astype(BF16)], [w_in_r], [], _epi_plain, F32, n_rw, bs, tn,
                   w_col_off=lay["r"] * LANES // tn, name="prev_proj")

        cwid = 2 * LANES if nb % 2 == 0 else LANES
        hg, h_p, c_p = _rg_prompt(proj, *rg_params, bp=bp, tp=tp, d=d, m_tot=m_tot,
                                  cb_u=lay["u"], cb_g=lay["g"], cwid=cwid)
        cbuf_t = jnp.transpose(state_rg_conv[i], (1, 0, 2))
        hg, h_s, c_s = _rg_sample(proj, hg, cbuf_t, state_rg_h[i], *rg_params, ts=ts, bs=bs, d=d, mp=mp,
                                  cb_u=lay["u"], cb_g=lay["g"], cwid=cwid)

        ob, s_p = _rwkv_prompt(proj, mu_r, rw_params, bp=bp, tp=tp, d=d, m_tot=m_tot, lay=lay)
        st = jnp.transpose(state_rwkv[i], (1, 2, 3, 0))
        ob, st_new = _rwkv_sample(proj, prev, mu_r, rw_params, st, ob, ts=ts, bs=bs, d=d, mp=mp, lay=lay)
        s_s = jnp.transpose(st_new, (3, 0, 1, 2))

        merged = _mm([hg, ob], [w_rg_o[i].astype(BF16), w_rw_o[i].astype(BF16)],
                     [("tile", proj, lay["ga"] * LANES // tn), ("tile", proj, lay["gb"] * LANES // tn)],
                     _epi_merge, BF16, d, tm, tn, name="merge")
        tn_o = 1024 if d % 1024 == 0 else tn
        x1 = _mm([merged], [w_o[i].astype(BF16)], [("tile", x, 0)], _epi_residual, F32, d, tm, tn_o, name="out_proj")
        xn2 = _rmsnorm(x1, norm_ffn[i], BF16, tm)
        tn_up = 2048 if d_ff % 2048 == 0 else tn
        hf = _mm([xn2], [w_up[i].astype(BF16)], [], _epi_relu2, BF16, d_ff, tm, tn_up, name="mlp_up")
        x2 = _mm([hf], [w_down[i].astype(BF16)], [("tile", x1, 0)], _epi_residual, F32, d, tm, tn, name="mlp_down")
        xn3 = _rmsnorm(x2, norm_ple[i], BF16, tm)
        last = i == depth - 1
        gf = norm_f[None] if last else jnp.ones((1, d), F32)
        if last:
            y = _mm([xn3, p_all[i]], [w_ple_gate[i].astype(BF16), w_ple[i].astype(BF16)],
                    [("tile", x2, 0), ("row", gf, 0)], _epi_ple_final, F32, d, 256, d, name="ple_final")
        else:
            x = _mm([xn3, p_all[i]], [w_ple_gate[i].astype(BF16), w_ple[i].astype(BF16)],
                    [("tile", x2, 0)], lambda accs, ex: ex[0] + jax.nn.sigmoid(accs[0]) * accs[1],
                    F32, d, 256, d, name="ple")

        hp_l.append(h_p[:, 0])
        cp_l.append(c_p[:, SUBLANES - (n_tap - 1):])
        sp_l.append(s_p)
        xp_l.append(xn_last[:bp])
        hs_l.append(h_s)
        cs_l.append(jnp.transpose(c_s, (1, 0, 2)))
        ss_l.append(s_s)
        xs_l.append(xn_last[bp:bp + bs])

    y_prompt = y[:mp].reshape(bp, tp, d)
    y_sample = jnp.transpose(y[mp:].reshape(ts, bs, d), (1, 0, 2))
    return (y_prompt, y_sample,
            jnp.stack(hp_l), jnp.stack(cp_l), jnp.stack(sp_l), jnp.stack(xp_l),
            jnp.stack(hs_l), jnp.stack(cs_l), jnp.stack(ss_l), jnp.stack(xs_l))
```

```python
import functools

import jax
import jax.numpy as jnp
from jax import lax
from jax.experimental import pallas as pl
from jax.experimental.pallas import tpu as pltpu

F32 = jnp.float32
BF16 = jnp.bfloat16
LANES = 128
SUBLANES = 8
HEAD = 64
CHUNK = 64
EPS = 1e-6
GN_EPS = 64e-5
RG_C = 8.0
VMEM_LIMIT = 56 * 1024 * 1024


def _cparams(n_grid, vmem=VMEM_LIMIT):
    return pltpu.CompilerParams(dimension_semantics=("arbitrary",) * n_grid, vmem_limit_bytes=vmem)


def _round_up(x, m):
    return (x + m - 1) // m * m


def _rms(x, g):
    ms = jnp.mean(x * x, axis=-1, keepdims=True)
    return (x * lax.rsqrt(ms + EPS)) * g


class _Rows:
    def __init__(self, *arrays):
        self.arrays = arrays

    @property
    def width(self):
        return self.arrays[0].shape[1]

    def specs(self, tm, n_p, col):
        if len(self.arrays) == 1:
            return [lambda wid: pl.BlockSpec((tm, wid), lambda n, i: (i, col(n)))]
        return [lambda wid: pl.BlockSpec((tm, wid), lambda n, i: (jnp.minimum(i, n_p - 1), col(n))),
                lambda wid: pl.BlockSpec((tm, wid), lambda n, i: (jnp.maximum(i - n_p, 0), col(n)))]


def _load_rows(refs, is_prompt):
    if len(refs) == 1:
        return refs[0][...]
    return jnp.where(is_prompt, refs[0][...], refs[1][...])


def _rmsnorm_kernel(*refs, n_src, n_p):
    g_ref, o_ref = refs[n_src], refs[n_src + 1]
    x = _load_rows(refs[:n_src], pl.program_id(1) < n_p)
    o_ref[...] = _rms(x, g_ref[...]).astype(o_ref.dtype)


def _rmsnorm(x, g, out_dtype, tm, mp):
    d = x.width
    m = sum(a.shape[0] for a in x.arrays)
    n_p = mp // tm
    in_specs = [mk(d) for mk in x.specs(tm, n_p, lambda n: 0)] + [pl.BlockSpec((1, d), lambda n, i: (0, 0))]
    return pl.pallas_call(
        functools.partial(_rmsnorm_kernel, n_src=len(x.arrays), n_p=n_p),
        grid=(1, m // tm),
        in_specs=in_specs,
        out_specs=pl.BlockSpec((tm, d), lambda n, i: (i, 0)),
        out_shape=jax.ShapeDtypeStruct((m, d), out_dtype),
        compiler_params=_cparams(2),
        name="rmsnorm",
    )(*x.arrays, g.reshape(1, d))


def _mm_kernel(*refs, x_counts, n_w, cast_w, extra_counts, out_counts, n_p, epilogue):
    pos = 0
    x_refs = []
    for cnt in x_counts:
        x_refs.append(refs[pos:pos + cnt])
        pos += cnt
    w_refs = refs[pos:pos + n_w]
    pos += n_w
    e_refs = []
    for cnt in extra_counts:
        e_refs.append(refs[pos:pos + cnt])
        pos += cnt
    o_refs = []
    for cnt in out_counts:
        o_refs.append(refs[pos:pos + cnt])
        pos += cnt
    w_scratch = refs[pos:]
    i = pl.program_id(1)
    is_prompt = i < n_p
    if cast_w:
        @pl.when(i == 0)
        def _():
            for w_ref, s_ref in zip(w_refs, w_scratch):
                s_ref[...] = w_ref[...].astype(BF16)
        ws = [s[...] for s in w_scratch]
    else:
        ws = [w[...] for w in w_refs]
    accs = [jnp.dot(_load_rows(xr, is_prompt).astype(BF16), w, preferred_element_type=F32)
            for xr, w in zip(x_refs, ws)]
    outs = epilogue(accs, [_load_rows(er, is_prompt) for er in e_refs])
    for refs_o, tile in zip(o_refs, outs):
        if len(refs_o) == 1:
            refs_o[0][...] = tile.astype(refs_o[0].dtype)
        else:
            @pl.when(is_prompt)
            def _():
                refs_o[0][...] = tile.astype(refs_o[0].dtype)

            @pl.when(jnp.logical_not(is_prompt))
            def _():
                refs_o[1][...] = tile.astype(refs_o[1].dtype)


def _mm(xs, ws, extras, epilogue, outs, *, mp, ms, tm, tn, n_out, w_col_off=0, cast_w=False,
        single_buffer_w=False, name="mm"):
    n_p = mp // tm
    m = mp + ms
    grid = (n_out // tn, m // tm)
    in_specs, args, x_counts, extra_counts = [], [], [], []
    for x in xs:
        in_specs += [mk(x.width) for mk in x.specs(tm, n_p, lambda n: 0)]
        args += list(x.arrays)
        x_counts.append(len(x.arrays))
    w_mode = dict(pipeline_mode=pl.Buffered(1)) if single_buffer_w else {}
    in_specs += [pl.BlockSpec((w.shape[0], tn), lambda n, i: (0, n + w_col_off), **w_mode) for w in ws]
    args += list(ws)
    for kind, src, off in extras:
        if kind == "tile":
            in_specs += [mk(tn) for mk in src.specs(tm, n_p, lambda n, off=off: n + off)]
            args += list(src.arrays)
            extra_counts.append(len(src.arrays))
        elif kind == "row":
            in_specs.append(pl.BlockSpec((1, tn), lambda n, i, off=off: (0, n + off)))
            args.append(src)
            extra_counts.append(1)
        else:
            raise ValueError(kind)
    out_specs, out_shape, out_counts = [], [], []
    for dtype, split in outs:
        if split:
            out_specs += [pl.BlockSpec((tm, tn), lambda n, i: (jnp.minimum(i, n_p - 1), n)),
                          pl.BlockSpec((tm, tn), lambda n, i: (jnp.maximum(i - n_p, 0), n))]
            out_shape += [jax.ShapeDtypeStruct((mp, n_out), dtype), jax.ShapeDtypeStruct((ms, n_out), dtype)]
            out_counts.append(2)
        else:
            out_specs.append(pl.BlockSpec((tm, tn), lambda n, i: (i, n)))
            out_shape.append(jax.ShapeDtypeStruct((m, n_out), dtype))
            out_counts.append(1)
    kern = functools.partial(_mm_kernel, x_counts=tuple(x_counts), n_w=len(ws), cast_w=cast_w,
                             extra_counts=tuple(extra_counts), out_counts=tuple(out_counts), n_p=n_p,
                             epilogue=epilogue)
    scratch = [pltpu.VMEM((w.shape[0], tn), BF16) for w in ws] if cast_w else []
    res = pl.pallas_call(
        kern,
        grid=grid,
        in_specs=in_specs,
        out_specs=out_specs,
        out_shape=out_shape,
        scratch_shapes=scratch,
        compiler_params=_cparams(2),
        name=name,
    )(*args)
    return res


def _epi_plain(accs, extras):
    return (accs[0],)


def _epi_merge(accs, extras):
    ga, gb = extras
    return (jax.nn.sigmoid(ga.astype(F32)) * accs[0] + jax.nn.sigmoid(gb.astype(F32)) * accs[1],)


def _epi_residual(accs, extras):
    return (extras[0] + accs[0],)


def _epi_residual_norm(accs, extras):
    x = extras[0] + accs[0]
    return x, _rms(x, extras[1])


def _epi_relu2(accs, extras):
    h = jnp.maximum(accs[0], 0.0)
    return (h * h,)


def _epi_ple(accs, extras):
    return (extras[0] + jax.nn.sigmoid(accs[0]) * accs[1],)


def _epi_ple_final(accs, extras):
    return (_rms(extras[0] + jax.nn.sigmoid(accs[0]) * accs[1], extras[1]),)


def _softplus(x):
    return jnp.maximum(x, 0.0) + jnp.log1p(jnp.exp(-jnp.abs(x)))


def _split_bf16(x, parts):
    out = []
    rem = x
    for _ in range(parts):
        p = rem.astype(BF16)
        out.append(p)
        rem = rem - p.astype(F32)
    return out


def _dot_exact_lhs(a_bf16, x, parts=3):
    acc = None
    for p in _split_bf16(x, parts):
        t = jnp.dot(a_bf16, p, preferred_element_type=F32)
        acc = t if acc is None else acc + t
    return acc


def _dot_exact_rhs(x, b_bf16, parts=2):
    acc = None
    for p in _split_bf16(x, parts):
        t = jnp.dot(p, b_bf16, preferred_element_type=F32)
        acc = t if acc is None else acc + t
    return acc


def _dot_3pass(a, b):
    a_hi, a_lo = _split_bf16(a, 2)
    b_hi, b_lo = _split_bf16(b, 2)
    return (jnp.dot(a_hi, b_hi, preferred_element_type=F32)
            + jnp.dot(a_hi, b_lo, preferred_element_type=F32)
            + jnp.dot(a_lo, b_hi, preferred_element_type=F32))


def _head_ones():
    r = lax.broadcasted_iota(jnp.int32, (LANES, LANES), 0) // HEAD
    c = lax.broadcasted_iota(jnp.int32, (LANES, LANES), 1) // HEAD
    return (r == c).astype(BF16)


def _rwkv_prep(zr, zk, zv, tw, za, sg, w0, w2, a0, a2, g2, kkw, kaw, rkw, ones_bd):
    wlog = -_softplus(-(w0 + jnp.dot(tw, w2, preferred_element_type=F32))) - 0.5
    lw = -jnp.exp(wlog)
    a = jax.nn.sigmoid(a0 + jnp.dot(za, a2, preferred_element_type=F32))
    g = jnp.dot(sg, g2, preferred_element_type=F32)
    kk = zk * kkw
    ss = _dot_exact_rhs(kk * kk, ones_bd)
    kk = kk * lax.rsqrt(jnp.maximum(ss, 1e-24))
    k2 = zk * (1.0 + (a - 1.0) * kaw)
    beta = kk * a
    bonus = _dot_exact_rhs(zr * k2 * rkw, ones_bd) * zv
    return lw, g, kk, k2, beta, bonus


def _rwkv_post(y, bonus, g, gnw, gnb, ones_bd):
    mu = _dot_exact_rhs(y, ones_bd) * (1.0 / HEAD)
    d = y - mu
    var = _dot_exact_rhs(d * d, ones_bd) * (1.0 / HEAD)
    yn = (d * lax.rsqrt(var + GN_EPS)) * gnw + gnb
    return (yn + bonus) * g


def _rg_gates(xc, wa, ba, wx, bx, sp):
    xcb = xc.astype(BF16)
    r = jax.nn.sigmoid(jnp.dot(xcb, wa, preferred_element_type=F32) + ba)
    i = jax.nn.sigmoid(jnp.dot(xcb, wx, preferred_element_type=F32) + bx)
    log_a = (-RG_C * r) * sp
    a = jnp.exp(log_a)
    mult = jnp.sqrt(-jnp.tanh(log_a) * (a * a + 1.0))
    return a, mult, i * xc


def _rg_prompt_kernel(u_ref, gt_ref, cw_ref, cb_ref, wa_ref, ba_ref, wx_ref, bx_ref, lam_ref,
                      hg_ref, hl_ref, cs_ref, us_ref, a_ref, b_ref, *, t_len, rc):
    cwid = u_ref.shape[1]
    nb = cwid // LANES
    us_ref[0:SUBLANES, :] = jnp.zeros((SUBLANES, cwid), F32)
    us_ref[SUBLANES:, :] = u_ref[...]
    sp = _softplus(-lam_ref[...])
    w0 = cw_ref[0:1, :]
    w1 = cw_ref[1:2, :]
    w2 = cw_ref[2:3, :]
    w3 = cw_ref[3:4, :]
    cb = cb_ref[...]
    n_ch = t_len // rc

    def phase1(c, carry):
        r0 = pl.multiple_of(c * rc, rc)
        e = us_ref[pl.ds(r0, rc + SUBLANES), :]
        u0 = e[SUBLANES:]
        u1 = pltpu.roll(e, 1, 0)[SUBLANES:]
        u2 = pltpu.roll(e, 2, 0)[SUBLANES:]
        u3 = pltpu.roll(e, 3, 0)[SUBLANES:]
        xc = cb + (u3 * w0 + u2 * w1 + u1 * w2 + u0 * w3)
        row = lax.broadcasted_iota(jnp.int32, (rc, LANES), 0) + r0
        for n in range(nb):
            sl = slice(n * LANES, (n + 1) * LANES)
            a, mult, ix = _rg_gates(xc[:, sl], wa_ref[n], ba_ref[n], wx_ref[n], bx_ref[n], sp[:, sl])
            mult = jnp.where(row == 0, 1.0, mult)
            a_ref[n, pl.ds(r0, rc), :] = a
            b_ref[n, pl.ds(r0, rc), :] = mult * ix
        return carry

    lax.fori_loop(0, n_ch, phase1, 0)

    rowi = lax.broadcasted_iota(jnp.int32, (SUBLANES, LANES), 0)

    def scan(i, h_prev):
        r0 = pl.multiple_of(i * SUBLANES, SUBLANES)
        out = []
        for n in range(nb):
            a = a_ref[n, pl.ds(r0, SUBLANES), :]
            b = b_ref[n, pl.ds(r0, SUBLANES), :]
            for d in (1, 2, 4):
                a_sh = jnp.where(rowi < d, 1.0, pltpu.roll(a, d, 0))
                b_sh = jnp.where(rowi < d, 0.0, pltpu.roll(b, d, 0))
                b = a * b_sh + b
                a = a * a_sh
            h = b + a * h_prev[n]
            b_ref[n, pl.ds(r0, SUBLANES), :] = h
            out.append(h[SUBLANES - 1:SUBLANES, :])
        return tuple(out)

    h_last = lax.fori_loop(0, t_len // SUBLANES, scan, tuple(jnp.zeros((1, LANES), F32) for _ in range(nb)),
                           unroll=4)
    hl_ref[0] = jnp.concatenate(list(h_last), axis=1)
    cs_ref[0] = u_ref[pl.ds(t_len - SUBLANES, SUBLANES), :]

    def phase3(c, carry):
        r0 = pl.multiple_of(c * rc, rc)
        for n in range(nb):
            sl = slice(n * LANES, (n + 1) * LANES)
            gate = gt_ref[pl.ds(r0, rc), sl]
            hg_ref[pl.ds(r0, rc), sl] = (b_ref[n, pl.ds(r0, rc), :] * jax.nn.gelu(gate)).astype(hg_ref.dtype)
        return carry

    lax.fori_loop(0, n_ch, phase3, 0)


def _rg_prompt(p, conv_w, conv_b, wa, ba, wx, bx, lam, *, bp, tp, d, cb_u, cb_g, cwid):
    nb = cwid // LANES
    nblk = d // cwid
    rc = 256 if tp % 256 == 0 else tp
    kern = functools.partial(_rg_prompt_kernel, t_len=tp, rc=rc)
    return pl.pallas_call(
        kern,
        grid=(bp, nblk),
        in_specs=[
            pl.BlockSpec((tp, cwid), lambda b, n: (b, cb_u * LANES // cwid + n)),
            pl.BlockSpec((tp, cwid), lambda b, n: (b, cb_g * LANES // cwid + n)),
            pl.BlockSpec((4, cwid), lambda b, n: (0, n)),
            pl.BlockSpec((1, cwid), lambda b, n: (0, n)),
            pl.BlockSpec((nb, LANES, LANES), lambda b, n: (n, 0, 0)),
            pl.BlockSpec((nb, 1, LANES), lambda b, n: (n, 0, 0)),
            pl.BlockSpec((nb, LANES, LANES), lambda b, n: (n, 0, 0)),
            pl.BlockSpec((nb, 1, LANES), lambda b, n: (n, 0, 0)),
            pl.BlockSpec((1, cwid), lambda b, n: (0, n)),
        ],
        out_specs=[
            pl.BlockSpec((tp, cwid), lambda b, n: (b, n)),
            pl.BlockSpec((1, 1, cwid), lambda b, n: (b, 0, n)),
            pl.BlockSpec((1, SUBLANES, cwid), lambda b, n: (b, 0, n)),
        ],
        out_shape=[
            jax.ShapeDtypeStruct((bp * tp, d), BF16),
            jax.ShapeDtypeStruct((bp, 1, d), F32),
            jax.ShapeDtypeStruct((bp, SUBLANES, d), F32),
        ],
        scratch_shapes=[
            pltpu.VMEM((tp + SUBLANES, cwid), F32),
            pltpu.VMEM((nb, tp, LANES), F32),
            pltpu.VMEM((nb, tp, LANES), F32),
        ],
        compiler_params=_cparams(2),
        name="rg_prompt",
    )(p, p, conv_w, conv_b, wa, ba, wx, bx, lam)


def _rg_sample_kernel(u_ref, gt_ref, cbuf_ref, h0_ref, cw_ref, cb_ref, wa_ref, ba_ref, wx_ref, bx_ref, lam_ref,
                      hg_ref, hl_ref, cs_ref, *, ts, bs):
    cwid = u_ref.shape[1]
    nb = cwid // LANES
    n_tap = cw_ref.shape[0]
    sp = _softplus(-lam_ref[...])
    cb = cb_ref[...]
    ext = [cbuf_ref[j] for j in range(n_tap - 1)] + [u_ref[t * bs:(t + 1) * bs, :] for t in range(ts)]
    for j in range(n_tap - 1):
        cs_ref[j] = ext[len(ext) - (n_tap - 1) + j]
    h = h0_ref[...]
    for t in range(ts):
        conv = ext[t] * cw_ref[0:1, :]
        for j in range(1, n_tap):
            conv = conv + ext[t + j] * cw_ref[j:j + 1, :]
        xc = cb + conv
        pieces = []
        for n in range(nb):
            sl = slice(n * LANES, (n + 1) * LANES)
            a, mult, ix = _rg_gates(xc[:, sl], wa_ref[n], ba_ref[n], wx_ref[n], bx_ref[n], sp[:, sl])
            pieces.append(a * h[:, sl] + mult * ix)
        h = pieces[0] if nb == 1 else jnp.concatenate(pieces, axis=1)
        gate = gt_ref[t * bs:(t + 1) * bs, :]
        hg_ref[t * bs:(t + 1) * bs, :] = (h * jax.nn.gelu(gate)).astype(hg_ref.dtype)
    hl_ref[...] = h


def _rg_sample(p, cbuf_t, h0, conv_w, conv_b, wa, ba, wx, bx, lam, *, ts, bs, d, mp, cb_u, cb_g, cwid):
    nb = cwid // LANES
    nblk = d // cwid
    rows = ts * bs
    rblk = mp // rows
    kern = functools.partial(_rg_sample_kernel, ts=ts, bs=bs)
    n_tap = conv_w.shape[0]
    return pl.pallas_call(
        kern,
        grid=(nblk,),
        in_specs=[
            pl.BlockSpec((rows, cwid), lambda n: (rblk, cb_u * LANES // cwid + n)),
            pl.BlockSpec((rows, cwid), lambda n: (rblk, cb_g * LANES // cwid + n)),
            pl.BlockSpec((n_tap - 1, bs, cwid), lambda n: (0, 0, n)),
            pl.BlockSpec((bs, cwid), lambda n: (0, n)),
            pl.BlockSpec((n_tap, cwid), lambda n: (0, n)),
            pl.BlockSpec((1, cwid), lambda n: (0, n)),
            pl.BlockSpec((nb, LANES, LANES), lambda n: (n, 0, 0)),
            pl.BlockSpec((nb, 1, LANES), lambda n: (n, 0, 0)),
            pl.BlockSpec((nb, LANES, LANES), lambda n: (n, 0, 0)),
            pl.BlockSpec((nb, 1, LANES), lambda n: (n, 0, 0)),
            pl.BlockSpec((1, cwid), lambda n: (0, n)),
        ],
        out_specs=[
            pl.BlockSpec((rows, cwid), lambda n: (0, n)),
            pl.BlockSpec((bs, cwid), lambda n: (0, n)),
            pl.BlockSpec((n_tap - 1, bs, cwid), lambda n: (0, 0, n)),
        ],
        out_shape=[
            jax.ShapeDtypeStruct((rows, d), BF16),
            jax.ShapeDtypeStruct((bs, d), F32),
            jax.ShapeDtypeStruct((n_tap - 1, bs, d), F32),
        ],
        compiler_params=_cparams(1),
        name="rg_sample",
    )(p, p, cbuf_t, h0, conv_w, conv_b, wa, ba, wx, bx, lam)


def _shift_rows(x, prev_row, rowi):
    return jnp.where(rowi == 0, prev_row, pltpu.roll(x, 1, 0))


def _head_split(z, lane_head):
    return jnp.concatenate([jnp.where(lane_head == 0, z, 0.0), jnp.where(lane_head == 1, z, 0.0)], axis=0)


def _solve_unit_lower(n_mats, rhss, lane_head, between_stages=None):
    n_sys = len(n_mats)
    c = n_mats[0].shape[0]
    width = rhss[0].shape[1]
    done = [[] for _ in range(n_sys)]
    for blk in range(c // SUBLANES):
        lo = blk * SUBLANES
        rs = []
        for i in range(n_sys):
            r = rhss[i][lo:lo + SUBLANES, :]
            if blk > 0:
                x_prev = jnp.concatenate(done[i] + [jnp.zeros((c - lo, width), F32)], axis=0)
                r = r - jnp.dot(n_mats[i][lo:lo + SUBLANES, :], _head_split(x_prev, lane_head),
                                preferred_element_type=F32)
            rs.append(r)
        if between_stages is not None:
            between_stages(blk)
        for j in range(SUBLANES - 1):
            for i in range(n_sys):
                nrow = n_mats[i][lo:lo + SUBLANES, :]
                mult = jnp.where(lane_head == 0, nrow[:, lo + j:lo + j + 1], nrow[:, HEAD + lo + j:HEAD + lo + j + 1])
                rs[i] = rs[i] - mult * rs[i][j:j + 1, :]
        for i in range(n_sys):
            done[i].append(rs[i])
    return [jnp.concatenate(b, axis=0) for b in done]


def _rwkv_prompt_kernel(pr_ref, pk_ref, pv_ref, pwd_ref, pad_ref, pgd_ref,
                        mur_ref, muk_ref, muv_ref, muwd_ref, muad_ref, mugd_ref,
                        w0_ref, w2_ref, a0_ref, a2_ref, g2_ref, kkw_ref, kaw_ref, rkw_ref, gnw_ref, gnb_ref,
                        ob_ref, s_ref,
                        r_s, lw_s, k_s, v_s, kap_s, bet_s, g_s, bon_s, y_s, tw_s, za_s, sg_s, lm_s, yb_s,
                        *, t_len, rc, group):
    ones_bd = _head_ones()
    n_ch = t_len // rc
    rowi = lax.broadcasted_iota(jnp.int32, (rc, 1), 0)

    def shifted(refs, mus, carry, r0):
        zs, lasts = [], []
        for ref, mu, prev in zip(refs, mus, carry):
            p = ref[pl.ds(r0, rc), :].astype(F32)
            zs.append(p + mu[...] * (_shift_rows(p, prev, rowi) - p))
            lasts.append(p[rc - 1:rc, :])
        return zs, tuple(lasts)

    @pl.when(pl.program_id(1) == 0)
    def _():
        lora_refs = (pwd_ref, pad_ref, pgd_ref)

        def phase0(c, carry):
            r0 = pl.multiple_of(c * rc, rc)
            (zwd, zad, zgd), lasts = shifted(lora_refs, (muwd_ref, muad_ref, mugd_ref), carry, r0)
            rows = pl.ds(r0, rc)
            tw_s[rows, :] = jnp.tanh(zwd).astype(BF16)
            za_s[rows, :] = zad.astype(BF16)
            sg_s[rows, :] = jax.nn.sigmoid(zgd).astype(BF16)
            return lasts

        lax.fori_loop(0, n_ch, phase0, tuple(jnp.zeros((1, ref.shape[1]), F32) for ref in lora_refs))

    def phase1(c, carry):
        r0 = pl.multiple_of(c * rc, rc)
        (zr, zk, zv), lasts = shifted((pr_ref, pk_ref, pv_ref), (mur_ref, muk_ref, muv_ref), carry, r0)
        rows = pl.ds(r0, rc)
        lw, g, kk, k2, beta, bonus = _rwkv_prep(
            zr, zk, zv, tw_s[rows, :], za_s[rows, :], sg_s[rows, :],
            w0_ref[...], w2_ref[...], a0_ref[...], a2_ref[...], g2_ref[...],
            kkw_ref[...], kaw_ref[...], rkw_ref[...], ones_bd)
        r_s[rows, :] = zr
        lw_s[rows, :] = lw
        k_s[rows, :] = k2
        v_s[rows, :] = zv
        kap_s[rows, :] = kk
        bet_s[rows, :] = beta
        g_s[rows, :] = g
        bon_s[rows, :] = bonus
        return lasts

    lax.fori_loop(0, n_ch, phase1, tuple(jnp.zeros((1, LANES), F32) for _ in range(3)))

    c_len = CHUNK
    ti = lax.broadcasted_iota(jnp.int32, (c_len, LANES), 0)
    si = lax.broadcasted_iota(jnp.int32, (c_len, LANES), 1) % HEAD
    strict = ti > si
    incl = ti >= si
    lri = lax.broadcasted_iota(jnp.int32, (c_len, c_len), 0)
    lci = lax.broadcasted_iota(jnp.int32, (c_len, c_len), 1)
    l_cum = (lri >= lci).astype(BF16)
    lane_head = lax.broadcasted_iota(jnp.int32, (1, LANES), 1) // HEAD
    lane_head2 = jnp.concatenate([lane_head, lane_head], axis=1)
    bri = lax.broadcasted_iota(jnp.int32, (LANES, LANES), 0)
    bci = lax.broadcasted_iota(jnp.int32, (LANES, LANES), 1)
    same_head = (bri // HEAD) == (bci // HEAD)
    same_head2 = jnp.concatenate([same_head, same_head], axis=1)
    eye = bri == bci
    zc = jnp.zeros((c_len, LANES), F32)
    nt_dims = (((1,), (1,)), ((), ()))

    def precompute(gi, between_stages):
        chunks = [gi * group + cc for cc in range(group)]
        rows = [pl.ds(pl.multiple_of(c * c_len, c_len), c_len) for c in chunks]
        lws = [lw_s[rw, :] for rw in rows]
        gcums = [_dot_exact_lhs(l_cum, lw) for lw in lws]
        g_ends = [g[c_len - 1:c_len, :] for g in gcums]
        kts = [kap_s[rw, :] * jnp.exp(g - lw) for rw, g, lw in zip(rows, gcums, lws)]
        rts = [r_s[rw, :] * jnp.exp(g) for rw, g in zip(rows, gcums)]
        e_negs = [jnp.exp(-g) for g in gcums]
        lhss = [jnp.concatenate([kt, rt], axis=0) for kt, rt in zip(kts, rts)]
        o_bs = [lax.dot_general(lhs, _head_split(bet_s[rw, :] * en, lane_head), nt_dims, preferred_element_type=F32)
                for lhs, rw, en in zip(lhss, rows, e_negs)]
        o_ks = [lax.dot_general(lhs, _head_split(k_s[rw, :] * en, lane_head), nt_dims, preferred_element_type=F32)
                for lhs, rw, en in zip(lhss, rows, e_negs)]
        n_mats = [jnp.where(strict, o[0:c_len], 0.0) for o in o_bs]
        v_bds = [_head_split(v_s[rw, :], lane_head) for rw in rows]
        avs = [jnp.dot(jnp.where(strict, o[0:c_len], 0.0), v_bd, preferred_element_type=F32)
               for o, v_bd in zip(o_ks, v_bds)]
        rhss = [jnp.concatenate([kt, av], axis=1) for kt, av in zip(kts, avs)]
        a_rs = [jnp.concatenate([jnp.where(incl, ob[c_len:], 0.0), jnp.where(incl, ok[c_len:], 0.0)], axis=1)
                for ob, ok in zip(o_bs, o_ks)]
        e_hats = [jnp.exp(ge - g) for ge, g in zip(g_ends, gcums)]
        bk_ts = [jnp.concatenate([bet_s[rw, :] * eh, k_s[rw, :] * eh], axis=0).T for rw, eh in zip(rows, e_hats)]
        xs = _solve_unit_lower(n_mats, rhss, lane_head2, between_stages)
        wus = [-x for x in xs]
        tops = [jnp.dot(a_r, jnp.concatenate([_head_split(wu, lane_head2),
                                              jnp.concatenate([jnp.zeros((LANES, LANES), F32), v_bd], axis=1)],
                                             axis=0), preferred_element_type=F32)
                for a_r, wu, v_bd in zip(a_rs, wus, v_bds)]
        bots = [jnp.dot(bk_t, jnp.concatenate([wu, jnp.concatenate([zc, v_s[rw, :]], axis=1)], axis=0),
                        preferred_element_type=F32)
                for bk_t, wu, rw in zip(bk_ts, wus, rows)]
        for c, top, bot, rt, ge in zip(chunks, tops, bots, rts, g_ends):
            bot = jnp.where(same_head2, bot, 0.0)
            base = pl.multiple_of(c * (c_len + LANES), SUBLANES)
            lm_s[pl.ds(base, c_len), :] = top[:, 0:LANES] + rt
            lm_s[pl.ds(base + c_len, LANES), :] = bot[:, 0:LANES] + jnp.where(eye, jnp.exp(ge), 0.0)
            yb_s[pl.ds(base, c_len), :] = top[:, LANES:]
            yb_s[pl.ds(base + c_len, LANES), :] = bot[:, LANES:]

    def advance(c, h_bd):
        base = pl.multiple_of(c * (c_len + LANES), SUBLANES)
        res = _dot_3pass(lm_s[pl.ds(base, c_len + LANES), :], h_bd) + yb_s[pl.ds(base, c_len + LANES), :]
        y_s[pl.ds(pl.multiple_of(c * c_len, c_len), c_len), :] = res[0:c_len]
        return res[c_len:]

    assert group <= c_len // SUBLANES
    n_groups = t_len // (c_len * group)
    precompute(jnp.int32(0), None)

    def fused(gi, h):
        state = [h]

        def between_stages(blk):
            if blk < group:
                state[0] = advance((gi - 1) * group + blk, state[0])

        precompute(gi, between_stages)
        return state[0]

    h_bd = lax.fori_loop(1, n_groups, fused, jnp.zeros((LANES, LANES), F32))
    for blk in range(group):
        h_bd = advance(jnp.int32((n_groups - 1) * group + blk), h_bd)
    s_bd = h_bd.T
    s_ref[0, 0] = s_bd[0:HEAD, 0:HEAD]
    s_ref[0, 1] = s_bd[HEAD:, HEAD:]

    def phase3(c, carry):
        rows = pl.ds(pl.multiple_of(c * rc, rc), rc)
        o = _rwkv_post(y_s[rows, :], bon_s[rows, :], g_s[rows, :], gnw_ref[...], gnb_ref[...], ones_bd)
        ob_ref[rows, :] = o.astype(ob_ref.dtype)
        return carry

    lax.fori_loop(0, n_ch, phase3, 0)


def _rwkv_specs(row_block, rows, q_rows, lay):
    pw, pa, pg, nb = lay["pw"], lay["pa"], lay["pg"], lay["nb"]

    def im(col_fn, on_rows=False):
        if row_block is None:
            return lambda b, hp: (b if on_rows else 0, col_fn(hp))
        return lambda hp: (row_block if on_rows else 0, col_fn(hp))

    def triple(n_rows, on_rows, c_r, c_k, c_v, c_wd, c_ad, c_gd):
        return [
            pl.BlockSpec((n_rows, LANES), im(lambda hp: c_r + hp, on_rows)),
            pl.BlockSpec((n_rows, LANES), im(lambda hp: c_k + hp, on_rows)),
            pl.BlockSpec((n_rows, LANES), im(lambda hp: c_v + hp, on_rows)),
            pl.BlockSpec((n_rows, pw * LANES), im(lambda hp: c_wd // pw, on_rows)),
            pl.BlockSpec((n_rows, pa * LANES), im(lambda hp: c_ad // pa, on_rows)),
            pl.BlockSpec((n_rows, pg * LANES), im(lambda hp: c_gd // pg, on_rows)),
        ]

    specs = triple(rows, True, lay["r"], lay["k"], lay["v"], lay["wd"], lay["ad"], lay["gd"])
    if q_rows:
        specs += triple(q_rows, False, 0, nb, 2 * nb, 0, pw, pw + pa)
    specs += triple(1, False, 0, nb, 2 * nb, 0, pw, pw + pa)
    per_pair = im(lambda hp: hp)
    specs += [
        pl.BlockSpec((1, LANES), per_pair),
        pl.BlockSpec((pw * LANES, LANES), per_pair),
        pl.BlockSpec((1, LANES), per_pair),
        pl.BlockSpec((pa * LANES, LANES), per_pair),
        pl.BlockSpec((pg * LANES, LANES), per_pair),
        pl.BlockSpec((1, LANES), per_pair),
        pl.BlockSpec((1, LANES), per_pair),
        pl.BlockSpec((1, LANES), per_pair),
        pl.BlockSpec((1, LANES), per_pair),
        pl.BlockSpec((1, LANES), per_pair),
    ]
    return specs


def _rwkv_prompt(p_main, p_tail, mu_rkv, mu_lora, params, *, bp, tp, d, lay):
    n_hp = d // LANES
    pw, pa, pg = lay["pw"], lay["pa"], lay["pg"]
    rc = 256 if tp % 256 == 0 else tp
    n_heads = d // HEAD
    n_chunks = tp // CHUNK
    group = next(g for g in (8, 4, 2, 1) if n_chunks % g == 0)
    kern = functools.partial(_rwkv_prompt_kernel, t_len=tp, rc=rc, group=group)
    vm = pltpu.VMEM((tp, LANES), F32)
    return pl.pallas_call(
        kern,
        grid=(bp, n_hp),
        in_specs=_rwkv_specs(None, tp, 0, lay),
        out_specs=[
            pl.BlockSpec((tp, LANES), lambda b, hp: (b, hp)),
            pl.BlockSpec((1, 2, HEAD, HEAD), lambda b, hp: (b, hp, 0, 0)),
        ],
        out_shape=[
            jax.ShapeDtypeStruct((bp * tp, d), BF16),
            jax.ShapeDtypeStruct((bp, n_heads, HEAD, HEAD), F32),
        ],
        scratch_shapes=[vm] * 9 + [
            pltpu.VMEM((tp, pw * LANES), BF16),
            pltpu.VMEM((tp, pa * LANES), BF16),
            pltpu.VMEM((tp, pg * LANES), BF16),
            pltpu.VMEM((n_chunks * (CHUNK + LANES), LANES), F32),
            pltpu.VMEM((n_chunks * (CHUNK + LANES), LANES), F32),
        ],
        compiler_params=_cparams(2),
        name="rwkv_prompt",
    )(p_main, p_main, p_main, p_tail, p_tail, p_tail,
      mu_rkv, mu_rkv, mu_rkv, mu_lora, mu_lora, mu_lora, *params)


def _rwkv_sample_kernel(pr_ref, pk_ref, pv_ref, pwd_ref, pad_ref, pgd_ref,
                        qr_ref, qk_ref, qv_ref, qwd_ref, qad_ref, qgd_ref,
                        mur_ref, muk_ref, muv_ref, muwd_ref, muad_ref, mugd_ref,
                        w0_ref, w2_ref, a0_ref, a2_ref, g2_ref, kkw_ref, kaw_ref, rkw_ref, gnw_ref, gnb_ref,
                        st_ref, ob_ref, so_ref, y_s, *, ts, bs):
    ones_bd = _head_ones()
    refs = (pr_ref, pk_ref, pv_ref, pwd_ref, pad_ref, pgd_ref)
    prevs = (qr_ref, qk_ref, qv_ref, qwd_ref, qad_ref, qgd_ref)
    mus = (mur_ref, muk_ref, muv_ref, muwd_ref, muad_ref, mugd_ref)
    zs = []
    for ref, q, mu in zip(refs, prevs, mus):
        p = ref[...].astype(F32)
        pp = jnp.concatenate([q[...].astype(F32), p[0:(ts - 1) * bs, :]], axis=0)
        zs.append(p + mu[...] * (pp - p))
    zr, zk, zv, zwd, zad, zgd = zs
    lw, g, kk, k2, beta, bonus = _rwkv_prep(
        zr, zk, zv, jnp.tanh(zwd).astype(BF16), zad.astype(BF16), jax.nn.sigmoid(zgd).astype(BF16),
        w0_ref[...], w2_ref[...], a0_ref[...], a2_ref[...], g2_ref[...],
        kkw_ref[...], kaw_ref[...], rkw_ref[...], ones_bd)
    w_dec = jnp.exp(lw)

    for t in range(ts):
        rows = slice(t * bs, (t + 1) * bs)
        wt, kkt, bt, kt, rt, vt = (a[rows, :].T for a in (w_dec, kk, beta, k2, zr, zv))
        for h in range(2):
            lo = h * HEAD
            hd = slice(lo, lo + HEAD)
            s = (st_ref if t == 0 else so_ref)[h]
            sa = -jnp.sum(s * kkt[hd][None], axis=1, keepdims=True)
            s = s * wt[hd][None] + sa * bt[hd][None] + vt[hd][:, None, :] * kt[hd][None]
            so_ref[h] = s
            y_s[t, hd, :] = jnp.sum(s * rt[hd][None], axis=1)
    for t in range(ts):
        yt = y_s[t].T
        rows = slice(t * bs, (t + 1) * bs)
        o = _rwkv_post(yt, bonus[rows], g[rows], gnw_ref[...], gnb_ref[...], ones_bd)
        ob_ref[rows, :] = o.astype(ob_ref.dtype)


def _rwkv_sample(p_main, p_tail, q_rkv, q_lora, mu_rkv, mu_lora, params, st, *, ts, bs, d, mp, lay):
    n_hp = d // LANES
    rows = ts * bs
    rblk = mp // rows
    in_specs = _rwkv_specs(rblk, rows, bs, lay)
    in_specs.append(pl.BlockSpec((2, HEAD, HEAD, bs), lambda hp: (hp, 0, 0, 0)))
    kern = functools.partial(_rwkv_sample_kernel, ts=ts, bs=bs)
    return pl.pallas_call(
        kern,
        grid=(n_hp,),
        in_specs=in_specs,
        out_specs=[
            pl.BlockSpec((rows, LANES), lambda hp: (0, hp)),
            pl.BlockSpec((2, HEAD, HEAD, bs), lambda hp: (hp, 0, 0, 0)),
        ],
        out_shape=[
            jax.ShapeDtypeStruct((rows, d), BF16),
            jax.ShapeDtypeStruct(st.shape, F32),
        ],
        scratch_shapes=[pltpu.VMEM((ts, LANES, bs), F32)],
        compiler_params=_cparams(1),
        name="rwkv_sample",
    )(p_main, p_main, p_main, p_tail, p_tail, p_tail, q_rkv, q_rkv, q_rkv, q_lora, q_lora, q_lora,
      mu_rkv, mu_rkv, mu_rkv, mu_lora, mu_lora, mu_lora, *params, st)


def _pad_cols(a, width):
    return jnp.pad(a, ((0, 0), (0, width - a.shape[1])))


def _pad_rows(a, height):
    return jnp.pad(a, ((0, height - a.shape[0]), (0, 0)))


def _tiles(d, d_ff, n_tail):
    pick = lambda n, cands: next(c for c in cands if n % c == 0)
    return dict(
        tm=512, tm_wide=256,
        main=pick(5 * d, (1280, 1024, 512, 256, 128)),
        tail=pick(n_tail, (2304, 1536, 1024, 512, 256, 128)),
        prev=pick(d, (1024, 512, 256, 128)),
        up=pick(d_ff, (1024, 512, 256, 128)),
        down=pick(d, (512, 256, 128)),
    )


def kernel(x_prompt, x_sample, p_prompt, p_sample, state_rg_h, state_rg_conv, state_rwkv, state_shift,
           norm_mix, w_in, conv_w, conv_b, rg_wa, rg_ba, rg_wx, rg_bx, rg_lam, w_rg_o,
           mu_shift, rw_w0, rw_w2, rw_a0, rw_a2, rw_g2, rw_kk, rw_ka, rw_rk, rw_gn_w, rw_gn_b,
           w_rw_o, w_o, norm_ffn, w_up, w_down, norm_ple, w_ple_gate, w_ple, norm_f):
    bp, tp, d = x_prompt.shape
    bs, ts, _ = x_sample.shape
    depth = w_in.shape[0]
    n_heads, head = rw_rk.shape[1], rw_rk.shape[2]
    r_w, r_a, r_g = rw_w2.shape[1], rw_a2.shape[1], rw_g2.shape[1]
    d_ple = w_ple.shape[1]
    d_ff = w_up.shape[2]
    n_tap = conv_w.shape[1]
    assert head == HEAD and n_heads * HEAD == d and d % LANES == 0 and bs == LANES
    assert rg_wa.shape[2] == LANES and tp % CHUNK == 0 and n_tap == 4
    mp, ms = bp * tp, bs * ts
    nb = d // LANES
    pw, pa, pg = (_round_up(r, LANES) // LANES for r in (r_w, r_a, r_g))
    n_lora = (pw + pa + pg) * LANES
    n_tail = 2 * d + n_lora
    lay = dict(u=0, g=nb, r=2 * nb, k=3 * nb, v=4 * nb, wd=2 * nb, ad=2 * nb + pw, gd=2 * nb + pw + pa,
               pw=pw, pa=pa, pg=pg, nb=nb)
    t = _tiles(d, d_ff, n_tail)
    tm = t["tm"]
    assert mp % tm == 0 and ms % tm == 0 and mp % (ts * bs) == 0 and ms % t["tm_wide"] == 0
    assert lay["wd"] % pw == 0 and lay["ad"] % pa == 0 and lay["gd"] % pg == 0 and pw % pa == 0 and (pw + pa) % pg == 0
    assert (2 * d) % t["prev"] == 0 and (2 * d) % 512 == 0 and n_lora % 512 == 0 and d % 512 == 0
    mm = functools.partial(_mm, mp=mp, ms=ms)

    x = _Rows(x_prompt.reshape(mp, d), jnp.transpose(x_sample, (1, 0, 2)).reshape(ms, d))
    hp_l, cp_l, sp_l, xp_l, hs_l, cs_l, ss_l, xs_l = [], [], [], [], [], [], [], []
    y_p = y_s = None
    for i in range(depth):
        o_rw = 2 * d
        o_lora = o_rw + 3 * d
        o_g = o_lora + r_w + r_a + r_g
        wi = w_in[i]

        def lora_cols(a):
            return jnp.concatenate([
                _pad_cols(a[:, o_lora:o_lora + r_w], pw * LANES),
                _pad_cols(a[:, o_lora + r_w:o_lora + r_w + r_a], pa * LANES),
                _pad_cols(a[:, o_lora + r_w + r_a:o_g], pg * LANES)], axis=1)

        w_tail = jnp.concatenate([wi[:, o_g:], lora_cols(wi)], axis=1).astype(BF16)
        mu_all = jnp.concatenate([jnp.zeros((1, o_rw), F32), mu_shift[i][None], jnp.zeros((1, 2 * d), F32)], axis=1)
        mu_rkv = mu_shift[i][None, :3 * d]
        mu_lora = lora_cols(mu_all)
        rw_params = (
            rw_w0[i][None], _pad_rows(rw_w2[i], pw * LANES).astype(BF16),
            rw_a0[i][None], _pad_rows(rw_a2[i], pa * LANES).astype(BF16),
            _pad_rows(rw_g2[i], pg * LANES).astype(BF16),
            rw_kk[i][None], rw_ka[i][None], rw_rk[i].reshape(1, d), rw_gn_w[i][None], rw_gn_b[i][None])
        rg_params = (conv_w[i], conv_b[i][None], rg_wa[i].astype(BF16), rg_ba[i][:, None, :],
                     rg_wx[i].astype(BF16), rg_bx[i][:, None, :], rg_lam[i][None])

        xn = _Rows(_rmsnorm(x, norm_mix[i], BF16, tm, mp))
        xa = x.arrays if len(x.arrays) == 2 else (x.arrays[0][:mp], x.arrays[0][mp:])
        x_last = jnp.concatenate([xa[0].reshape(bp, tp, d)[:, -1], xa[1][(ts - 1) * bs:]], axis=0)
        n_last = _round_up(bp + bs, SUBLANES)
        xn_last = _rmsnorm(_Rows(_pad_rows(x_last, n_last)), norm_mix[i], F32, n_last, n_last)
        (p_main,) = mm([xn], [wi], [], _epi_plain, [(F32, False)], tm=tm, tn=t["main"], n_out=5 * d,
                       cast_w=True, name="in_proj")
        (p_tail,) = mm([xn], [w_tail], [], _epi_plain, [(BF16, False)], tm=tm, tn=t["tail"], n_out=n_tail,
                       name="in_proj_tail")
        xprev = _Rows(state_shift[i])
        (q_rkv,) = _mm([xprev], [wi], [], _epi_plain, [(F32, False)], mp=bs, ms=0, tm=bs, tn=t["prev"],
                       n_out=3 * d, w_col_off=o_rw // t["prev"], cast_w=True, name="prev_proj")
        (q_lora,) = _mm([xprev], [w_tail], [], _epi_plain, [(BF16, False)], mp=bs, ms=0, tm=bs, tn=512,
                        n_out=n_lora, w_col_off=2 * d // 512, name="prev_proj_tail")

        cwid = 2 * LANES if nb % 2 == 0 else LANES
        hg_p, h_p, c_p = _rg_prompt(p_main, *rg_params, bp=bp, tp=tp, d=d, cb_u=lay["u"], cb_g=lay["g"], cwid=cwid)
        cbuf_t = jnp.transpose(state_rg_conv[i], (1, 0, 2))
        hg_s, h_s, c_s = _rg_sample(p_main, cbuf_t, state_rg_h[i], *rg_params, ts=ts, bs=bs, d=d, mp=mp,
                                    cb_u=lay["u"], cb_g=lay["g"], cwid=cwid)

        ob_p, s_p = _rwkv_prompt(p_main, p_tail, mu_rkv, mu_lora, rw_params, bp=bp, tp=tp, d=d, lay=lay)
        st = jnp.transpose(state_rwkv[i], (1, 2, 3, 0))
        ob_s, st_new = _rwkv_sample(p_main, p_tail, q_rkv, q_lora, mu_rkv, mu_lora, rw_params, st,
                                    ts=ts, bs=bs, d=d, mp=mp, lay=lay)
        s_s = jnp.transpose(st_new, (3, 0, 1, 2))

        tw = t["tm_wide"]
        (merged,) = mm([_Rows(hg_p, hg_s), _Rows(ob_p, ob_s)], [w_rg_o[i].astype(BF16), w_rw_o[i].astype(BF16)],
                       [("tile", _Rows(p_tail), 0), ("tile", _Rows(p_tail), 1)],
                       _epi_merge, [(BF16, False)], tm=tw, tn=d, n_out=d, single_buffer_w=True, name="merge")
        x1, xn2 = mm([_Rows(merged)], [w_o[i].astype(BF16)], [("tile", x, 0), ("row", norm_ffn[i][None], 0)],
                     _epi_residual_norm, [(F32, False), (BF16, False)], tm=tw, tn=d, n_out=d, name="out_proj")
        (hf,) = mm([_Rows(xn2)], [w_up[i]], [], _epi_relu2, [(BF16, False)], tm=tm, tn=t["up"], n_out=d_ff,
                   cast_w=True, name="mlp_up")
        (x2,) = mm([_Rows(hf)], [w_down[i].astype(BF16)], [("tile", _Rows(x1), 0)], _epi_residual, [(F32, False)],
                   tm=tm, tn=t["down"], n_out=d, name="mlp_down")
        xn3 = _rmsnorm(_Rows(x2), norm_ple[i], BF16, tm, mp)
        p_rows = _Rows(p_prompt[i].reshape(mp, d_ple), jnp.transpose(p_sample[i], (1, 0, 2)).reshape(ms, d_ple))
        ple_ws = [w_ple_gate[i].astype(BF16), w_ple[i].astype(BF16)]
        if i == depth - 1:
            y_p, y_s = mm([_Rows(xn3), p_rows], ple_ws, [("tile", _Rows(x2), 0), ("row", norm_f[None], 0)],
                          _epi_ple_final, [(F32, True)], tm=tw, tn=d, n_out=d, name="ple_final")
        else:
            x = _Rows(*mm([_Rows(xn3), p_rows], ple_ws, [("tile", _Rows(x2), 0)], _epi_ple, [(F32, False)],
                          tm=tw, tn=d, n_out=d, name="ple"))

        hp_l.append(h_p[:, 0])
        cp_l.append(c_p[:, SUBLANES - (n_tap - 1):])
        sp_l.append(s_p)
        xp_l.append(xn_last[:bp])
        hs_l.append(h_s)
        cs_l.append(jnp.transpose(c_s, (1, 0, 2)))
        ss_l.append(s_s)
        xs_l.append(xn_last[bp:bp + bs])

    y_prompt = y_p.reshape(bp, tp, d)
    y_sample = jnp.transpose(y_s.reshape(ts, bs, d), (1, 0, 2))
    return (y_prompt, y_sample,
            jnp.stack(hp_l), jnp.stack(cp_l), jnp.stack(sp_l), jnp.stack(xp_l),
            jnp.stack(hs_l), jnp.stack(cs_l), jnp.stack(ss_l), jnp.stack(xs_l))
```

```python
import functools

import jax
import jax.numpy as jnp
from jax import lax
from jax.experimental import pallas as pl
from jax.experimental.pallas import tpu as pltpu

F32 = jnp.float32
BF16 = jnp.bfloat16
LANES = 128
SUBLANES = 8
HEAD = 64
CHUNK = 64
EPS = 1e-6
GN_EPS = 64e-5
RG_C = 8.0
VMEM_LIMIT = 56 * 1024 * 1024


def _cparams(n_grid, vmem=VMEM_LIMIT):
    return pltpu.CompilerParams(dimension_semantics=("arbitrary",) * n_grid, vmem_limit_bytes=vmem)


def _round_up(x, m):
    return (x + m - 1) // m * m


def _rms(x, g):
    ms = jnp.mean(x * x, axis=-1, keepdims=True)
    return (x * lax.rsqrt(ms + EPS)) * g


class _Rows:
    def __init__(self, *arrays):
        self.arrays = arrays

    @property
    def width(self):
        return self.arrays[0].shape[1]

    def specs(self, tm, n_p, col):
        if len(self.arrays) == 1:
            return [lambda wid: pl.BlockSpec((tm, wid), lambda n, i: (i, col(n)))]
        return [lambda wid: pl.BlockSpec((tm, wid), lambda n, i: (jnp.minimum(i, n_p - 1), col(n))),
                lambda wid: pl.BlockSpec((tm, wid), lambda n, i: (jnp.maximum(i - n_p, 0), col(n)))]


def _load_rows(refs, is_prompt):
    if len(refs) == 1:
        return refs[0][...]
    return jnp.where(is_prompt, refs[0][...], refs[1][...])


def _rmsnorm_kernel(*refs, n_src, n_p):
    g_ref, o_ref = refs[n_src], refs[n_src + 1]
    x = _load_rows(refs[:n_src], pl.program_id(1) < n_p)
    o_ref[...] = _rms(x, g_ref[...]).astype(o_ref.dtype)


def _rmsnorm(x, g, out_dtype, tm, mp):
    d = x.width
    m = sum(a.shape[0] for a in x.arrays)
    n_p = mp // tm
    in_specs = [mk(d) for mk in x.specs(tm, n_p, lambda n: 0)] + [pl.BlockSpec((1, d), lambda n, i: (0, 0))]
    return pl.pallas_call(
        functools.partial(_rmsnorm_kernel, n_src=len(x.arrays), n_p=n_p),
        grid=(1, m // tm),
        in_specs=in_specs,
        out_specs=pl.BlockSpec((tm, d), lambda n, i: (i, 0)),
        out_shape=jax.ShapeDtypeStruct((m, d), out_dtype),
        compiler_params=_cparams(2),
        name="rmsnorm",
    )(*x.arrays, g.reshape(1, d))


def _mm_kernel(*refs, x_counts, n_w, cast_w, w_transposed, extra_counts, out_counts, n_p, epilogue):
    pos = 0
    x_refs = []
    for cnt in x_counts:
        x_refs.append(refs[pos:pos + cnt])
        pos += cnt
    w_refs = refs[pos:pos + n_w]
    pos += n_w
    e_refs = []
    for cnt in extra_counts:
        e_refs.append(refs[pos:pos + cnt])
        pos += cnt
    o_refs = []
    for cnt in out_counts:
        o_refs.append(refs[pos:pos + cnt])
        pos += cnt
    w_scratch = refs[pos:]
    i = pl.program_id(1)
    is_prompt = i < n_p
    if cast_w:
        @pl.when(i == 0)
        def _():
            for w_ref, s_ref in zip(w_refs, w_scratch):
                s_ref[...] = w_ref[...].astype(BF16)
        ws = [s[...] for s in w_scratch]
    else:
        ws = [w[...] for w in w_refs]
    dims = (((1,), (1 if w_transposed else 0,)), ((), ()))
    accs = [lax.dot_general(_load_rows(xr, is_prompt).astype(BF16), w, dims, preferred_element_type=F32)
            for xr, w in zip(x_refs, ws)]
    outs = epilogue(accs, [_load_rows(er, is_prompt) for er in e_refs])
    for refs_o, tile in zip(o_refs, outs):
        if len(refs_o) == 1:
            refs_o[0][...] = tile.astype(refs_o[0].dtype)
        else:
            @pl.when(is_prompt)
            def _():
                refs_o[0][...] = tile.astype(refs_o[0].dtype)

            @pl.when(jnp.logical_not(is_prompt))
            def _():
                refs_o[1][...] = tile.astype(refs_o[1].dtype)


def _mm(xs, ws, extras, epilogue, outs, *, mp, ms, tm, tn, n_out, w_col_off=0, cast_w=False,
        w_transposed=False, single_buffer_w=False, name="mm"):
    n_p = mp // tm
    m = mp + ms
    grid = (n_out // tn, m // tm)
    in_specs, args, x_counts, extra_counts = [], [], [], []
    for x in xs:
        in_specs += [mk(x.width) for mk in x.specs(tm, n_p, lambda n: 0)]
        args += list(x.arrays)
        x_counts.append(len(x.arrays))
    w_mode = dict(pipeline_mode=pl.Buffered(1)) if single_buffer_w else {}
    if w_transposed:
        w_blocks = [(tn, w.shape[1]) for w in ws]
        in_specs += [pl.BlockSpec(blk, lambda n, i: (n + w_col_off, 0), **w_mode) for blk in w_blocks]
    else:
        w_blocks = [(w.shape[0], tn) for w in ws]
        in_specs += [pl.BlockSpec(blk, lambda n, i: (0, n + w_col_off), **w_mode) for blk in w_blocks]
    args += list(ws)
    for kind, src, off in extras:
        if kind == "tile":
            in_specs += [mk(tn) for mk in src.specs(tm, n_p, lambda n, off=off: n + off)]
            args += list(src.arrays)
            extra_counts.append(len(src.arrays))
        elif kind == "row":
            in_specs.append(pl.BlockSpec((1, tn), lambda n, i, off=off: (0, n + off)))
            args.append(src)
            extra_counts.append(1)
        else:
            raise ValueError(kind)
    out_specs, out_shape, out_counts = [], [], []
    for dtype, split in outs:
        if split:
            out_specs += [pl.BlockSpec((tm, tn), lambda n, i: (jnp.minimum(i, n_p - 1), n)),
                          pl.BlockSpec((tm, tn), lambda n, i: (jnp.maximum(i - n_p, 0), n))]
            out_shape += [jax.ShapeDtypeStruct((mp, n_out), dtype), jax.ShapeDtypeStruct((ms, n_out), dtype)]
            out_counts.append(2)
        else:
            out_specs.append(pl.BlockSpec((tm, tn), lambda n, i: (i, n)))
            out_shape.append(jax.ShapeDtypeStruct((m, n_out), dtype))
            out_counts.append(1)
    kern = functools.partial(_mm_kernel, x_counts=tuple(x_counts), n_w=len(ws), cast_w=cast_w,
                             w_transposed=w_transposed, extra_counts=tuple(extra_counts),
                             out_counts=tuple(out_counts), n_p=n_p, epilogue=epilogue)
    scratch = [pltpu.VMEM(blk, BF16) for blk in w_blocks] if cast_w else []
    res = pl.pallas_call(
        kern,
        grid=grid,
        in_specs=in_specs,
        out_specs=out_specs,
        out_shape=out_shape,
        scratch_shapes=scratch,
        compiler_params=_cparams(2),
        name=name,
    )(*args)
    return res


def _epi_plain(accs, extras):
    return (accs[0],)


def _epi_merge(accs, extras):
    ga, gb = extras
    return (jax.nn.sigmoid(ga.astype(F32)) * accs[0] + jax.nn.sigmoid(gb.astype(F32)) * accs[1],)


def _epi_residual(accs, extras):
    return (extras[0] + accs[0],)


def _epi_residual_norm(accs, extras):
    x = extras[0] + accs[0]
    return x, _rms(x, extras[1])


def _epi_relu2(accs, extras):
    h = jnp.maximum(accs[0], 0.0)
    return (h * h,)


def _epi_ple(accs, extras):
    return (extras[0] + jax.nn.sigmoid(accs[0]) * accs[1],)


def _epi_ple_final(accs, extras):
    return (_rms(extras[0] + jax.nn.sigmoid(accs[0]) * accs[1], extras[1]),)


def _softplus(x):
    return jnp.maximum(x, 0.0) + jnp.log1p(jnp.exp(-jnp.abs(x)))


def _split_bf16(x, parts):
    out = []
    rem = x
    for _ in range(parts):
        p = rem.astype(BF16)
        out.append(p)
        rem = rem - p.astype(F32)
    return out


def _dot_exact_lhs(a_bf16, x, parts=3):
    acc = None
    for p in _split_bf16(x, parts):
        t = jnp.dot(a_bf16, p, preferred_element_type=F32)
        acc = t if acc is None else acc + t
    return acc


def _dot_exact_rhs(x, b_bf16, parts=2):
    acc = None
    for p in _split_bf16(x, parts):
        t = jnp.dot(p, b_bf16, preferred_element_type=F32)
        acc = t if acc is None else acc + t
    return acc


def _dot_3pass(a, b):
    a_hi, a_lo = _split_bf16(a, 2)
    b_hi, b_lo = _split_bf16(b, 2)
    return (jnp.dot(a_hi, b_hi, preferred_element_type=F32)
            + jnp.dot(a_hi, b_lo, preferred_element_type=F32)
            + jnp.dot(a_lo, b_hi, preferred_element_type=F32))


def _head_ones():
    r = lax.broadcasted_iota(jnp.int32, (LANES, LANES), 0) // HEAD
    c = lax.broadcasted_iota(jnp.int32, (LANES, LANES), 1) // HEAD
    return (r == c).astype(BF16)


def _rwkv_prep(zr, zk, zv, tw, za, sg, w0, w2, a0, a2, g2, kkw, kaw, rkw, ones_bd):
    wlog = -_softplus(-(w0 + jnp.dot(tw, w2, preferred_element_type=F32))) - 0.5
    lw = -jnp.exp(wlog)
    a = jax.nn.sigmoid(a0 + jnp.dot(za, a2, preferred_element_type=F32))
    g = jnp.dot(sg, g2, preferred_element_type=F32)
    kk = zk * kkw
    ss = _dot_exact_rhs(kk * kk, ones_bd)
    kk = kk * lax.rsqrt(jnp.maximum(ss, 1e-24))
    k2 = zk * (1.0 + (a - 1.0) * kaw)
    beta = kk * a
    bonus = _dot_exact_rhs(zr * k2 * rkw, ones_bd) * zv
    return lw, g, kk, k2, beta, bonus


def _rwkv_post(y, bonus, g, gnw, gnb, ones_bd):
    mu = _dot_exact_rhs(y, ones_bd) * (1.0 / HEAD)
    d = y - mu
    var = _dot_exact_rhs(d * d, ones_bd) * (1.0 / HEAD)
    yn = (d * lax.rsqrt(var + GN_EPS)) * gnw + gnb
    return (yn + bonus) * g


def _rg_gates(xc, wa, ba, wx, bx, sp):
    xcb = xc.astype(BF16)
    r = jax.nn.sigmoid(jnp.dot(xcb, wa, preferred_element_type=F32) + ba)
    i = jax.nn.sigmoid(jnp.dot(xcb, wx, preferred_element_type=F32) + bx)
    log_a = (-RG_C * r) * sp
    a = jnp.exp(log_a)
    mult = jnp.sqrt(-jnp.tanh(log_a) * (a * a + 1.0))
    return a, mult, i * xc


def _rg_prompt_kernel(u_ref, gt_ref, cw_ref, cb_ref, wa_ref, ba_ref, wx_ref, bx_ref, lam_ref,
                      hg_ref, hl_ref, cs_ref, us_ref, a_ref, b_ref, *, t_len, rc):
    cwid = u_ref.shape[1]
    nb = cwid // LANES
    us_ref[0:SUBLANES, :] = jnp.zeros((SUBLANES, cwid), F32)
    us_ref[SUBLANES:, :] = u_ref[...]
    sp = _softplus(-lam_ref[...])
    w0 = cw_ref[0:1, :]
    w1 = cw_ref[1:2, :]
    w2 = cw_ref[2:3, :]
    w3 = cw_ref[3:4, :]
    cb = cb_ref[...]
    n_ch = t_len // rc

    def phase1(c, carry):
        r0 = pl.multiple_of(c * rc, rc)
        e = us_ref[pl.ds(r0, rc + SUBLANES), :]
        u0 = e[SUBLANES:]
        u1 = pltpu.roll(e, 1, 0)[SUBLANES:]
        u2 = pltpu.roll(e, 2, 0)[SUBLANES:]
        u3 = pltpu.roll(e, 3, 0)[SUBLANES:]
        xc = cb + (u3 * w0 + u2 * w1 + u1 * w2 + u0 * w3)
        row = lax.broadcasted_iota(jnp.int32, (rc, LANES), 0) + r0
        for n in range(nb):
            sl = slice(n * LANES, (n + 1) * LANES)
            a, mult, ix = _rg_gates(xc[:, sl], wa_ref[n], ba_ref[n], wx_ref[n], bx_ref[n], sp[:, sl])
            mult = jnp.where(row == 0, 1.0, mult)
            a_ref[n, pl.ds(r0, rc), :] = a
            b_ref[n, pl.ds(r0, rc), :] = mult * ix
        return carry

    lax.fori_loop(0, n_ch, phase1, 0)

    rowi = lax.broadcasted_iota(jnp.int32, (SUBLANES, LANES), 0)

    def scan(i, h_prev):
        r0 = pl.multiple_of(i * SUBLANES, SUBLANES)
        out = []
        for n in range(nb):
            a = a_ref[n, pl.ds(r0, SUBLANES), :]
            b = b_ref[n, pl.ds(r0, SUBLANES), :]
            for d in (1, 2, 4):
                a_sh = jnp.where(rowi < d, 1.0, pltpu.roll(a, d, 0))
                b_sh = jnp.where(rowi < d, 0.0, pltpu.roll(b, d, 0))
                b = a * b_sh + b
                a = a * a_sh
            h = b + a * h_prev[n]
            b_ref[n, pl.ds(r0, SUBLANES), :] = h
            out.append(h[SUBLANES - 1:SUBLANES, :])
        return tuple(out)

    h_last = lax.fori_loop(0, t_len // SUBLANES, scan, tuple(jnp.zeros((1, LANES), F32) for _ in range(nb)),
                           unroll=4)
    hl_ref[0] = jnp.concatenate(list(h_last), axis=1)
    cs_ref[0] = u_ref[pl.ds(t_len - SUBLANES, SUBLANES), :]

    def phase3(c, carry):
        r0 = pl.multiple_of(c * rc, rc)
        for n in range(nb):
            sl = slice(n * LANES, (n + 1) * LANES)
            gate = gt_ref[pl.ds(r0, rc), sl]
            hg_ref[pl.ds(r0, rc), sl] = (b_ref[n, pl.ds(r0, rc), :] * jax.nn.gelu(gate)).astype(hg_ref.dtype)
        return carry

    lax.fori_loop(0, n_ch, phase3, 0)


def _rg_prompt(p, conv_w, conv_b, wa, ba, wx, bx, lam, *, bp, tp, d, cb_u, cb_g, cwid):
    nb = cwid // LANES
    nblk = d // cwid
    rc = 256 if tp % 256 == 0 else tp
    kern = functools.partial(_rg_prompt_kernel, t_len=tp, rc=rc)
    return pl.pallas_call(
        kern,
        grid=(bp, nblk),
        in_specs=[
            pl.BlockSpec((tp, cwid), lambda b, n: (b, cb_u * LANES // cwid + n)),
            pl.BlockSpec((tp, cwid), lambda b, n: (b, cb_g * LANES // cwid + n)),
            pl.BlockSpec((4, cwid), lambda b, n: (0, n)),
            pl.BlockSpec((1, cwid), lambda b, n: (0, n)),
            pl.BlockSpec((nb, LANES, LANES), lambda b, n: (n, 0, 0)),
            pl.BlockSpec((nb, 1, LANES), lambda b, n: (n, 0, 0)),
            pl.BlockSpec((nb, LANES, LANES), lambda b, n: (n, 0, 0)),
            pl.BlockSpec((nb, 1, LANES), lambda b, n: (n, 0, 0)),
            pl.BlockSpec((1, cwid), lambda b, n: (0, n)),
        ],
        out_specs=[
            pl.BlockSpec((tp, cwid), lambda b, n: (b, n)),
            pl.BlockSpec((1, 1, cwid), lambda b, n: (b, 0, n)),
            pl.BlockSpec((1, SUBLANES, cwid), lambda b, n: (b, 0, n)),
        ],
        out_shape=[
            jax.ShapeDtypeStruct((bp * tp, d), BF16),
            jax.ShapeDtypeStruct((bp, 1, d), F32),
            jax.ShapeDtypeStruct((bp, SUBLANES, d), F32),
        ],
        scratch_shapes=[
            pltpu.VMEM((tp + SUBLANES, cwid), F32),
            pltpu.VMEM((nb, tp, LANES), F32),
            pltpu.VMEM((nb, tp, LANES), F32),
        ],
        compiler_params=_cparams(2),
        name="rg_prompt",
    )(p, p, conv_w, conv_b, wa, ba, wx, bx, lam)


def _rg_sample_kernel(u_ref, gt_ref, cbuf_ref, h0_ref, cw_ref, cb_ref, wa_ref, ba_ref, wx_ref, bx_ref, lam_ref,
                      hg_ref, hl_ref, cs_ref, *, ts, bs):
    cwid = u_ref.shape[1]
    nb = cwid // LANES
    n_tap = cw_ref.shape[0]
    sp = _softplus(-lam_ref[...])
    cb = cb_ref[...]
    ext = [cbuf_ref[j] for j in range(n_tap - 1)] + [u_ref[t * bs:(t + 1) * bs, :] for t in range(ts)]
    for j in range(n_tap - 1):
        cs_ref[j] = ext[len(ext) - (n_tap - 1) + j]
    h = h0_ref[...]
    for t in range(ts):
        conv = ext[t] * cw_ref[0:1, :]
        for j in range(1, n_tap):
            conv = conv + ext[t + j] * cw_ref[j:j + 1, :]
        xc = cb + conv
        pieces = []
        for n in range(nb):
            sl = slice(n * LANES, (n + 1) * LANES)
            a, mult, ix = _rg_gates(xc[:, sl], wa_ref[n], ba_ref[n], wx_ref[n], bx_ref[n], sp[:, sl])
            pieces.append(a * h[:, sl] + mult * ix)
        h = pieces[0] if nb == 1 else jnp.concatenate(pieces, axis=1)
        gate = gt_ref[t * bs:(t + 1) * bs, :]
        hg_ref[t * bs:(t + 1) * bs, :] = (h * jax.nn.gelu(gate)).astype(hg_ref.dtype)
    hl_ref[...] = h


def _rg_sample(p, cbuf_t, h0, conv_w, conv_b, wa, ba, wx, bx, lam, *, ts, bs, d, mp, cb_u, cb_g, cwid):
    nb = cwid // LANES
    nblk = d // cwid
    rows = ts * bs
    rblk = mp // rows
    kern = functools.partial(_rg_sample_kernel, ts=ts, bs=bs)
    n_tap = conv_w.shape[0]
    return pl.pallas_call(
        kern,
        grid=(nblk,),
        in_specs=[
            pl.BlockSpec((rows, cwid), lambda n: (rblk, cb_u * LANES // cwid + n)),
            pl.BlockSpec((rows, cwid), lambda n: (rblk, cb_g * LANES // cwid + n)),
            pl.BlockSpec((n_tap - 1, bs, cwid), lambda n: (0, 0, n)),
            pl.BlockSpec((bs, cwid), lambda n: (0, n)),
            pl.BlockSpec((n_tap, cwid), lambda n: (0, n)),
            pl.BlockSpec((1, cwid), lambda n: (0, n)),
            pl.BlockSpec((nb, LANES, LANES), lambda n: (n, 0, 0)),
            pl.BlockSpec((nb, 1, LANES), lambda n: (n, 0, 0)),
            pl.BlockSpec((nb, LANES, LANES), lambda n: (n, 0, 0)),
            pl.BlockSpec((nb, 1, LANES), lambda n: (n, 0, 0)),
            pl.BlockSpec((1, cwid), lambda n: (0, n)),
        ],
        out_specs=[
            pl.BlockSpec((rows, cwid), lambda n: (0, n)),
            pl.BlockSpec((bs, cwid), lambda n: (0, n)),
            pl.BlockSpec((n_tap - 1, bs, cwid), lambda n: (0, 0, n)),
        ],
        out_shape=[
            jax.ShapeDtypeStruct((rows, d), BF16),
            jax.ShapeDtypeStruct((bs, d), F32),
            jax.ShapeDtypeStruct((n_tap - 1, bs, d), F32),
        ],
        compiler_params=_cparams(1),
        name="rg_sample",
    )(p, p, cbuf_t, h0, conv_w, conv_b, wa, ba, wx, bx, lam)


def _shift_rows(x, prev_row, rowi):
    return jnp.where(rowi == 0, prev_row, pltpu.roll(x, 1, 0))


def _head_split(z, lane_head):
    return jnp.concatenate([jnp.where(lane_head == 0, z, 0.0), jnp.where(lane_head == 1, z, 0.0)], axis=0)


def _solve_unit_lower(n_mats, rhss, lane_head, between_stages=None):
    n_sys = len(n_mats)
    c = n_mats[0].shape[0]
    width = rhss[0].shape[1]
    done = [[] for _ in range(n_sys)]
    for blk in range(c // SUBLANES):
        lo = blk * SUBLANES
        rs = []
        for i in range(n_sys):
            r = rhss[i][lo:lo + SUBLANES, :]
            if blk > 0:
                x_prev = jnp.concatenate(done[i] + [jnp.zeros((c - lo, width), F32)], axis=0)
                r = r - jnp.dot(n_mats[i][lo:lo + SUBLANES, :], _head_split(x_prev, lane_head),
                                preferred_element_type=F32)
            rs.append(r)
        if between_stages is not None:
            between_stages(blk)
        for j in range(SUBLANES - 1):
            for i in range(n_sys):
                nrow = n_mats[i][lo:lo + SUBLANES, :]
                mult = jnp.where(lane_head == 0, nrow[:, lo + j:lo + j + 1], nrow[:, HEAD + lo + j:HEAD + lo + j + 1])
                rs[i] = rs[i] - mult * rs[i][j:j + 1, :]
        for i in range(n_sys):
            done[i].append(rs[i])
    return [jnp.concatenate(b, axis=0) for b in done]


def _rwkv_prompt_kernel(pr_ref, pk_ref, pv_ref, pwd_ref, pad_ref, pgd_ref,
                        mur_ref, muk_ref, muv_ref, muwd_ref, muad_ref, mugd_ref,
                        w0_ref, w2_ref, a0_ref, a2_ref, g2_ref, kkw_ref, kaw_ref, rkw_ref, gnw_ref, gnb_ref,
                        ob_ref, s_ref,
                        r_s, lw_s, k_s, v_s, kap_s, bet_s, g_s, bon_s, y_s, tw_s, za_s, sg_s, lm_s, yb_s,
                        *, t_len, rc, group):
    ones_bd = _head_ones()
    n_ch = t_len // rc
    rowi = lax.broadcasted_iota(jnp.int32, (rc, 1), 0)

    def shifted(refs, mus, carry, r0):
        zs, lasts = [], []
        for ref, mu, prev in zip(refs, mus, carry):
            p = ref[pl.ds(r0, rc), :].astype(F32)
            zs.append(p + mu[...] * (_shift_rows(p, prev, rowi) - p))
            lasts.append(p[rc - 1:rc, :])
        return zs, tuple(lasts)

    @pl.when(pl.program_id(1) == 0)
    def _():
        lora_refs = (pwd_ref, pad_ref, pgd_ref)

        def phase0(c, carry):
            r0 = pl.multiple_of(c * rc, rc)
            (zwd, zad, zgd), lasts = shifted(lora_refs, (muwd_ref, muad_ref, mugd_ref), carry, r0)
            rows = pl.ds(r0, rc)
            tw_s[rows, :] = jnp.tanh(zwd).astype(BF16)
            za_s[rows, :] = zad.astype(BF16)
            sg_s[rows, :] = jax.nn.sigmoid(zgd).astype(BF16)
            return lasts

        lax.fori_loop(0, n_ch, phase0, tuple(jnp.zeros((1, ref.shape[1]), F32) for ref in lora_refs))

    def phase1(c, carry):
        r0 = pl.multiple_of(c * rc, rc)
        (zr, zk, zv), lasts = shifted((pr_ref, pk_ref, pv_ref), (mur_ref, muk_ref, muv_ref), carry, r0)
        rows = pl.ds(r0, rc)
        lw, g, kk, k2, beta, bonus = _rwkv_prep(
            zr, zk, zv, tw_s[rows, :], za_s[rows, :], sg_s[rows, :],
            w0_ref[...], w2_ref[...], a0_ref[...], a2_ref[...], g2_ref[...],
            kkw_ref[...], kaw_ref[...], rkw_ref[...], ones_bd)
        r_s[rows, :] = zr
        lw_s[rows, :] = lw
        k_s[rows, :] = k2
        v_s[rows, :] = zv
        kap_s[rows, :] = kk
        bet_s[rows, :] = beta
        g_s[rows, :] = g
        bon_s[rows, :] = bonus
        return lasts

    lax.fori_loop(0, n_ch, phase1, tuple(jnp.zeros((1, LANES), F32) for _ in range(3)))

    c_len = CHUNK
    slot = c_len + LANES
    ti = lax.broadcasted_iota(jnp.int32, (c_len, LANES), 0)
    si = lax.broadcasted_iota(jnp.int32, (c_len, LANES), 1) % HEAD
    strict = ti > si
    incl = ti >= si
    lri = lax.broadcasted_iota(jnp.int32, (c_len, c_len), 0)
    lci = lax.broadcasted_iota(jnp.int32, (c_len, c_len), 1)
    l_cum = (lri >= lci).astype(BF16)
    lane_head = lax.broadcasted_iota(jnp.int32, (1, LANES), 1) // HEAD
    lane_head2 = jnp.concatenate([lane_head, lane_head], axis=1)
    bri = lax.broadcasted_iota(jnp.int32, (LANES, LANES), 0)
    bci = lax.broadcasted_iota(jnp.int32, (LANES, LANES), 1)
    same_head = (bri // HEAD) == (bci // HEAD)
    same_head2 = jnp.concatenate([same_head, same_head], axis=1)
    eye = bri == bci
    zc = jnp.zeros((c_len, LANES), F32)
    nt_dims = (((1,), (1,)), ((), ()))

    def precompute(gi, between_stages):
        chunks = [gi * group + cc for cc in range(group)]
        rows = [pl.ds(pl.multiple_of(c * c_len, c_len), c_len) for c in chunks]
        lws = [lw_s[rw, :] for rw in rows]
        gcums = [_dot_exact_lhs(l_cum, lw) for lw in lws]
        g_ends = [g[c_len - 1:c_len, :] for g in gcums]
        kts = [kap_s[rw, :] * jnp.exp(g - lw) for rw, g, lw in zip(rows, gcums, lws)]
        rts = [r_s[rw, :] * jnp.exp(g) for rw, g in zip(rows, gcums)]
        e_negs = [jnp.exp(-g) for g in gcums]
        lhss = [jnp.concatenate([kt, rt], axis=0) for kt, rt in zip(kts, rts)]
        o_bs = [lax.dot_general(lhs, _head_split(bet_s[rw, :] * en, lane_head), nt_dims, preferred_element_type=F32)
                for lhs, rw, en in zip(lhss, rows, e_negs)]
        o_ks = [lax.dot_general(lhs, _head_split(k_s[rw, :] * en, lane_head), nt_dims, preferred_element_type=F32)
                for lhs, rw, en in zip(lhss, rows, e_negs)]
        n_mats = [jnp.where(strict, o[0:c_len], 0.0) for o in o_bs]
        v_bds = [_head_split(v_s[rw, :], lane_head) for rw in rows]
        avs = [jnp.dot(jnp.where(strict, o[0:c_len], 0.0), v_bd, preferred_element_type=F32)
               for o, v_bd in zip(o_ks, v_bds)]
        rhss = [jnp.concatenate([kt, av], axis=1) for kt, av in zip(kts, avs)]
        a_rs = [jnp.concatenate([jnp.where(incl, ob[c_len:], 0.0), jnp.where(incl, ok[c_len:], 0.0)], axis=1)
                for ob, ok in zip(o_bs, o_ks)]
        e_hats = [jnp.exp(ge - g) for ge, g in zip(g_ends, gcums)]
        bk_ts = [jnp.concatenate([bet_s[rw, :] * eh, k_s[rw, :] * eh], axis=0).T for rw, eh in zip(rows, e_hats)]
        xs = _solve_unit_lower(n_mats, rhss, lane_head2, between_stages)
        wus = [-x for x in xs]
        tops = [jnp.dot(a_r, jnp.concatenate([_head_split(wu, lane_head2),
                                              jnp.concatenate([jnp.zeros((LANES, LANES), F32), v_bd], axis=1)],
                                             axis=0), preferred_element_type=F32)
                for a_r, wu, v_bd in zip(a_rs, wus, v_bds)]
        bots = [jnp.dot(bk_t, jnp.concatenate([wu, jnp.concatenate([zc, v_s[rw, :]], axis=1)], axis=0),
                        preferred_element_type=F32)
                for bk_t, wu, rw in zip(bk_ts, wus, rows)]
        for c, top, bot, rt, ge in zip(chunks, tops, bots, rts, g_ends):
            bot = jnp.where(same_head2, bot, 0.0)
            base = pl.multiple_of((c + group) * slot, SUBLANES)
            lm_s[pl.ds(base, c_len), :] = top[:, 0:LANES] + rt
            lm_s[pl.ds(base + c_len, LANES), :] = bot[:, 0:LANES] + jnp.where(eye, jnp.exp(ge), 0.0)
            yb_s[pl.ds(base, c_len), :] = top[:, LANES:]
            yb_s[pl.ds(base + c_len, LANES), :] = bot[:, LANES:]

    def advance(c, h_bd):
        base = pl.multiple_of((c + group) * slot, SUBLANES)
        res = _dot_3pass(lm_s[pl.ds(base, slot), :], h_bd) + yb_s[pl.ds(base, slot), :]
        y_s[pl.ds(pl.multiple_of((c + group) * c_len, c_len), c_len), :] = res[0:c_len]
        return res[c_len:]

    def phase3(c, carry):
        r0 = pl.multiple_of(c * rc, rc)
        o = _rwkv_post(y_s[pl.ds(r0 + group * c_len, rc), :], bon_s[pl.ds(r0, rc), :], g_s[pl.ds(r0, rc), :],
                       gnw_ref[...], gnb_ref[...], ones_bd)
        ob_ref[pl.ds(r0, rc), :] = o.astype(ob_ref.dtype)
        return carry

    assert group <= c_len // SUBLANES
    n_groups = t_len // (c_len * group)
    lm_s[0:group * slot, :] = jnp.zeros((group * slot, LANES), F32)
    yb_s[0:group * slot, :] = jnp.zeros((group * slot, LANES), F32)

    def fused(gi, h):
        state = [h]

        def between_stages(blk):
            if blk < group:
                state[0] = advance((gi - 1) * group + blk, state[0])

        precompute(gi, between_stages)
        return state[0]

    h_bd = lax.fori_loop(0, n_groups, fused, jnp.zeros((LANES, LANES), F32))
    n_ready = min(group, ((n_groups - 1) * group * c_len) // rc)
    for blk in range(group):
        h_bd = advance(jnp.int32((n_groups - 1) * group + blk), h_bd)
        if blk < n_ready:
            phase3(jnp.int32(blk), 0)
    s_bd = h_bd.T
    s_ref[0, 0] = s_bd[0:HEAD, 0:HEAD]
    s_ref[0, 1] = s_bd[HEAD:, HEAD:]
    lax.fori_loop(n_ready, n_ch, phase3, 0)


def _rwkv_specs(row_block, rows, q_rows, lay):
    pw, pa, pg, nb = lay["pw"], lay["pa"], lay["pg"], lay["nb"]

    def im(col_fn, on_rows=False):
        if row_block is None:
            return lambda b, hp: (b if on_rows else 0, col_fn(hp))
        return lambda hp: (row_block if on_rows else 0, col_fn(hp))

    def triple(n_rows, on_rows, c_r, c_k, c_v, c_wd, c_ad, c_gd):
        return [
            pl.BlockSpec((n_rows, LANES), im(lambda hp: c_r + hp, on_rows)),
            pl.BlockSpec((n_rows, LANES), im(lambda hp: c_k + hp, on_rows)),
            pl.BlockSpec((n_rows, LANES), im(lambda hp: c_v + hp, on_rows)),
            pl.BlockSpec((n_rows, pw * LANES), im(lambda hp: c_wd // pw, on_rows)),
            pl.BlockSpec((n_rows, pa * LANES), im(lambda hp: c_ad // pa, on_rows)),
            pl.BlockSpec((n_rows, pg * LANES), im(lambda hp: c_gd // pg, on_rows)),
        ]

    specs = triple(rows, True, lay["r"], lay["k"], lay["v"], lay["wd"], lay["ad"], lay["gd"])
    if q_rows:
        specs += triple(q_rows, False, 0, nb, 2 * nb, 0, pw, pw + pa)
    specs += triple(1, False, 0, nb, 2 * nb, 0, pw, pw + pa)
    per_pair = im(lambda hp: hp)
    specs += [
        pl.BlockSpec((1, LANES), per_pair),
        pl.BlockSpec((pw * LANES, LANES), per_pair),
        pl.BlockSpec((1, LANES), per_pair),
        pl.BlockSpec((pa * LANES, LANES), per_pair),
        pl.BlockSpec((pg * LANES, LANES), per_pair),
        pl.BlockSpec((1, LANES), per_pair),
        pl.BlockSpec((1, LANES), per_pair),
        pl.BlockSpec((1, LANES), per_pair),
        pl.BlockSpec((1, LANES), per_pair),
        pl.BlockSpec((1, LANES), per_pair),
    ]
    return specs


def _rwkv_prompt(p_main, p_tail, mu_rkv, mu_lora, params, *, bp, tp, d, lay):
    n_hp = d // LANES
    pw, pa, pg = lay["pw"], lay["pa"], lay["pg"]
    rc = 256 if tp % 256 == 0 else tp
    n_heads = d // HEAD
    n_chunks = tp // CHUNK
    group = next(g for g in (8, 4, 2, 1) if n_chunks % g == 0)
    kern = functools.partial(_rwkv_prompt_kernel, t_len=tp, rc=rc, group=group)
    vm = pltpu.VMEM((tp, LANES), F32)
    return pl.pallas_call(
        kern,
        grid=(bp, n_hp),
        in_specs=_rwkv_specs(None, tp, 0, lay),
        out_specs=[
            pl.BlockSpec((tp, LANES), lambda b, hp: (b, hp)),
            pl.BlockSpec((1, 2, HEAD, HEAD), lambda b, hp: (b, hp, 0, 0)),
        ],
        out_shape=[
            jax.ShapeDtypeStruct((bp * tp, d), BF16),
            jax.ShapeDtypeStruct((bp, n_heads, HEAD, HEAD), F32),
        ],
        scratch_shapes=[vm] * 8 + [
            pltpu.VMEM((tp + group * CHUNK, LANES), F32),
            pltpu.VMEM((tp, pw * LANES), BF16),
            pltpu.VMEM((tp, pa * LANES), BF16),
            pltpu.VMEM((tp, pg * LANES), BF16),
            pltpu.VMEM(((n_chunks + group) * (CHUNK + LANES), LANES), F32),
            pltpu.VMEM(((n_chunks + group) * (CHUNK + LANES), LANES), F32),
        ],
        compiler_params=_cparams(2),
        name="rwkv_prompt",
    )(p_main, p_main, p_main, p_tail, p_tail, p_tail,
      mu_rkv, mu_rkv, mu_rkv, mu_lora, mu_lora, mu_lora, *params)


def _rwkv_sample_kernel(pr_ref, pk_ref, pv_ref, pwd_ref, pad_ref, pgd_ref,
                        qr_ref, qk_ref, qv_ref, qwd_ref, qad_ref, qgd_ref,
                        mur_ref, muk_ref, muv_ref, muwd_ref, muad_ref, mugd_ref,
                        w0_ref, w2_ref, a0_ref, a2_ref, g2_ref, kkw_ref, kaw_ref, rkw_ref, gnw_ref, gnb_ref,
                        st_ref, ob_ref, so_ref, y_s, *, ts, bs):
    ones_bd = _head_ones()
    refs = (pr_ref, pk_ref, pv_ref, pwd_ref, pad_ref, pgd_ref)
    prevs = (qr_ref, qk_ref, qv_ref, qwd_ref, qad_ref, qgd_ref)
    mus = (mur_ref, muk_ref, muv_ref, muwd_ref, muad_ref, mugd_ref)
    zs = []
    for ref, q, mu in zip(refs, prevs, mus):
        p = ref[...].astype(F32)
        pp = jnp.concatenate([q[...].astype(F32), p[0:(ts - 1) * bs, :]], axis=0)
        zs.append(p + mu[...] * (pp - p))
    zr, zk, zv, zwd, zad, zgd = zs
    lw, g, kk, k2, beta, bonus = _rwkv_prep(
        zr, zk, zv, jnp.tanh(zwd).astype(BF16), zad.astype(BF16), jax.nn.sigmoid(zgd).astype(BF16),
        w0_ref[...], w2_ref[...], a0_ref[...], a2_ref[...], g2_ref[...],
        kkw_ref[...], kaw_ref[...], rkw_ref[...], ones_bd)
    w_dec = jnp.exp(lw)

    for t in range(ts):
        rows = slice(t * bs, (t + 1) * bs)
        wt, kkt, bt, kt, rt, vt = (a[rows, :].T for a in (w_dec, kk, beta, k2, zr, zv))
        for h in range(2):
            lo = h * HEAD
            hd = slice(lo, lo + HEAD)
            s = (st_ref if t == 0 else so_ref)[h]
            sa = -jnp.sum(s * kkt[hd][None], axis=1, keepdims=True)
            s = s * wt[hd][None] + sa * bt[hd][None] + vt[hd][:, None, :] * kt[hd][None]
            so_ref[h] = s
            y_s[t, hd, :] = jnp.sum(s * rt[hd][None], axis=1)
    for t in range(ts):
        yt = y_s[t].T
        rows = slice(t * bs, (t + 1) * bs)
        o = _rwkv_post(yt, bonus[rows], g[rows], gnw_ref[...], gnb_ref[...], ones_bd)
        ob_ref[rows, :] = o.astype(ob_ref.dtype)


def _rwkv_sample(p_main, p_tail, q_rkv, q_lora, mu_rkv, mu_lora, params, st, *, ts, bs, d, mp, lay):
    n_hp = d // LANES
    rows = ts * bs
    rblk = mp // rows
    in_specs = _rwkv_specs(rblk, rows, bs, lay)
    in_specs.append(pl.BlockSpec((2, HEAD, HEAD, bs), lambda hp: (hp, 0, 0, 0)))
    kern = functools.partial(_rwkv_sample_kernel, ts=ts, bs=bs)
    return pl.pallas_call(
        kern,
        grid=(n_hp,),
        in_specs=in_specs,
        out_specs=[
            pl.BlockSpec((rows, LANES), lambda hp: (0, hp)),
            pl.BlockSpec((2, HEAD, HEAD, bs), lambda hp: (hp, 0, 0, 0)),
        ],
        out_shape=[
            jax.ShapeDtypeStruct((rows, d), BF16),
            jax.ShapeDtypeStruct(st.shape, F32),
        ],
        scratch_shapes=[pltpu.VMEM((ts, LANES, bs), F32)],
        compiler_params=_cparams(1),
        name="rwkv_sample",
    )(p_main, p_main, p_main, p_tail, p_tail, p_tail, q_rkv, q_rkv, q_rkv, q_lora, q_lora, q_lora,
      mu_rkv, mu_rkv, mu_rkv, mu_lora, mu_lora, mu_lora, *params, st)


def _pad_cols(a, width):
    return jnp.pad(a, ((0, 0), (0, width - a.shape[1])))


def _pad_rows(a, height):
    return jnp.pad(a, ((0, height - a.shape[0]), (0, 0)))


def _tiles(d, d_ff, n_tail):
    pick = lambda n, cands: next(c for c in cands if n % c == 0)
    return dict(
        tm=512, tm_wide=256,
        main=pick(5 * d, (1280, 1024, 512, 256, 128)),
        tail=pick(n_tail, (2304, 1536, 1024, 512, 256, 128)),
        prev=pick(d, (1024, 512, 256, 128)),
        up=pick(d_ff, (1024, 512, 256, 128)),
        down=pick(d, (512, 256, 128)),
    )


def kernel(x_prompt, x_sample, p_prompt, p_sample, state_rg_h, state_rg_conv, state_rwkv, state_shift,
           norm_mix, w_in, conv_w, conv_b, rg_wa, rg_ba, rg_wx, rg_bx, rg_lam, w_rg_o,
           mu_shift, rw_w0, rw_w2, rw_a0, rw_a2, rw_g2, rw_kk, rw_ka, rw_rk, rw_gn_w, rw_gn_b,
           w_rw_o, w_o, norm_ffn, w_up, w_down, norm_ple, w_ple_gate, w_ple, norm_f):
    bp, tp, d = x_prompt.shape
    bs, ts, _ = x_sample.shape
    depth = w_in.shape[0]
    n_heads, head = rw_rk.shape[1], rw_rk.shape[2]
    r_w, r_a, r_g = rw_w2.shape[1], rw_a2.shape[1], rw_g2.shape[1]
    d_ple = w_ple.shape[1]
    d_ff = w_up.shape[2]
    n_tap = conv_w.shape[1]
    assert head == HEAD and n_heads * HEAD == d and d % LANES == 0 and bs == LANES
    assert rg_wa.shape[2] == LANES and tp % CHUNK == 0 and n_tap == 4
    mp, ms = bp * tp, bs * ts
    nb = d // LANES
    pw, pa, pg = (_round_up(r, LANES) // LANES for r in (r_w, r_a, r_g))
    n_lora = (pw + pa + pg) * LANES
    n_tail = 2 * d + n_lora
    lay = dict(u=0, g=nb, r=2 * nb, k=3 * nb, v=4 * nb, wd=2 * nb, ad=2 * nb + pw, gd=2 * nb + pw + pa,
               pw=pw, pa=pa, pg=pg, nb=nb)
    t = _tiles(d, d_ff, n_tail)
    tm = t["tm"]
    assert mp % tm == 0 and ms % tm == 0 and mp % (ts * bs) == 0 and ms % t["tm_wide"] == 0
    assert lay["wd"] % pw == 0 and lay["ad"] % pa == 0 and lay["gd"] % pg == 0 and pw % pa == 0 and (pw + pa) % pg == 0
    assert (2 * d) % t["prev"] == 0 and (2 * d) % 512 == 0 and n_lora % 512 == 0 and d % 512 == 0
    mm = functools.partial(_mm, mp=mp, ms=ms)

    x = _Rows(x_prompt.reshape(mp, d), jnp.transpose(x_sample, (1, 0, 2)).reshape(ms, d))
    hp_l, cp_l, sp_l, xp_l, hs_l, cs_l, ss_l, xs_l = [], [], [], [], [], [], [], []
    y_p = y_s = None
    for i in range(depth):
        o_rw = 2 * d
        o_lora = o_rw + 3 * d
        o_g = o_lora + r_w + r_a + r_g
        wt = jnp.swapaxes(w_in[i], 0, 1)

        def lora_rows(a):
            return jnp.concatenate([
                _pad_rows(a[o_lora:o_lora + r_w], pw * LANES),
                _pad_rows(a[o_lora + r_w:o_lora + r_w + r_a], pa * LANES),
                _pad_rows(a[o_lora + r_w + r_a:o_g], pg * LANES)], axis=0)

        wt_tail = jnp.concatenate([wt[o_g:], lora_rows(wt)], axis=0).astype(BF16)
        mu_all = jnp.concatenate([jnp.zeros((o_rw, 1), F32), mu_shift[i][:, None], jnp.zeros((2 * d, 1), F32)], axis=0)
        mu_rkv = mu_shift[i][None, :3 * d]
        mu_lora = lora_rows(mu_all).reshape(1, n_lora)
        rw_params = (
            rw_w0[i][None], _pad_rows(rw_w2[i], pw * LANES).astype(BF16),
            rw_a0[i][None], _pad_rows(rw_a2[i], pa * LANES).astype(BF16),
            _pad_rows(rw_g2[i], pg * LANES).astype(BF16),
            rw_kk[i][None], rw_ka[i][None], rw_rk[i].reshape(1, d), rw_gn_w[i][None], rw_gn_b[i][None])
        rg_params = (conv_w[i], conv_b[i][None], rg_wa[i].astype(BF16), rg_ba[i][:, None, :],
                     rg_wx[i].astype(BF16), rg_bx[i][:, None, :], rg_lam[i][None])

        xn = _Rows(_rmsnorm(x, norm_mix[i], BF16, tm, mp))
        xa = x.arrays if len(x.arrays) == 2 else (x.arrays[0][:mp], x.arrays[0][mp:])
        x_last = jnp.concatenate([xa[0].reshape(bp, tp, d)[:, -1], xa[1][(ts - 1) * bs:]], axis=0)
        n_last = _round_up(bp + bs, SUBLANES)
        xn_last = _rmsnorm(_Rows(_pad_rows(x_last, n_last)), norm_mix[i], F32, n_last, n_last)
        (p_main,) = mm([xn], [wt], [], _epi_plain, [(F32, False)], tm=tm, tn=t["main"], n_out=5 * d,
                       cast_w=True, w_transposed=True, name="in_proj")
        (p_tail,) = mm([xn], [wt_tail], [], _epi_plain, [(BF16, False)], tm=tm, tn=t["tail"], n_out=n_tail,
                       w_transposed=True, name="in_proj_tail")
        xprev = _Rows(state_shift[i])
        (q_rkv,) = _mm([xprev], [wt], [], _epi_plain, [(F32, False)], mp=bs, ms=0, tm=bs, tn=t["prev"],
                       n_out=3 * d, w_col_off=o_rw // t["prev"], cast_w=True, w_transposed=True, name="prev_proj")
        (q_lora,) = _mm([xprev], [wt_tail], [], _epi_plain, [(BF16, False)], mp=bs, ms=0, tm=bs, tn=512,
                        n_out=n_lora, w_col_off=2 * d // 512, w_transposed=True, name="prev_proj_tail")

        cwid = 2 * LANES if nb % 2 == 0 else LANES
        hg_p, h_p, c_p = _rg_prompt(p_main, *rg_params, bp=bp, tp=tp, d=d, cb_u=lay["u"], cb_g=lay["g"], cwid=cwid)
        cbuf_t = jnp.transpose(state_rg_conv[i], (1, 0, 2))
        hg_s, h_s, c_s = _rg_sample(p_main, cbuf_t, state_rg_h[i], *rg_params, ts=ts, bs=bs, d=d, mp=mp,
                                    cb_u=lay["u"], cb_g=lay["g"], cwid=cwid)

        ob_p, s_p = _rwkv_prompt(p_main, p_tail, mu_rkv, mu_lora, rw_params, bp=bp, tp=tp, d=d, lay=lay)
        st = jnp.transpose(state_rwkv[i], (1, 2, 3, 0))
        ob_s, st_new = _rwkv_sample(p_main, p_tail, q_rkv, q_lora, mu_rkv, mu_lora, rw_params, st,
                                    ts=ts, bs=bs, d=d, mp=mp, lay=lay)
        s_s = jnp.transpose(st_new, (3, 0, 1, 2))

        tw = t["tm_wide"]
        (merged,) = mm([_Rows(hg_p, hg_s), _Rows(ob_p, ob_s)], [w_rg_o[i].astype(BF16), w_rw_o[i].astype(BF16)],
                       [("tile", _Rows(p_tail), 0), ("tile", _Rows(p_tail), 1)],
                       _epi_merge, [(BF16, False)], tm=tw, tn=d, n_out=d, single_buffer_w=True, name="merge")
        x1, xn2 = mm([_Rows(merged)], [w_o[i].astype(BF16)], [("tile", x, 0), ("row", norm_ffn[i][None], 0)],
                     _epi_residual_norm, [(F32, False), (BF16, False)], tm=tw, tn=d, n_out=d, name="out_proj")
        (hf,) = mm([_Rows(xn2)], [w_up[i]], [], _epi_relu2, [(BF16, False)], tm=tm, tn=t["up"], n_out=d_ff,
                   cast_w=True, name="mlp_up")
        (x2,) = mm([_Rows(hf)], [w_down[i].astype(BF16)], [("tile", _Rows(x1), 0)], _epi_residual, [(F32, False)],
                   tm=tm, tn=t["down"], n_out=d, name="mlp_down")
        xn3 = _rmsnorm(_Rows(x2), norm_ple[i], BF16, tm, mp)
        p_rows = _Rows(p_prompt[i].reshape(mp, d_ple), jnp.transpose(p_sample[i], (1, 0, 2)).reshape(ms, d_ple))
        ple_ws = [w_ple_gate[i].astype(BF16), w_ple[i].astype(BF16)]
        if i == depth - 1:
            y_p, y_s = mm([_Rows(xn3), p_rows], ple_ws, [("tile", _Rows(x2), 0), ("row", norm_f[None], 0)],
                          _epi_ple_final, [(F32, True)], tm=tw, tn=d, n_out=d, name="ple_final")
        else:
            x = _Rows(*mm([_Rows(xn3), p_rows], ple_ws, [("tile", _Rows(x2), 0)], _epi_ple, [(F32, False)],
                          tm=tw, tn=d, n_out=d, name="ple"))

        hp_l.append(h_p[:, 0])
        cp_l.append(c_p[:, SUBLANES - (n_tap - 1):])
        sp_l.append(s_p)
        xp_l.append(xn_last[:bp])
        hs_l.append(h_s)
        cs_l.append(jnp.transpose(c_s, (1, 0, 2)))
        ss_l.append(s_s)
        xs_l.append(xn_last[bp:bp + bs])

    y_prompt = y_p.reshape(bp, tp, d)
    y_sample = jnp.transpose(y_s.reshape(ts, bs, d), (1, 0, 2))
    return (y_prompt, y_sample,
            jnp.stack(hp_l), jnp.stack(cp_l), jnp.stack(sp_l), jnp.stack(xp_l),
            jnp.stack(hs_l), jnp.stack(cs_l), jnp.stack(ss_l), jnp.stack(xs_l))
```

```python
import functools

import jax
import jax.numpy as jnp
from jax import lax
from jax.experimental import pallas as pl
from jax.experimental.pallas import tpu as pltpu

F32 = jnp.float32
BF16 = jnp.bfloat16
LANES = 128
SUBLANES = 8
HEAD = 64
CHUNK = 64
_SAMPLE_ROWS = 4
EPS = 1e-6
GN_EPS = 64e-5
RG_C = 8.0
VMEM_LIMIT = 56 * 1024 * 1024


def _cparams(n_grid, vmem=VMEM_LIMIT):
    return pltpu.CompilerParams(dimension_semantics=("arbitrary",) * n_grid, vmem_limit_bytes=vmem)


def _round_up(x, m):
    return (x + m - 1) // m * m


def _rms(x, g):
    ms = jnp.mean(x * x, axis=-1, keepdims=True)
    return (x * lax.rsqrt(ms + EPS)) * g


class _Rows:
    def __init__(self, *arrays):
        self.arrays = arrays

    @property
    def width(self):
        return self.arrays[0].shape[1]

    def specs(self, tm, n_p, col):
        if len(self.arrays) == 1:
            return [lambda wid: pl.BlockSpec((tm, wid), lambda n, i: (i, col(n)))]
        return [lambda wid: pl.BlockSpec((tm, wid), lambda n, i: (jnp.minimum(i, n_p - 1), col(n))),
                lambda wid: pl.BlockSpec((tm, wid), lambda n, i: (jnp.maximum(i - n_p, 0), col(n)))]


def _load_rows(refs, is_prompt):
    if len(refs) == 1:
        return refs[0][...]
    return jnp.where(is_prompt, refs[0][...], refs[1][...])


def _rmsnorm_kernel(*refs, n_src, n_p):
    g_ref, o_ref = refs[n_src], refs[n_src + 1]
    x = _load_rows(refs[:n_src], pl.program_id(1) < n_p)
    o_ref[...] = _rms(x, g_ref[...]).astype(o_ref.dtype)


def _rmsnorm(x, g, out_dtype, tm, mp):
    d = x.width
    m = sum(a.shape[0] for a in x.arrays)
    n_p = mp // tm
    in_specs = [mk(d) for mk in x.specs(tm, n_p, lambda n: 0)] + [pl.BlockSpec((1, d), lambda n, i: (0, 0))]
    return pl.pallas_call(
        functools.partial(_rmsnorm_kernel, n_src=len(x.arrays), n_p=n_p),
        grid=(1, m // tm),
        in_specs=in_specs,
        out_specs=pl.BlockSpec((tm, d), lambda n, i: (i, 0)),
        out_shape=jax.ShapeDtypeStruct((m, d), out_dtype),
        compiler_params=_cparams(2),
        name="rmsnorm",
    )(*x.arrays, g.reshape(1, d))


def _mm_kernel(*refs, x_counts, n_w, cast_w, w_transposed, extra_counts, out_counts, n_p, epilogue):
    pos = 0
    x_refs = []
    for cnt in x_counts:
        x_refs.append(refs[pos:pos + cnt])
        pos += cnt
    w_refs = refs[pos:pos + n_w]
    pos += n_w
    e_refs = []
    for cnt in extra_counts:
        e_refs.append(refs[pos:pos + cnt])
        pos += cnt
    o_refs = []
    for cnt in out_counts:
        o_refs.append(refs[pos:pos + cnt])
        pos += cnt
    w_scratch = refs[pos:]
    i = pl.program_id(1)
    is_prompt = i < n_p
    if cast_w:
        @pl.when(i == 0)
        def _():
            for w_ref, s_ref in zip(w_refs, w_scratch):
                s_ref[...] = w_ref[...].astype(BF16)
        ws = [s[...] for s in w_scratch]
    else:
        ws = [w[...] for w in w_refs]
    dims = (((1,), (1 if w_transposed else 0,)), ((), ()))
    accs = [lax.dot_general(_load_rows(xr, is_prompt).astype(BF16), w, dims, preferred_element_type=F32)
            for xr, w in zip(x_refs, ws)]
    outs = epilogue(accs, [_load_rows(er, is_prompt) for er in e_refs])
    for refs_o, tile in zip(o_refs, outs):
        if len(refs_o) == 1:
            refs_o[0][...] = tile.astype(refs_o[0].dtype)
        else:
            @pl.when(is_prompt)
            def _():
                refs_o[0][...] = tile.astype(refs_o[0].dtype)

            @pl.when(jnp.logical_not(is_prompt))
            def _():
                refs_o[1][...] = tile.astype(refs_o[1].dtype)


def _mm(xs, ws, extras, epilogue, outs, *, mp, ms, tm, tn, n_out, w_col_off=0, cast_w=False,
        w_transposed=False, single_buffer_w=False, name="mm"):
    n_p = mp // tm
    m = mp + ms
    grid = (n_out // tn, m // tm)
    in_specs, args, x_counts, extra_counts = [], [], [], []
    for x in xs:
        in_specs += [mk(x.width) for mk in x.specs(tm, n_p, lambda n: 0)]
        args += list(x.arrays)
        x_counts.append(len(x.arrays))
    w_mode = dict(pipeline_mode=pl.Buffered(1)) if single_buffer_w else {}
    if w_transposed:
        w_blocks = [(tn, w.shape[1]) for w in ws]
        in_specs += [pl.BlockSpec(blk, lambda n, i: (n + w_col_off, 0), **w_mode) for blk in w_blocks]
    else:
        w_blocks = [(w.shape[0], tn) for w in ws]
        in_specs += [pl.BlockSpec(blk, lambda n, i: (0, n + w_col_off), **w_mode) for blk in w_blocks]
    args += list(ws)
    for kind, src, off in extras:
        if kind == "tile":
            in_specs += [mk(tn) for mk in src.specs(tm, n_p, lambda n, off=off: n + off)]
            args += list(src.arrays)
            extra_counts.append(len(src.arrays))
        elif kind == "row":
            in_specs.append(pl.BlockSpec((1, tn), lambda n, i, off=off: (0, n + off)))
            args.append(src)
            extra_counts.append(1)
        else:
            raise ValueError(kind)
    out_specs, out_shape, out_counts = [], [], []
    for dtype, split in outs:
        if split:
            out_specs += [pl.BlockSpec((tm, tn), lambda n, i: (jnp.minimum(i, n_p - 1), n)),
                          pl.BlockSpec((tm, tn), lambda n, i: (jnp.maximum(i - n_p, 0), n))]
            out_shape += [jax.ShapeDtypeStruct((mp, n_out), dtype), jax.ShapeDtypeStruct((ms, n_out), dtype)]
            out_counts.append(2)
        else:
            out_specs.append(pl.BlockSpec((tm, tn), lambda n, i: (i, n)))
            out_shape.append(jax.ShapeDtypeStruct((m, n_out), dtype))
            out_counts.append(1)
    kern = functools.partial(_mm_kernel, x_counts=tuple(x_counts), n_w=len(ws), cast_w=cast_w,
                             w_transposed=w_transposed, extra_counts=tuple(extra_counts),
                             out_counts=tuple(out_counts), n_p=n_p, epilogue=epilogue)
    scratch = [pltpu.VMEM(blk, BF16) for blk in w_blocks] if cast_w else []
    res = pl.pallas_call(
        kern,
        grid=grid,
        in_specs=in_specs,
        out_specs=out_specs,
        out_shape=out_shape,
        scratch_shapes=scratch,
        compiler_params=_cparams(2),
        name=name,
    )(*args)
    return res


def _epi_plain(accs, extras):
    return (accs[0],)


def _epi_merge(accs, extras):
    ga, gb = extras
    return (jax.nn.sigmoid(ga.astype(F32)) * accs[0] + jax.nn.sigmoid(gb.astype(F32)) * accs[1],)


def _epi_residual(accs, extras):
    return (extras[0] + accs[0],)


def _epi_residual_norm(accs, extras):
    x = extras[0] + accs[0]
    return x, _rms(x, extras[1])


def _epi_relu2(accs, extras):
    h = jnp.maximum(accs[0], 0.0)
    return (h * h,)


def _epi_ple(accs, extras):
    return (extras[0] + jax.nn.sigmoid(accs[0]) * accs[1],)


def _epi_ple_final(accs, extras):
    return (_rms(extras[0] + jax.nn.sigmoid(accs[0]) * accs[1], extras[1]),)


def _softplus(x):
    return jnp.maximum(x, 0.0) + jnp.log1p(jnp.exp(-jnp.abs(x)))


def _split_bf16(x, parts):
    out = []
    rem = x
    for _ in range(parts):
        p = rem.astype(BF16)
        out.append(p)
        rem = rem - p.astype(F32)
    return out


def _dot_exact_lhs(a_bf16, x, parts=3):
    acc = None
    for p in _split_bf16(x, parts):
        t = jnp.dot(a_bf16, p, preferred_element_type=F32)
        acc = t if acc is None else acc + t
    return acc


def _head_sum(x, ones_bd):
    return jnp.dot(x.astype(BF16), ones_bd, preferred_element_type=F32)


def _sigmoid(x):
    return 0.5 * jnp.tanh(0.5 * x) + 0.5


def _dot_3pass(a, b):
    a_hi, a_lo = _split_bf16(a, 2)
    b_hi, b_lo = _split_bf16(b, 2)
    return (jnp.dot(a_hi, b_hi, preferred_element_type=F32)
            + jnp.dot(a_hi, b_lo, preferred_element_type=F32)
            + jnp.dot(a_lo, b_hi, preferred_element_type=F32))


def _head_ones():
    r = lax.broadcasted_iota(jnp.int32, (LANES, LANES), 0) // HEAD
    c = lax.broadcasted_iota(jnp.int32, (LANES, LANES), 1) // HEAD
    return (r == c).astype(BF16)


def _rwkv_prep(zr, zk, zv, tw, za, sg, w0, w2, a0, a2, g2, kkw, kaw, rkw, ones_bd):
    wlog = -_softplus(-(w0 + jnp.dot(tw, w2, preferred_element_type=F32))) - 0.5
    lw = -jnp.exp(wlog)
    a = _sigmoid(a0 + jnp.dot(za, a2, preferred_element_type=F32))
    g = jnp.dot(sg, g2, preferred_element_type=F32)
    kk = zk * kkw
    ss = _head_sum(kk * kk, ones_bd)
    kk = kk * lax.rsqrt(jnp.maximum(ss, 1e-24))
    k2 = zk * (1.0 + (a - 1.0) * kaw)
    beta = kk * a
    bonus = _head_sum(zr * k2 * rkw, ones_bd) * zv
    return lw, g, kk, k2, beta, bonus


def _rwkv_post(y, bonus, g, gnw, gnb, ones_bd):
    mu = _head_sum(y, ones_bd) * (1.0 / HEAD)
    d = y - mu
    var = _head_sum(d * d, ones_bd) * (1.0 / HEAD)
    yn = (d * lax.rsqrt(var + GN_EPS)) * gnw + gnb
    return (yn + bonus) * g


def _rg_gates(xc, wa, ba, wx, bx, sp):
    xcb = xc.astype(BF16)
    r = _sigmoid(jnp.dot(xcb, wa, preferred_element_type=F32) + ba)
    i = _sigmoid(jnp.dot(xcb, wx, preferred_element_type=F32) + bx)
    log_a = (-RG_C * r) * sp
    a = jnp.exp(log_a)
    mult = jnp.sqrt(-jnp.tanh(log_a) * (a * a + 1.0))
    return a, mult, i * xc


def _rg_prompt_kernel(u_ref, gt_ref, cw_ref, cb_ref, wa_ref, ba_ref, wx_ref, bx_ref, lam_ref,
                      hg_ref, hl_ref, cs_ref, us_ref, a_ref, b_ref, *, t_len, rc):
    cwid = u_ref.shape[1]
    nb = cwid // LANES
    us_ref[0:SUBLANES, :] = jnp.zeros((SUBLANES, cwid), F32)
    us_ref[SUBLANES:, :] = u_ref[...]
    sp = _softplus(-lam_ref[...])
    w0 = cw_ref[0:1, :]
    w1 = cw_ref[1:2, :]
    w2 = cw_ref[2:3, :]
    w3 = cw_ref[3:4, :]
    cb = cb_ref[...]
    n_ch = t_len // rc

    def phase1(c, carry):
        r0 = pl.multiple_of(c * rc, rc)
        e = us_ref[pl.ds(r0, rc + SUBLANES), :]
        u0 = e[SUBLANES:]
        u1 = pltpu.roll(e, 1, 0)[SUBLANES:]
        u2 = pltpu.roll(e, 2, 0)[SUBLANES:]
        u3 = pltpu.roll(e, 3, 0)[SUBLANES:]
        xc = cb + (u3 * w0 + u2 * w1 + u1 * w2 + u0 * w3)
        row = lax.broadcasted_iota(jnp.int32, (rc, LANES), 0) + r0
        for n in range(nb):
            sl = slice(n * LANES, (n + 1) * LANES)
            a, mult, ix = _rg_gates(xc[:, sl], wa_ref[n], ba_ref[n], wx_ref[n], bx_ref[n], sp[:, sl])
            mult = jnp.where(row == 0, 1.0, mult)
            a_ref[n, pl.ds(r0, rc), :] = a
            b_ref[n, pl.ds(r0, rc), :] = mult * ix
        return carry

    lax.fori_loop(0, n_ch, phase1, 0)

    rowi = lax.broadcasted_iota(jnp.int32, (SUBLANES, LANES), 0)

    def scan(i, h_prev):
        r0 = pl.multiple_of(i * SUBLANES, SUBLANES)
        out = []
        for n in range(nb):
            a = a_ref[n, pl.ds(r0, SUBLANES), :]
            b = b_ref[n, pl.ds(r0, SUBLANES), :]
            for d in (1, 2, 4):
                a_sh = jnp.where(rowi < d, 1.0, pltpu.roll(a, d, 0))
                b_sh = jnp.where(rowi < d, 0.0, pltpu.roll(b, d, 0))
                b = a * b_sh + b
                a = a * a_sh
            h = b + a * h_prev[n]
            b_ref[n, pl.ds(r0, SUBLANES), :] = h
            out.append(h[SUBLANES - 1:SUBLANES, :])
        return tuple(out)

    h_last = lax.fori_loop(0, t_len // SUBLANES, scan, tuple(jnp.zeros((1, LANES), F32) for _ in range(nb)),
                           unroll=4)
    hl_ref[0] = jnp.concatenate(list(h_last), axis=1)
    cs_ref[0] = u_ref[pl.ds(t_len - SUBLANES, SUBLANES), :]

    def phase3(c, carry):
        r0 = pl.multiple_of(c * rc, rc)
        for n in range(nb):
            sl = slice(n * LANES, (n + 1) * LANES)
            gate = gt_ref[pl.ds(r0, rc), sl]
            hg_ref[pl.ds(r0, rc), sl] = (b_ref[n, pl.ds(r0, rc), :] * jax.nn.gelu(gate)).astype(hg_ref.dtype)
        return carry

    lax.fori_loop(0, n_ch, phase3, 0)


def _rg_prompt(p, conv_w, conv_b, wa, ba, wx, bx, lam, *, bp, tp, d, cb_u, cb_g, cwid):
    nb = cwid // LANES
    nblk = d // cwid
    rc = 256 if tp % 256 == 0 else tp
    kern = functools.partial(_rg_prompt_kernel, t_len=tp, rc=rc)
    return pl.pallas_call(
        kern,
        grid=(bp, nblk),
        in_specs=[
            pl.BlockSpec((tp, cwid), lambda b, n: (b, cb_u * LANES // cwid + n)),
            pl.BlockSpec((tp, cwid), lambda b, n: (b, cb_g * LANES // cwid + n)),
            pl.BlockSpec((4, cwid), lambda b, n: (0, n)),
            pl.BlockSpec((1, cwid), lambda b, n: (0, n)),
            pl.BlockSpec((nb, LANES, LANES), lambda b, n: (n, 0, 0)),
            pl.BlockSpec((nb, 1, LANES), lambda b, n: (n, 0, 0)),
            pl.BlockSpec((nb, LANES, LANES), lambda b, n: (n, 0, 0)),
            pl.BlockSpec((nb, 1, LANES), lambda b, n: (n, 0, 0)),
            pl.BlockSpec((1, cwid), lambda b, n: (0, n)),
        ],
        out_specs=[
            pl.BlockSpec((tp, cwid), lambda b, n: (b, n)),
            pl.BlockSpec((1, 1, cwid), lambda b, n: (b, 0, n)),
            pl.BlockSpec((1, SUBLANES, cwid), lambda b, n: (b, 0, n)),
        ],
        out_shape=[
            jax.ShapeDtypeStruct((bp * tp, d), BF16),
            jax.ShapeDtypeStruct((bp, 1, d), F32),
            jax.ShapeDtypeStruct((bp, SUBLANES, d), F32),
        ],
        scratch_shapes=[
            pltpu.VMEM((tp + SUBLANES, cwid), F32),
            pltpu.VMEM((nb, tp, LANES), F32),
            pltpu.VMEM((nb, tp, LANES), F32),
        ],
        compiler_params=_cparams(2),
        name="rg_prompt",
    )(p, p, conv_w, conv_b, wa, ba, wx, bx, lam)


def _rg_sample_kernel(u_ref, gt_ref, cbuf_ref, h0_ref, cw_ref, cb_ref, wa_ref, ba_ref, wx_ref, bx_ref, lam_ref,
                      hg_ref, hl_ref, cs_ref, *, ts, bs):
    cwid = u_ref.shape[1]
    nb = cwid // LANES
    n_tap = cw_ref.shape[0]
    sp = _softplus(-lam_ref[...])
    cb = cb_ref[...]
    ext = [cbuf_ref[j] for j in range(n_tap - 1)] + [u_ref[t * bs:(t + 1) * bs, :] for t in range(ts)]
    for j in range(n_tap - 1):
        cs_ref[j] = ext[len(ext) - (n_tap - 1) + j]
    h = h0_ref[...]
    for t in range(ts):
        conv = ext[t] * cw_ref[0:1, :]
        for j in range(1, n_tap):
            conv = conv + ext[t + j] * cw_ref[j:j + 1, :]
        xc = cb + conv
        pieces = []
        for n in range(nb):
            sl = slice(n * LANES, (n + 1) * LANES)
            a, mult, ix = _rg_gates(xc[:, sl], wa_ref[n], ba_ref[n], wx_ref[n], bx_ref[n], sp[:, sl])
            pieces.append(a * h[:, sl] + mult * ix)
        h = pieces[0] if nb == 1 else jnp.concatenate(pieces, axis=1)
        gate = gt_ref[t * bs:(t + 1) * bs, :]
        hg_ref[t * bs:(t + 1) * bs, :] = (h * jax.nn.gelu(gate)).astype(hg_ref.dtype)
    hl_ref[...] = h


def _rg_sample(p, cbuf_t, h0, conv_w, conv_b, wa, ba, wx, bx, lam, *, ts, bs, d, mp, cb_u, cb_g, cwid):
    nb = cwid // LANES
    nblk = d // cwid
    rows = ts * bs
    rblk = mp // rows
    kern = functools.partial(_rg_sample_kernel, ts=ts, bs=bs)
    n_tap = conv_w.shape[0]
    return pl.pallas_call(
        kern,
        grid=(nblk,),
        in_specs=[
            pl.BlockSpec((rows, cwid), lambda n: (rblk, cb_u * LANES // cwid + n)),
            pl.BlockSpec((rows, cwid), lambda n: (rblk, cb_g * LANES // cwid + n)),
            pl.BlockSpec((n_tap - 1, bs, cwid), lambda n: (0, 0, n)),
            pl.BlockSpec((bs, cwid), lambda n: (0, n)),
            pl.BlockSpec((n_tap, cwid), lambda n: (0, n)),
            pl.BlockSpec((1, cwid), lambda n: (0, n)),
            pl.BlockSpec((nb, LANES, LANES), lambda n: (n, 0, 0)),
            pl.BlockSpec((nb, 1, LANES), lambda n: (n, 0, 0)),
            pl.BlockSpec((nb, LANES, LANES), lambda n: (n, 0, 0)),
            pl.BlockSpec((nb, 1, LANES), lambda n: (n, 0, 0)),
            pl.BlockSpec((1, cwid), lambda n: (0, n)),
        ],
        out_specs=[
            pl.BlockSpec((rows, cwid), lambda n: (0, n)),
            pl.BlockSpec((bs, cwid), lambda n: (0, n)),
            pl.BlockSpec((n_tap - 1, bs, cwid), lambda n: (0, 0, n)),
        ],
        out_shape=[
            jax.ShapeDtypeStruct((rows, d), BF16),
            jax.ShapeDtypeStruct((bs, d), F32),
            jax.ShapeDtypeStruct((n_tap - 1, bs, d), F32),
        ],
        compiler_params=_cparams(1),
        name="rg_sample",
    )(p, p, cbuf_t, h0, conv_w, conv_b, wa, ba, wx, bx, lam)


def _shift_rows(x, prev_row, rowi):
    return jnp.where(rowi == 0, prev_row, pltpu.roll(x, 1, 0))


def _head_split(z, lane_head):
    return jnp.concatenate([jnp.where(lane_head == 0, z, 0.0), jnp.where(lane_head == 1, z, 0.0)], axis=0)


def _solve_unit_lower(n_mats, rhss, lane_head, xbd_ref, between_stages=None):
    n_sys = len(n_mats)
    c = n_mats[0].shape[0]
    xbd_ref[...] = jnp.zeros(xbd_ref.shape, F32)
    done = [[] for _ in range(n_sys)]
    for blk in range(c // SUBLANES):
        lo = blk * SUBLANES
        rs = []
        for i in range(n_sys):
            r = rhss[i][lo:lo + SUBLANES, :]
            if blk > 0:
                r = r - jnp.dot(n_mats[i][lo:lo + SUBLANES, :], xbd_ref[i], preferred_element_type=F32)
            rs.append(r)
        if between_stages is not None:
            between_stages(blk)
        for j in range(SUBLANES - 1):
            for i in range(n_sys):
                nrow = n_mats[i][lo:lo + SUBLANES, :]
                mult = jnp.where(lane_head == 0, nrow[:, lo + j:lo + j + 1], nrow[:, HEAD + lo + j:HEAD + lo + j + 1])
                rs[i] = rs[i] - mult * rs[i][j:j + 1, :]
        for i in range(n_sys):
            done[i].append(rs[i])
            xbd_ref[i, lo:lo + SUBLANES, :] = jnp.where(lane_head == 0, rs[i], 0.0)
            xbd_ref[i, c + lo:c + lo + SUBLANES, :] = jnp.where(lane_head == 1, rs[i], 0.0)
    return [jnp.concatenate(b, axis=0) for b in done]


def _rwkv_prompt_kernel(pr_ref, pk_ref, pv_ref, pwd_ref, pad_ref, pgd_ref,
                        mur_ref, muk_ref, muv_ref, muwd_ref, muad_ref, mugd_ref,
                        w0_ref, w2_ref, a0_ref, a2_ref, g2_ref, kkw_ref, kaw_ref, rkw_ref, gnw_ref, gnb_ref,
                        ob_ref, s_ref,
                        r_s, lw_s, k_s, v_s, kap_s, bet_s, g_s, bon_s, y_s, tw_s, za_s, sg_s, lm_s, yb_s, xbd_s,
                        *, t_len, rc, group):
    ones_bd = _head_ones()
    n_ch = t_len // rc
    rowi = lax.broadcasted_iota(jnp.int32, (rc, 1), 0)

    def shifted(refs, mus, carry, r0):
        zs, lasts = [], []
        for ref, mu, prev in zip(refs, mus, carry):
            p = ref[pl.ds(r0, rc), :].astype(F32)
            zs.append(p + mu[...] * (_shift_rows(p, prev, rowi) - p))
            lasts.append(p[rc - 1:rc, :])
        return zs, tuple(lasts)

    @pl.when(pl.program_id(1) == 0)
    def _():
        lora_refs = (pwd_ref, pad_ref, pgd_ref)

        def phase0(c, carry):
            r0 = pl.multiple_of(c * rc, rc)
            (zwd, zad, zgd), lasts = shifted(lora_refs, (muwd_ref, muad_ref, mugd_ref), carry, r0)
            rows = pl.ds(r0, rc)
            tw_s[rows, :] = jnp.tanh(zwd).astype(BF16)
            za_s[rows, :] = zad.astype(BF16)
            sg_s[rows, :] = jax.nn.sigmoid(zgd).astype(BF16)
            return lasts

        lax.fori_loop(0, n_ch, phase0, tuple(jnp.zeros((1, ref.shape[1]), F32) for ref in lora_refs))

    def phase1(c, carry):
        r0 = pl.multiple_of(c * rc, rc)
        (zr, zk, zv), lasts = shifted((pr_ref, pk_ref, pv_ref), (mur_ref, muk_ref, muv_ref), carry, r0)
        rows = pl.ds(r0, rc)
        lw, g, kk, k2, beta, bonus = _rwkv_prep(
            zr, zk, zv, tw_s[rows, :], za_s[rows, :], sg_s[rows, :],
            w0_ref[...], w2_ref[...], a0_ref[...], a2_ref[...], g2_ref[...],
            kkw_ref[...], kaw_ref[...], rkw_ref[...], ones_bd)
        r_s[rows, :] = zr
        lw_s[rows, :] = lw
        k_s[rows, :] = k2
        v_s[rows, :] = zv
        kap_s[rows, :] = kk
        bet_s[rows, :] = beta
        g_s[rows, :] = g
        bon_s[rows, :] = bonus
        return lasts

    lax.fori_loop(0, n_ch, phase1, tuple(jnp.zeros((1, LANES), F32) for _ in range(3)))

    c_len = CHUNK
    slot = c_len + LANES
    ti = lax.broadcasted_iota(jnp.int32, (c_len, LANES), 0)
    si = lax.broadcasted_iota(jnp.int32, (c_len, LANES), 1) % HEAD
    strict = ti > si
    incl = ti >= si
    lri = lax.broadcasted_iota(jnp.int32, (c_len, c_len), 0)
    lci = lax.broadcasted_iota(jnp.int32, (c_len, c_len), 1)
    l_cum = (lri >= lci).astype(BF16)
    lane_head = lax.broadcasted_iota(jnp.int32, (1, LANES), 1) // HEAD
    lane_head2 = jnp.concatenate([lane_head, lane_head], axis=1)
    bri = lax.broadcasted_iota(jnp.int32, (LANES, LANES), 0)
    bci = lax.broadcasted_iota(jnp.int32, (LANES, LANES), 1)
    same_head = (bri // HEAD) == (bci // HEAD)
    same_head2 = jnp.concatenate([same_head, same_head], axis=1)
    eye = bri == bci
    zc = jnp.zeros((c_len, LANES), F32)
    nt_dims = (((1,), (1,)), ((), ()))

    def precompute(gi, between_stages):
        chunks = [gi * group + cc for cc in range(group)]
        rows = [pl.ds(pl.multiple_of(c * c_len, c_len), c_len) for c in chunks]
        lws = [lw_s[rw, :] for rw in rows]
        gcums = [_dot_exact_lhs(l_cum, lw) for lw in lws]
        g_ends = [g[c_len - 1:c_len, :] for g in gcums]
        kts = [kap_s[rw, :] * jnp.exp(g - lw) for rw, g, lw in zip(rows, gcums, lws)]
        rts = [r_s[rw, :] * jnp.exp(g) for rw, g in zip(rows, gcums)]
        e_negs = [jnp.exp(-g) for g in gcums]
        lhss = [jnp.concatenate([kt, rt], axis=0) for kt, rt in zip(kts, rts)]
        o_bs = [lax.dot_general(lhs, _head_split(bet_s[rw, :] * en, lane_head), nt_dims, preferred_element_type=F32)
                for lhs, rw, en in zip(lhss, rows, e_negs)]
        o_ks = [lax.dot_general(lhs, _head_split(k_s[rw, :] * en, lane_head), nt_dims, preferred_element_type=F32)
                for lhs, rw, en in zip(lhss, rows, e_negs)]
        n_mats = [jnp.where(strict, o[0:c_len], 0.0) for o in o_bs]
        v_bds = [_head_split(v_s[rw, :], lane_head) for rw in rows]
        avs = [jnp.dot(jnp.where(strict, o[0:c_len], 0.0), v_bd, preferred_element_type=F32)
               for o, v_bd in zip(o_ks, v_bds)]
        rhss = [jnp.concatenate([kt, av], axis=1) for kt, av in zip(kts, avs)]
        a_rs = [jnp.concatenate([jnp.where(incl, -ob[c_len:], 0.0), jnp.where(incl, ok[c_len:], 0.0)], axis=1)
                for ob, ok in zip(o_bs, o_ks)]
        e_hats = [jnp.exp(ge - g) for ge, g in zip(g_ends, gcums)]
        bk_ts = [jnp.concatenate([bet_s[rw, :] * eh, k_s[rw, :] * eh], axis=0).T for rw, eh in zip(rows, e_hats)]
        xs = _solve_unit_lower(n_mats, rhss, lane_head2, xbd_s, between_stages)
        wus = [-x for x in xs]
        tops = [jnp.dot(a_r, jnp.concatenate([xbd_s[i],
                                              jnp.concatenate([jnp.zeros((LANES, LANES), F32), v_bd], axis=1)],
                                             axis=0), preferred_element_type=F32)
                for i, (a_r, v_bd) in enumerate(zip(a_rs, v_bds))]
        bots = [jnp.dot(bk_t, jnp.concatenate([wu, jnp.concatenate([zc, v_s[rw, :]], axis=1)], axis=0),
                        preferred_element_type=F32)
                for bk_t, wu, rw in zip(bk_ts, wus, rows)]
        for c, top, bot, rt, ge in zip(chunks, tops, bots, rts, g_ends):
            bot = jnp.where(same_head2, bot, 0.0)
            base = pl.multiple_of((c + group) * slot, SUBLANES)
            lm_s[pl.ds(base, c_len), :] = top[:, 0:LANES] + rt
            lm_s[pl.ds(base + c_len, LANES), :] = bot[:, 0:LANES] + jnp.where(eye, jnp.exp(ge), 0.0)
            yb_s[pl.ds(base, c_len), :] = top[:, LANES:]
            yb_s[pl.ds(base + c_len, LANES), :] = bot[:, LANES:]

    def advance(c, h_bd):
        base = pl.multiple_of((c + group) * slot, SUBLANES)
        res = _dot_3pass(lm_s[pl.ds(base, slot), :], h_bd) + yb_s[pl.ds(base, slot), :]
        y_s[pl.ds(pl.multiple_of((c + group) * c_len, c_len), c_len), :] = res[0:c_len]
        return res[c_len:]

    def phase3(c, carry):
        r0 = pl.multiple_of(c * rc, rc)
        o = _rwkv_post(y_s[pl.ds(r0 + group * c_len, rc), :], bon_s[pl.ds(r0, rc), :], g_s[pl.ds(r0, rc), :],
                       gnw_ref[...], gnb_ref[...], ones_bd)
        ob_ref[pl.ds(r0, rc), :] = o.astype(ob_ref.dtype)
        return carry

    assert group <= c_len // SUBLANES
    n_groups = t_len // (c_len * group)
    lm_s[0:group * slot, :] = jnp.zeros((group * slot, LANES), F32)
    yb_s[0:group * slot, :] = jnp.zeros((group * slot, LANES), F32)

    def fused(gi, h):
        state = [h]

        def between_stages(blk):
            if blk < group:
                state[0] = advance((gi - 1) * group + blk, state[0])

        precompute(gi, between_stages)
        return state[0]

    h_bd = lax.fori_loop(0, n_groups, fused, jnp.zeros((LANES, LANES), F32))
    n_ready = min(group, ((n_groups - 1) * group * c_len) // rc)
    for blk in range(group):
        h_bd = advance(jnp.int32((n_groups - 1) * group + blk), h_bd)
        if blk < n_ready:
            phase3(jnp.int32(blk), 0)
    s_bd = h_bd.T
    s_ref[0, 0] = s_bd[0:HEAD, 0:HEAD]
    s_ref[0, 1] = s_bd[HEAD:, HEAD:]
    lax.fori_loop(n_ready, n_ch, phase3, 0)


def _rwkv_specs(row_block, rows, q_rows, lay):
    pw, pa, pg, nb = lay["pw"], lay["pa"], lay["pg"], lay["nb"]

    def im(col_fn, on_rows=False):
        if row_block is None:
            return lambda b, hp: (b if on_rows else 0, col_fn(hp))
        return lambda hp: (row_block if on_rows else 0, col_fn(hp))

    def triple(n_rows, on_rows, c_r, c_k, c_v, c_wd, c_ad, c_gd):
        return [
            pl.BlockSpec((n_rows, LANES), im(lambda hp: c_r + hp, on_rows)),
            pl.BlockSpec((n_rows, LANES), im(lambda hp: c_k + hp, on_rows)),
            pl.BlockSpec((n_rows, LANES), im(lambda hp: c_v + hp, on_rows)),
            pl.BlockSpec((n_rows, pw * LANES), im(lambda hp: c_wd // pw, on_rows)),
            pl.BlockSpec((n_rows, pa * LANES), im(lambda hp: c_ad // pa, on_rows)),
            pl.BlockSpec((n_rows, pg * LANES), im(lambda hp: c_gd // pg, on_rows)),
        ]

    specs = triple(rows, True, lay["r"], lay["k"], lay["v"], lay["wd"], lay["ad"], lay["gd"])
    if q_rows:
        specs += triple(q_rows, False, 0, nb, 2 * nb, 0, pw, pw + pa)
    specs += triple(1, False, 0, nb, 2 * nb, 0, pw, pw + pa)
    per_pair = im(lambda hp: hp)
    specs += [
        pl.BlockSpec((1, LANES), per_pair),
        pl.BlockSpec((pw * LANES, LANES), per_pair),
        pl.BlockSpec((1, LANES), per_pair),
        pl.BlockSpec((pa * LANES, LANES), per_pair),
        pl.BlockSpec((pg * LANES, LANES), per_pair),
        pl.BlockSpec((1, LANES), per_pair),
        pl.BlockSpec((1, LANES), per_pair),
        pl.BlockSpec((1, LANES), per_pair),
        pl.BlockSpec((1, LANES), per_pair),
        pl.BlockSpec((1, LANES), per_pair),
    ]
    return specs


def _rwkv_prompt(p_main, p_tail, mu_rkv, mu_lora, params, *, bp, tp, d, lay):
    n_hp = d // LANES
    pw, pa, pg = lay["pw"], lay["pa"], lay["pg"]
    rc = 256 if tp % 256 == 0 else tp
    n_heads = d // HEAD
    n_chunks = tp // CHUNK
    group = next(g for g in (8, 4, 2, 1) if n_chunks % g == 0)
    kern = functools.partial(_rwkv_prompt_kernel, t_len=tp, rc=rc, group=group)
    vm = pltpu.VMEM((tp, LANES), F32)
    return pl.pallas_call(
        kern,
        grid=(bp, n_hp),
        in_specs=_rwkv_specs(None, tp, 0, lay),
        out_specs=[
            pl.BlockSpec((tp, LANES), lambda b, hp: (b, hp)),
            pl.BlockSpec((1, 2, HEAD, HEAD), lambda b, hp: (b, hp, 0, 0)),
        ],
        out_shape=[
            jax.ShapeDtypeStruct((bp * tp, d), BF16),
            jax.ShapeDtypeStruct((bp, n_heads, HEAD, HEAD), F32),
        ],
        scratch_shapes=[vm] * 8 + [
            pltpu.VMEM((tp + group * CHUNK, LANES), F32),
            pltpu.VMEM((tp, pw * LANES), BF16),
            pltpu.VMEM((tp, pa * LANES), BF16),
            pltpu.VMEM((tp, pg * LANES), BF16),
            pltpu.VMEM(((n_chunks + group) * (CHUNK + LANES), LANES), F32),
            pltpu.VMEM(((n_chunks + group) * (CHUNK + LANES), LANES), F32),
            pltpu.VMEM((group, 2 * CHUNK, 2 * LANES), F32),
        ],
        compiler_params=_cparams(2),
        name="rwkv_prompt",
    )(p_main, p_main, p_main, p_tail, p_tail, p_tail,
      mu_rkv, mu_rkv, mu_rkv, mu_lora, mu_lora, mu_lora, *params)


def _rwkv_sample_kernel(pr_ref, pk_ref, pv_ref, pwd_ref, pad_ref, pgd_ref,
                        qr_ref, qk_ref, qv_ref, qwd_ref, qad_ref, qgd_ref,
                        mur_ref, muk_ref, muv_ref, muwd_ref, muad_ref, mugd_ref,
                        w0_ref, w2_ref, a0_ref, a2_ref, g2_ref, kkw_ref, kaw_ref, rkw_ref, gnw_ref, gnb_ref,
                        st_ref, ob_ref, so_ref, y_s, tv_s, *, ts, bs):
    ones_bd = _head_ones()
    refs = (pr_ref, pk_ref, pv_ref, pwd_ref, pad_ref, pgd_ref)
    prevs = (qr_ref, qk_ref, qv_ref, qwd_ref, qad_ref, qgd_ref)
    mus = (mur_ref, muk_ref, muv_ref, muwd_ref, muad_ref, mugd_ref)
    zs = []
    for ref, q, mu in zip(refs, prevs, mus):
        p = ref[...].astype(F32)
        pp = jnp.concatenate([q[...].astype(F32), p[0:(ts - 1) * bs, :]], axis=0)
        zs.append(p + mu[...] * (pp - p))
    zr, zk, zv, zwd, zad, zgd = zs
    lw, g, kk, k2, beta, bonus = _rwkv_prep(
        zr, zk, zv, jnp.tanh(zwd).astype(BF16), zad.astype(BF16), jax.nn.sigmoid(zgd).astype(BF16),
        w0_ref[...], w2_ref[...], a0_ref[...], a2_ref[...], g2_ref[...],
        kkw_ref[...], kaw_ref[...], rkw_ref[...], ones_bd)
    w_dec = jnp.exp(lw)

    for t in range(ts):
        rows = slice(t * bs, (t + 1) * bs)
        for kind, a in enumerate((w_dec, kk, beta, k2, zr, zv)):
            tv_s[t, kind] = a[rows, :].T

    def total(p):
        p = p + pltpu.roll(p, 4, 0)
        p = p + pltpu.roll(p, 2, 0)
        return p + pltpu.roll(p, 1, 0)

    groups = HEAD // SUBLANES
    for h in range(2):
        lo = h * HEAD
        for i0 in range(0, HEAD, _SAMPLE_ROWS):
            s_rows = [st_ref[h, i0 + i].reshape(groups, SUBLANES, bs) for i in range(_SAMPLE_ROWS)]
            for t in range(ts):
                wt, kkt, bt, kt, rt = (tv_s[t, kind, lo:lo + HEAD, :].reshape(groups, SUBLANES, bs)
                                       for kind in range(5))
                for i in range(_SAMPLE_ROWS):
                    s = s_rows[i]
                    row = lo + i0 + i
                    sa = -total(jnp.sum(s * kkt, axis=0))
                    s = s * wt + sa[None] * bt + tv_s[t, 5, row:row + 1, :][None] * kt
                    s_rows[i] = s
                    y_s[t, row:row + 1, :] = total(jnp.sum(s * rt, axis=0))[0:1]
            for i in range(_SAMPLE_ROWS):
                so_ref[h, i0 + i] = s_rows[i].reshape(HEAD, bs)
    for t in range(ts):
        yt = y_s[t].T
        rows = slice(t * bs, (t + 1) * bs)
        o = _rwkv_post(yt, bonus[rows], g[rows], gnw_ref[...], gnb_ref[...], ones_bd)
        ob_ref[rows, :] = o.astype(ob_ref.dtype)


def _rwkv_sample(p_main, p_tail, q_rkv, q_lora, mu_rkv, mu_lora, params, st, *, ts, bs, d, mp, lay):
    n_hp = d // LANES
    rows = ts * bs
    rblk = mp // rows
    in_specs = _rwkv_specs(rblk, rows, bs, lay)
    in_specs.append(pl.BlockSpec((2, HEAD, HEAD, bs), lambda hp: (hp, 0, 0, 0)))
    kern = functools.partial(_rwkv_sample_kernel, ts=ts, bs=bs)
    return pl.pallas_call(
        kern,
        grid=(n_hp,),
        in_specs=in_specs,
        out_specs=[
            pl.BlockSpec((rows, LANES), lambda hp: (0, hp)),
            pl.BlockSpec((2, HEAD, HEAD, bs), lambda hp: (hp, 0, 0, 0)),
        ],
        out_shape=[
            jax.ShapeDtypeStruct((rows, d), BF16),
            jax.ShapeDtypeStruct(st.shape, F32),
        ],
        scratch_shapes=[pltpu.VMEM((ts, LANES, bs), F32), pltpu.VMEM((ts, 6, LANES, bs), F32)],
        compiler_params=_cparams(1),
        name="rwkv_sample",
    )(p_main, p_main, p_main, p_tail, p_tail, p_tail, q_rkv, q_rkv, q_rkv, q_lora, q_lora, q_lora,
      mu_rkv, mu_rkv, mu_rkv, mu_lora, mu_lora, mu_lora, *params, st)


def _pad_cols(a, width):
    return jnp.pad(a, ((0, 0), (0, width - a.shape[1])))


def _pad_rows(a, height):
    return jnp.pad(a, ((0, height - a.shape[0]), (0, 0)))


def _tiles(d, d_ff, n_tail):
    pick = lambda n, cands: next(c for c in cands if n % c == 0)
    return dict(
        tm=512, tm_wide=256,
        main=pick(5 * d, (1280, 1024, 512, 256, 128)),
        tail=pick(n_tail, (1536, 1024, 512, 256, 128)),
        prev=pick(d, (1024, 512, 256, 128)),
        up=pick(d_ff, (1024, 512, 256, 128)),
        down=pick(d, (512, 256, 128)),
    )


def kernel(x_prompt, x_sample, p_prompt, p_sample, state_rg_h, state_rg_conv, state_rwkv, state_shift,
           norm_mix, w_in, conv_w, conv_b, rg_wa, rg_ba, rg_wx, rg_bx, rg_lam, w_rg_o,
           mu_shift, rw_w0, rw_w2, rw_a0, rw_a2, rw_g2, rw_kk, rw_ka, rw_rk, rw_gn_w, rw_gn_b,
           w_rw_o, w_o, norm_ffn, w_up, w_down, norm_ple, w_ple_gate, w_ple, norm_f):
    bp, tp, d = x_prompt.shape
    bs, ts, _ = x_sample.shape
    depth = w_in.shape[0]
    n_heads, head = rw_rk.shape[1], rw_rk.shape[2]
    r_w, r_a, r_g = rw_w2.shape[1], rw_a2.shape[1], rw_g2.shape[1]
    d_ple = w_ple.shape[1]
    d_ff = w_up.shape[2]
    n_tap = conv_w.shape[1]
    assert head == HEAD and n_heads * HEAD == d and d % LANES == 0 and bs == LANES
    assert rg_wa.shape[2] == LANES and tp % CHUNK == 0 and n_tap == 4
    mp, ms = bp * tp, bs * ts
    nb = d // LANES
    pw, pa, pg = (_round_up(r, LANES) // LANES for r in (r_w, r_a, r_g))
    n_lora = (pw + pa + pg) * LANES
    n_tail = 2 * d + n_lora
    lay = dict(u=0, g=nb, r=2 * nb, k=3 * nb, v=4 * nb, wd=2 * nb, ad=2 * nb + pw, gd=2 * nb + pw + pa,
               pw=pw, pa=pa, pg=pg, nb=nb)
    t = _tiles(d, d_ff, n_tail)
    tm = t["tm"]
    assert mp % tm == 0 and ms % tm == 0 and mp % (ts * bs) == 0 and ms % t["tm_wide"] == 0
    assert lay["wd"] % pw == 0 and lay["ad"] % pa == 0 and lay["gd"] % pg == 0 and pw % pa == 0 and (pw + pa) % pg == 0
    assert (2 * d) % t["prev"] == 0 and (2 * d) % 512 == 0 and n_lora % 512 == 0 and d % 512 == 0
    mm = functools.partial(_mm, mp=mp, ms=ms)

    x = _Rows(x_prompt.reshape(mp, d), jnp.transpose(x_sample, (1, 0, 2)).reshape(ms, d))
    hp_l, cp_l, sp_l, xp_l, hs_l, cs_l, ss_l, xs_l = [], [], [], [], [], [], [], []
    y_p = y_s = None
    for i in range(depth):
        o_rw = 2 * d
        o_lora = o_rw + 3 * d
        o_g = o_lora + r_w + r_a + r_g
        wt = jnp.swapaxes(w_in[i], 0, 1)

        def lora_rows(a):
            return jnp.concatenate([
                _pad_rows(a[o_lora:o_lora + r_w], pw * LANES),
                _pad_rows(a[o_lora + r_w:o_lora + r_w + r_a], pa * LANES),
                _pad_rows(a[o_lora + r_w + r_a:o_g], pg * LANES)], axis=0)

        wt_tail = jnp.concatenate([wt[o_g:], lora_rows(wt)], axis=0)
        mu_all = jnp.concatenate([jnp.zeros((o_rw, 1), F32), mu_shift[i][:, None], jnp.zeros((2 * d, 1), F32)], axis=0)
        mu_rkv = mu_shift[i][None, :3 * d]
        mu_lora = lora_rows(mu_all).reshape(1, n_lora)
        rw_params = (
            rw_w0[i][None], _pad_rows(rw_w2[i], pw * LANES).astype(BF16),
            rw_a0[i][None], _pad_rows(rw_a2[i], pa * LANES).astype(BF16),
            _pad_rows(rw_g2[i], pg * LANES).astype(BF16),
            rw_kk[i][None], rw_ka[i][None], rw_rk[i].reshape(1, d), rw_gn_w[i][None], rw_gn_b[i][None])
        rg_params = (conv_w[i], conv_b[i][None], rg_wa[i].astype(BF16), rg_ba[i][:, None, :],
                     rg_wx[i].astype(BF16), rg_bx[i][:, None, :], rg_lam[i][None])

        xn = _Rows(_rmsnorm(x, norm_mix[i], BF16, tm, mp))
        xa = x.arrays if len(x.arrays) == 2 else (x.arrays[0][:mp], x.arrays[0][mp:])
        x_last = jnp.concatenate([xa[0].reshape(bp, tp, d)[:, -1], xa[1][(ts - 1) * bs:]], axis=0)
        n_last = _round_up(bp + bs, SUBLANES)
        xn_last = _rmsnorm(_Rows(_pad_rows(x_last, n_last)), norm_mix[i], F32, n_last, n_last)
        (p_main,) = mm([xn], [wt], [], _epi_plain, [(F32, False)], tm=tm, tn=t["main"], n_out=5 * d,
                       cast_w=True, w_transposed=True, name="in_proj")
        (p_tail,) = mm([xn], [wt_tail], [], _epi_plain, [(BF16, False)], tm=tm, tn=t["tail"], n_out=n_tail,
                       cast_w=True, w_transposed=True, name="in_proj_tail")
        xprev = _Rows(state_shift[i])
        (q_rkv,) = _mm([xprev], [wt], [], _epi_plain, [(F32, False)], mp=bs, ms=0, tm=bs, tn=t["prev"],
                       n_out=3 * d, w_col_off=o_rw // t["prev"], cast_w=True, w_transposed=True, name="prev_proj")
        (q_lora,) = _mm([xprev], [wt_tail], [], _epi_plain, [(BF16, False)], mp=bs, ms=0, tm=bs, tn=512,
                        n_out=n_lora, w_col_off=2 * d // 512, cast_w=True, w_transposed=True,
                        name="prev_proj_tail")

        cwid = 2 * LANES if nb % 2 == 0 else LANES
        hg_p, h_p, c_p = _rg_prompt(p_main, *rg_params, bp=bp, tp=tp, d=d, cb_u=lay["u"], cb_g=lay["g"], cwid=cwid)
        cbuf_t = jnp.transpose(state_rg_conv[i], (1, 0, 2))
        hg_s, h_s, c_s = _rg_sample(p_main, cbuf_t, state_rg_h[i], *rg_params, ts=ts, bs=bs, d=d, mp=mp,
                                    cb_u=lay["u"], cb_g=lay["g"], cwid=cwid)

        ob_p, s_p = _rwkv_prompt(p_main, p_tail, mu_rkv, mu_lora, rw_params, bp=bp, tp=tp, d=d, lay=lay)
        st = jnp.transpose(state_rwkv[i], (1, 2, 3, 0))
        ob_s, st_new = _rwkv_sample(p_main, p_tail, q_rkv, q_lora, mu_rkv, mu_lora, rw_params, st,
                                    ts=ts, bs=bs, d=d, mp=mp, lay=lay)
        s_s = jnp.transpose(st_new, (3, 0, 1, 2))

        tw = t["tm_wide"]
        (merged,) = mm([_Rows(hg_p, hg_s), _Rows(ob_p, ob_s)], [w_rg_o[i].astype(BF16), w_rw_o[i].astype(BF16)],
                       [("tile", _Rows(p_tail), 0), ("tile", _Rows(p_tail), 1)],
                       _epi_merge, [(BF16, False)], tm=tw, tn=d, n_out=d, single_buffer_w=True, name="merge")
        x1, xn2 = mm([_Rows(merged)], [w_o[i].astype(BF16)], [("tile", x, 0), ("row", norm_ffn[i][None], 0)],
                     _epi_residual_norm, [(F32, False), (BF16, False)], tm=tw, tn=d, n_out=d, name="out_proj")
        (hf,) = mm([_Rows(xn2)], [w_up[i]], [], _epi_relu2, [(BF16, False)], tm=tm, tn=t["up"], n_out=d_ff,
                   cast_w=True, name="mlp_up")
        (x2,) = mm([_Rows(hf)], [w_down[i].astype(BF16)], [("tile", _Rows(x1), 0)], _epi_residual, [(F32, False)],
                   tm=tm, tn=t["down"], n_out=d, name="mlp_down")
        xn3 = _rmsnorm(_Rows(x2), norm_ple[i], BF16, tm, mp)
        p_rows = _Rows(p_prompt[i].reshape(mp, d_ple), jnp.transpose(p_sample[i], (1, 0, 2)).reshape(ms, d_ple))
        ple_ws = [w_ple_gate[i].astype(BF16), w_ple[i].astype(BF16)]
        if i == depth - 1:
            y_p, y_s = mm([_Rows(xn3), p_rows], ple_ws, [("tile", _Rows(x2), 0), ("row", norm_f[None], 0)],
                          _epi_ple_final, [(F32, True)], tm=tw, tn=d, n_out=d, name="ple_final")
        else:
            x = _Rows(*mm([_Rows(xn3), p_rows], ple_ws, [("tile", _Rows(x2), 0)], _epi_ple, [(F32, False)],
                          tm=tw, tn=d, n_out=d, name="ple"))

        hp_l.append(h_p[:, 0])
        cp_l.append(c_p[:, SUBLANES - (n_tap - 1):])
        sp_l.append(s_p)
        xp_l.append(xn_last[:bp])
        hs_l.append(h_s)
        cs_l.append(jnp.transpose(c_s, (1, 0, 2)))
        ss_l.append(s_s)
        xs_l.append(xn_last[bp:bp + bs])

    y_prompt = y_p.reshape(bp, tp, d)
    y_sample = jnp.transpose(y_s.reshape(ts, bs, d), (1, 0, 2))
    return (y_prompt, y_sample,
            jnp.stack(hp_l), jnp.stack(cp_l), jnp.stack(sp_l), jnp.stack(xp_l),
            jnp.stack(hs_l), jnp.stack(cs_l), jnp.stack(ss_l), jnp.stack(xs_l))
```

```python
import functools

import jax
import jax.numpy as jnp
from jax import lax
from jax.experimental import pallas as pl
from jax.experimental.pallas import tpu as pltpu

F32 = jnp.float32
BF16 = jnp.bfloat16
LANES = 128
SUBLANES = 8
HEAD = 64
CHUNK = 64
_SAMPLE_ROWS = 4
_SOLVE_ROWS = 8
EPS = 1e-6
GN_EPS = 64e-5
RG_C = 8.0
VMEM_LIMIT = 56 * 1024 * 1024


def _cparams(n_grid, vmem=VMEM_LIMIT):
    return pltpu.CompilerParams(dimension_semantics=("arbitrary",) * n_grid, vmem_limit_bytes=vmem)


def _round_up(x, m):
    return (x + m - 1) // m * m


def _rms(x, g):
    ms = jnp.mean(x * x, axis=-1, keepdims=True)
    return (x * lax.rsqrt(ms + EPS)) * g


class _Rows:
    def __init__(self, *arrays):
        self.arrays = arrays

    @property
    def width(self):
        return self.arrays[0].shape[1]

    def specs(self, tm, n_p, col):
        if len(self.arrays) == 1:
            return [lambda wid: pl.BlockSpec((tm, wid), lambda n, i: (i, col(n)))]
        return [lambda wid: pl.BlockSpec((tm, wid), lambda n, i: (jnp.minimum(i, n_p - 1), col(n))),
                lambda wid: pl.BlockSpec((tm, wid), lambda n, i: (jnp.maximum(i - n_p, 0), col(n)))]


def _load_rows(refs, is_prompt):
    if len(refs) == 1:
        return refs[0][...]
    return jnp.where(is_prompt, refs[0][...], refs[1][...])


def _rmsnorm_kernel(*refs, n_src, n_p):
    g_ref, o_ref = refs[n_src], refs[n_src + 1]
    x = _load_rows(refs[:n_src], pl.program_id(1) < n_p)
    o_ref[...] = _rms(x, g_ref[...]).astype(o_ref.dtype)


def _rmsnorm(x, g, out_dtype, tm, mp):
    d = x.width
    m = sum(a.shape[0] for a in x.arrays)
    n_p = mp // tm
    in_specs = [mk(d) for mk in x.specs(tm, n_p, lambda n: 0)] + [pl.BlockSpec((1, d), lambda n, i: (0, 0))]
    return pl.pallas_call(
        functools.partial(_rmsnorm_kernel, n_src=len(x.arrays), n_p=n_p),
        grid=(1, m // tm),
        in_specs=in_specs,
        out_specs=pl.BlockSpec((tm, d), lambda n, i: (i, 0)),
        out_shape=jax.ShapeDtypeStruct((m, d), out_dtype),
        compiler_params=_cparams(2),
        name="rmsnorm",
    )(*x.arrays, g.reshape(1, d))


def _mm_kernel(*refs, x_counts, x_norm, n_w, cast_w, w_transposed, extra_counts, out_counts, n_p, epilogue):
    pos = 0
    x_refs = []
    for cnt in x_counts:
        x_refs.append(refs[pos:pos + cnt])
        pos += cnt
    w_refs = refs[pos:pos + n_w]
    pos += n_w
    e_refs = []
    for cnt in extra_counts:
        e_refs.append(refs[pos:pos + cnt])
        pos += cnt
    o_refs = []
    for cnt in out_counts:
        o_refs.append(refs[pos:pos + cnt])
        pos += cnt
    w_scratch = refs[pos:]
    i = pl.program_id(1)
    is_prompt = i < n_p
    if cast_w:
        @pl.when(i == 0)
        def _():
            for w_ref, s_ref in zip(w_refs, w_scratch):
                s_ref[...] = w_ref[...].astype(BF16)
        ws = [s[...] for s in w_scratch]
    else:
        ws = [w[...] for w in w_refs]
    dims = (((1,), (1 if w_transposed else 0,)), ((), ()))
    e_vals = [_load_rows(er, is_prompt) for er in e_refs]
    x_vals = []
    for idx, xr in enumerate(x_refs):
        if xr:
            x_vals.append(_load_rows(xr, is_prompt).astype(BF16))
        else:
            e_idx, g_idx = x_norm[idx]
            x_vals.append(_rms(e_vals[e_idx], e_vals[g_idx]).astype(BF16))
    accs = [lax.dot_general(x, w, dims, preferred_element_type=F32) for x, w in zip(x_vals, ws)]
    outs = epilogue(accs, e_vals)
    for refs_o, tile in zip(o_refs, outs):
        if len(refs_o) == 1:
            refs_o[0][...] = tile.astype(refs_o[0].dtype)
        else:
            @pl.when(is_prompt)
            def _():
                refs_o[0][...] = tile.astype(refs_o[0].dtype)

            @pl.when(jnp.logical_not(is_prompt))
            def _():
                refs_o[1][...] = tile.astype(refs_o[1].dtype)


def _mm(xs, ws, extras, epilogue, outs, *, mp, ms, tm, tn, n_out, w_col_off=0, cast_w=False,
        w_transposed=False, single_buffer_w=False, name="mm"):
    n_p = mp // tm
    m = mp + ms
    grid = (n_out // tn, m // tm)
    in_specs, args, x_counts, extra_counts, x_norm = [], [], [], [], {}
    for idx, x in enumerate(xs):
        if isinstance(x, tuple):
            x_norm[idx] = (x[1], x[2])
            x_counts.append(0)
            continue
        in_specs += [mk(x.width) for mk in x.specs(tm, n_p, lambda n: 0)]
        args += list(x.arrays)
        x_counts.append(len(x.arrays))
    w_mode = dict(pipeline_mode=pl.Buffered(1)) if single_buffer_w else {}
    if w_transposed:
        w_blocks = [(tn, w.shape[1]) for w in ws]
        in_specs += [pl.BlockSpec(blk, lambda n, i: (n + w_col_off, 0), **w_mode) for blk in w_blocks]
    else:
        w_blocks = [(w.shape[0], tn) for w in ws]
        in_specs += [pl.BlockSpec(blk, lambda n, i: (0, n + w_col_off), **w_mode) for blk in w_blocks]
    args += list(ws)
    for kind, src, off in extras:
        if kind == "tile":
            in_specs += [mk(tn) for mk in src.specs(tm, n_p, lambda n, off=off: n + off)]
            args += list(src.arrays)
            extra_counts.append(len(src.arrays))
        elif kind == "row":
            in_specs.append(pl.BlockSpec((1, tn), lambda n, i, off=off: (0, n + off)))
            args.append(src)
            extra_counts.append(1)
        else:
            raise ValueError(kind)
    out_specs, out_shape, out_counts = [], [], []
    for dtype, split in outs:
        if split:
            out_specs += [pl.BlockSpec((tm, tn), lambda n, i: (jnp.minimum(i, n_p - 1), n)),
                          pl.BlockSpec((tm, tn), lambda n, i: (jnp.maximum(i - n_p, 0), n))]
            out_shape += [jax.ShapeDtypeStruct((mp, n_out), dtype), jax.ShapeDtypeStruct((ms, n_out), dtype)]
            out_counts.append(2)
        else:
            out_specs.append(pl.BlockSpec((tm, tn), lambda n, i: (i, n)))
            out_shape.append(jax.ShapeDtypeStruct((m, n_out), dtype))
            out_counts.append(1)
    kern = functools.partial(_mm_kernel, x_counts=tuple(x_counts), x_norm=x_norm, n_w=len(ws), cast_w=cast_w,
                             w_transposed=w_transposed, extra_counts=tuple(extra_counts),
                             out_counts=tuple(out_counts), n_p=n_p, epilogue=epilogue)
    scratch = [pltpu.VMEM(blk, BF16) for blk in w_blocks] if cast_w else []
    res = pl.pallas_call(
        kern,
        grid=grid,
        in_specs=in_specs,
        out_specs=out_specs,
        out_shape=out_shape,
        scratch_shapes=scratch,
        compiler_params=_cparams(2),
        name=name,
    )(*args)
    return res


def _epi_plain(accs, extras):
    return (accs[0],)


def _epi_residual(accs, extras):
    return (extras[0] + accs[0],)


def _epi_merge(accs, extras):
    ga, gb = extras
    return (jax.nn.sigmoid(ga.astype(F32)) * accs[0] + jax.nn.sigmoid(gb.astype(F32)) * accs[1],)


def _epi_residual_norm(accs, extras):
    x = extras[0] + accs[0]
    return x, _rms(x, extras[1])


def _epi_relu2(accs, extras):
    h = jnp.maximum(accs[0], 0.0)
    return (h * h,)


def _epi_ple(accs, extras):
    return (extras[0] + jax.nn.sigmoid(accs[0]) * accs[1],)


def _epi_ple_final(accs, extras):
    return (_rms(extras[0] + jax.nn.sigmoid(accs[0]) * accs[1], extras[1]),)


def _softplus(x):
    return jnp.maximum(x, 0.0) + jnp.log1p(jnp.exp(-jnp.abs(x)))


def _split_bf16(x, parts):
    out = []
    rem = x
    for _ in range(parts):
        p = rem.astype(BF16)
        out.append(p)
        rem = rem - p.astype(F32)
    return out


def _dot_exact_lhs(a_bf16, x, parts=3):
    acc = None
    for p in _split_bf16(x, parts):
        t = jnp.dot(a_bf16, p, preferred_element_type=F32)
        acc = t if acc is None else acc + t
    return acc


def _head_sum(x, ones_bd):
    return jnp.dot(x.astype(BF16), ones_bd, preferred_element_type=F32)


def _sigmoid(x):
    return 0.5 * jnp.tanh(0.5 * x) + 0.5


def _dot_3pass(a, b):
    a_hi, a_lo = _split_bf16(a, 2)
    b_hi, b_lo = _split_bf16(b, 2)
    return (jnp.dot(a_hi, b_hi, preferred_element_type=F32)
            + jnp.dot(a_hi, b_lo, preferred_element_type=F32)
            + jnp.dot(a_lo, b_hi, preferred_element_type=F32))


def _head_ones():
    r = lax.broadcasted_iota(jnp.int32, (LANES, LANES), 0) // HEAD
    c = lax.broadcasted_iota(jnp.int32, (LANES, LANES), 1) // HEAD
    return (r == c).astype(BF16)


def _rwkv_prep(zr, zk, zv, tw, za, sg, w0, w2, a0, a2, g2, kkw, kaw, rkw, ones_bd):
    wlog = -_softplus(-(w0 + jnp.dot(tw, w2, preferred_element_type=F32))) - 0.5
    lw = -jnp.exp(wlog)
    a = _sigmoid(a0 + jnp.dot(za, a2, preferred_element_type=F32))
    g = jnp.dot(sg, g2, preferred_element_type=F32)
    kk = zk * kkw
    ss = _head_sum(kk * kk, ones_bd)
    kk = kk * lax.rsqrt(jnp.maximum(ss, 1e-24))
    k2 = zk * (1.0 + (a - 1.0) * kaw)
    beta = kk * a
    bonus = _head_sum(zr * k2 * rkw, ones_bd) * zv
    return lw, g, kk, k2, beta, bonus


def _rwkv_post(y, bonus, g, gnw, gnb, ones_bd):
    mu = _head_sum(y, ones_bd) * (1.0 / HEAD)
    d = y - mu
    var = _head_sum(d * d, ones_bd) * (1.0 / HEAD)
    yn = (d * lax.rsqrt(var + GN_EPS)) * gnw + gnb
    return (yn + bonus) * g


def _rg_gates(xc, wa, ba, wx, bx, sp):
    xcb = xc.astype(BF16)
    r = _sigmoid(jnp.dot(xcb, wa, preferred_element_type=F32) + ba)
    i = _sigmoid(jnp.dot(xcb, wx, preferred_element_type=F32) + bx)
    log_a = (-RG_C * r) * sp
    a = jnp.exp(log_a)
    mult = jnp.sqrt(-jnp.tanh(log_a) * (a * a + 1.0))
    return a, mult, i * xc


def _rg_prompt_kernel(u_ref, gt_ref, cw_ref, cb_ref, wa_ref, ba_ref, wx_ref, bx_ref, lam_ref,
                      hg_ref, hl_ref, cs_ref, us_ref, a_ref, b_ref, *, t_len, rc):
    cwid = u_ref.shape[1]
    nb = cwid // LANES
    us_ref[0:SUBLANES, :] = jnp.zeros((SUBLANES, cwid), F32)
    us_ref[SUBLANES:, :] = u_ref[...]
    sp = _softplus(-lam_ref[...])
    w0 = cw_ref[0:1, :]
    w1 = cw_ref[1:2, :]
    w2 = cw_ref[2:3, :]
    w3 = cw_ref[3:4, :]
    cb = cb_ref[...]
    n_ch = t_len // rc

    def phase1(c, carry):
        r0 = pl.multiple_of(c * rc, rc)
        e = us_ref[pl.ds(r0, rc + SUBLANES), :]
        u0 = e[SUBLANES:]
        u1 = pltpu.roll(e, 1, 0)[SUBLANES:]
        u2 = pltpu.roll(e, 2, 0)[SUBLANES:]
        u3 = pltpu.roll(e, 3, 0)[SUBLANES:]
        xc = cb + (u3 * w0 + u2 * w1 + u1 * w2 + u0 * w3)
        row = lax.broadcasted_iota(jnp.int32, (rc, LANES), 0) + r0
        for n in range(nb):
            sl = slice(n * LANES, (n + 1) * LANES)
            a, mult, ix = _rg_gates(xc[:, sl], wa_ref[n], ba_ref[n], wx_ref[n], bx_ref[n], sp[:, sl])
            mult = jnp.where(row == 0, 1.0, mult)
            a_ref[n, pl.ds(r0, rc), :] = a
            b_ref[n, pl.ds(r0, rc), :] = mult * ix
        return carry

    lax.fori_loop(0, n_ch, phase1, 0)

    rowi = lax.broadcasted_iota(jnp.int32, (SUBLANES, LANES), 0)

    def scan(i, h_prev):
        r0 = pl.multiple_of(i * SUBLANES, SUBLANES)
        out = []
        for n in range(nb):
            a = a_ref[n, pl.ds(r0, SUBLANES), :]
            b = b_ref[n, pl.ds(r0, SUBLANES), :]
            for d in (1, 2, 4):
                a_sh = jnp.where(rowi < d, 1.0, pltpu.roll(a, d, 0))
                b_sh = jnp.where(rowi < d, 0.0, pltpu.roll(b, d, 0))
                b = a * b_sh + b
                a = a * a_sh
            h = b + a * h_prev[n]
            b_ref[n, pl.ds(r0, SUBLANES), :] = h
            out.append(h[SUBLANES - 1:SUBLANES, :])
        return tuple(out)

    h_last = lax.fori_loop(0, t_len // SUBLANES, scan, tuple(jnp.zeros((1, LANES), F32) for _ in range(nb)),
                           unroll=4)
    hl_ref[0] = jnp.concatenate(list(h_last), axis=1)
    cs_ref[0] = u_ref[pl.ds(t_len - SUBLANES, SUBLANES), :]

    def phase3(c, carry):
        r0 = pl.multiple_of(c * rc, rc)
        for n in range(nb):
            sl = slice(n * LANES, (n + 1) * LANES)
            gate = gt_ref[pl.ds(r0, rc), sl]
            hg_ref[pl.ds(r0, rc), sl] = (b_ref[n, pl.ds(r0, rc), :] * jax.nn.gelu(gate)).astype(hg_ref.dtype)
        return carry

    lax.fori_loop(0, n_ch, phase3, 0)


def _rg_prompt(p, conv_w, conv_b, wa, ba, wx, bx, lam, *, bp, tp, d, cb_u, cb_g, cwid):
    nb = cwid // LANES
    nblk = d // cwid
    rc = 256 if tp % 256 == 0 else tp
    kern = functools.partial(_rg_prompt_kernel, t_len=tp, rc=rc)
    return pl.pallas_call(
        kern,
        grid=(bp, nblk),
        in_specs=[
            pl.BlockSpec((tp, cwid), lambda b, n: (b, cb_u * LANES // cwid + n)),
            pl.BlockSpec((tp, cwid), lambda b, n: (b, cb_g * LANES // cwid + n)),
            pl.BlockSpec((4, cwid), lambda b, n: (0, n)),
            pl.BlockSpec((1, cwid), lambda b, n: (0, n)),
            pl.BlockSpec((nb, LANES, LANES), lambda b, n: (n, 0, 0)),
            pl.BlockSpec((nb, 1, LANES), lambda b, n: (n, 0, 0)),
            pl.BlockSpec((nb, LANES, LANES), lambda b, n: (n, 0, 0)),
            pl.BlockSpec((nb, 1, LANES), lambda b, n: (n, 0, 0)),
            pl.BlockSpec((1, cwid), lambda b, n: (0, n)),
        ],
        out_specs=[
            pl.BlockSpec((tp, cwid), lambda b, n: (b, n)),
            pl.BlockSpec((1, 1, cwid), lambda b, n: (b, 0, n)),
            pl.BlockSpec((1, SUBLANES, cwid), lambda b, n: (b, 0, n)),
        ],
        out_shape=[
            jax.ShapeDtypeStruct((bp * tp, d), BF16),
            jax.ShapeDtypeStruct((bp, 1, d), F32),
            jax.ShapeDtypeStruct((bp, SUBLANES, d), F32),
        ],
        scratch_shapes=[
            pltpu.VMEM((tp + SUBLANES, cwid), F32),
            pltpu.VMEM((nb, tp, LANES), F32),
            pltpu.VMEM((nb, tp, LANES), F32),
        ],
        compiler_params=_cparams(2),
        name="rg_prompt",
    )(p, p, conv_w, conv_b, wa, ba, wx, bx, lam)


def _rg_sample_kernel(u_ref, gt_ref, cbuf_ref, h0_ref, cw_ref, cb_ref, wa_ref, ba_ref, wx_ref, bx_ref, lam_ref,
                      hg_ref, hl_ref, cs_ref, *, ts, bs):
    cwid = u_ref.shape[1]
    nb = cwid // LANES
    n_tap = cw_ref.shape[0]
    sp = _softplus(-lam_ref[...])
    cb = cb_ref[...]
    ext = [cbuf_ref[j] for j in range(n_tap - 1)] + [u_ref[t * bs:(t + 1) * bs, :] for t in range(ts)]
    for j in range(n_tap - 1):
        cs_ref[j] = ext[len(ext) - (n_tap - 1) + j]
    h = h0_ref[...]
    for t in range(ts):
        conv = ext[t] * cw_ref[0:1, :]
        for j in range(1, n_tap):
            conv = conv + ext[t + j] * cw_ref[j:j + 1, :]
        xc = cb + conv
        pieces = []
        for n in range(nb):
            sl = slice(n * LANES, (n + 1) * LANES)
            a, mult, ix = _rg_gates(xc[:, sl], wa_ref[n], ba_ref[n], wx_ref[n], bx_ref[n], sp[:, sl])
            pieces.append(a * h[:, sl] + mult * ix)
        h = pieces[0] if nb == 1 else jnp.concatenate(pieces, axis=1)
        gate = gt_ref[t * bs:(t + 1) * bs, :]
        hg_ref[t * bs:(t + 1) * bs, :] = (h * jax.nn.gelu(gate)).astype(hg_ref.dtype)
    hl_ref[...] = h


def _rg_sample(p, cbuf_t, h0, conv_w, conv_b, wa, ba, wx, bx, lam, *, ts, bs, d, mp, cb_u, cb_g, cwid):
    nb = cwid // LANES
    nblk = d // cwid
    rows = ts * bs
    rblk = mp // rows
    kern = functools.partial(_rg_sample_kernel, ts=ts, bs=bs)
    n_tap = conv_w.shape[0]
    return pl.pallas_call(
        kern,
        grid=(nblk,),
        in_specs=[
            pl.BlockSpec((rows, cwid), lambda n: (rblk, cb_u * LANES // cwid + n)),
            pl.BlockSpec((rows, cwid), lambda n: (rblk, cb_g * LANES // cwid + n)),
            pl.BlockSpec((n_tap - 1, bs, cwid), lambda n: (0, 0, n)),
            pl.BlockSpec((bs, cwid), lambda n: (0, n)),
            pl.BlockSpec((n_tap, cwid), lambda n: (0, n)),
            pl.BlockSpec((1, cwid), lambda n: (0, n)),
            pl.BlockSpec((nb, LANES, LANES), lambda n: (n, 0, 0)),
            pl.BlockSpec((nb, 1, LANES), lambda n: (n, 0, 0)),
            pl.BlockSpec((nb, LANES, LANES), lambda n: (n, 0, 0)),
            pl.BlockSpec((nb, 1, LANES), lambda n: (n, 0, 0)),
            pl.BlockSpec((1, cwid), lambda n: (0, n)),
        ],
        out_specs=[
            pl.BlockSpec((rows, cwid), lambda n: (0, n)),
            pl.BlockSpec((bs, cwid), lambda n: (0, n)),
            pl.BlockSpec((n_tap - 1, bs, cwid), lambda n: (0, 0, n)),
        ],
        out_shape=[
            jax.ShapeDtypeStruct((rows, d), BF16),
            jax.ShapeDtypeStruct((bs, d), F32),
            jax.ShapeDtypeStruct((n_tap - 1, bs, d), F32),
        ],
        compiler_params=_cparams(1),
        name="rg_sample",
    )(p, p, cbuf_t, h0, conv_w, conv_b, wa, ba, wx, bx, lam)


def _shift_rows(x, prev_row, rowi):
    return jnp.where(rowi == 0, prev_row, pltpu.roll(x, 1, 0))


def _head_split(z, lane_head):
    return jnp.concatenate([jnp.where(lane_head == 0, z, 0.0), jnp.where(lane_head == 1, z, 0.0)], axis=0)


def _solve_unit_lower(n_mats, rhss, lane_head, xbd_ref, between_stages=None):
    n_sys = len(n_mats)
    c = n_mats[0].shape[0]
    br = _SOLVE_ROWS
    xbd_ref[...] = jnp.zeros(xbd_ref.shape, F32)
    done = [[] for _ in range(n_sys)]
    for blk in range(c // br):
        lo = blk * br
        rs = []
        for i in range(n_sys):
            r = rhss[i][lo:lo + br, :]
            if blk > 0:
                r = r - jnp.dot(n_mats[i][lo:lo + br, :], xbd_ref[i], preferred_element_type=F32)
            rs.append(r)
        if between_stages is not None:
            between_stages(blk)
        for j in range(br - 1):
            for i in range(n_sys):
                nrow = n_mats[i][lo:lo + br, :]
                mult = jnp.where(lane_head == 0, nrow[:, lo + j:lo + j + 1], nrow[:, HEAD + lo + j:HEAD + lo + j + 1])
                rs[i] = rs[i] - mult * rs[i][j:j + 1, :]
        for i in range(n_sys):
            done[i].append(rs[i])
            xbd_ref[i, lo:lo + br, :] = jnp.where(lane_head == 0, rs[i], 0.0)
            xbd_ref[i, c + lo:c + lo + br, :] = jnp.where(lane_head == 1, rs[i], 0.0)
    return [jnp.concatenate(b, axis=0) for b in done]


def _rwkv_prompt_kernel(pr_ref, pk_ref, pv_ref, pwd_ref, pad_ref, pgd_ref,
                        mur_ref, muk_ref, muv_ref, muwd_ref, muad_ref, mugd_ref,
                        w0_ref, w2_ref, a0_ref, a2_ref, g2_ref, kkw_ref, kaw_ref, rkw_ref, gnw_ref, gnb_ref,
                        ob_ref, s_ref,
                        r_s, lw_s, k_s, v_s, kap_s, bet_s, g_s, bon_s, y_s, tw_s, za_s, sg_s, lm_s, yb_s, xbd_s,
                        *, t_len, rc, group):
    ones_bd = _head_ones()
    n_ch = t_len // rc
    rowi = lax.broadcasted_iota(jnp.int32, (rc, 1), 0)

    def shifted(refs, mus, carry, r0):
        zs, lasts = [], []
        for ref, mu, prev in zip(refs, mus, carry):
            p = ref[pl.ds(r0, rc), :].astype(F32)
            zs.append(p + mu[...] * (_shift_rows(p, prev, rowi) - p))
            lasts.append(p[rc - 1:rc, :])
        return zs, tuple(lasts)

    @pl.when(pl.program_id(1) == 0)
    def _():
        lora_refs = (pwd_ref, pad_ref, pgd_ref)

        def phase0(c, carry):
            r0 = pl.multiple_of(c * rc, rc)
            (zwd, zad, zgd), lasts = shifted(lora_refs, (muwd_ref, muad_ref, mugd_ref), carry, r0)
            rows = pl.ds(r0, rc)
            tw_s[rows, :] = jnp.tanh(zwd).astype(BF16)
            za_s[rows, :] = zad.astype(BF16)
            sg_s[rows, :] = jax.nn.sigmoid(zgd).astype(BF16)
            return lasts

        lax.fori_loop(0, n_ch, phase0, tuple(jnp.zeros((1, ref.shape[1]), F32) for ref in lora_refs))

    def phase1(c, carry):
        r0 = pl.multiple_of(c * rc, rc)
        (zr, zk, zv), lasts = shifted((pr_ref, pk_ref, pv_ref), (mur_ref, muk_ref, muv_ref), carry, r0)
        rows = pl.ds(r0, rc)
        lw, g, kk, k2, beta, bonus = _rwkv_prep(
            zr, zk, zv, tw_s[rows, :], za_s[rows, :], sg_s[rows, :],
            w0_ref[...], w2_ref[...], a0_ref[...], a2_ref[...], g2_ref[...],
            kkw_ref[...], kaw_ref[...], rkw_ref[...], ones_bd)
        r_s[rows, :] = zr
        lw_s[rows, :] = lw
        k_s[rows, :] = k2
        v_s[rows, :] = zv
        kap_s[rows, :] = kk
        bet_s[rows, :] = beta
        g_s[rows, :] = g
        bon_s[rows, :] = bonus
        return lasts

    lax.fori_loop(0, n_ch, phase1, tuple(jnp.zeros((1, LANES), F32) for _ in range(3)))

    c_len = CHUNK
    slot = c_len + LANES
    ti = lax.broadcasted_iota(jnp.int32, (c_len, LANES), 0)
    si = lax.broadcasted_iota(jnp.int32, (c_len, LANES), 1) % HEAD
    strict = ti > si
    incl = ti >= si
    lri = lax.broadcasted_iota(jnp.int32, (c_len, c_len), 0)
    lci = lax.broadcasted_iota(jnp.int32, (c_len, c_len), 1)
    l_cum = (lri >= lci).astype(BF16)
    lane_head = lax.broadcasted_iota(jnp.int32, (1, LANES), 1) // HEAD
    lane_head2 = jnp.concatenate([lane_head, lane_head], axis=1)
    bri = lax.broadcasted_iota(jnp.int32, (LANES, LANES), 0)
    bci = lax.broadcasted_iota(jnp.int32, (LANES, LANES), 1)
    same_head = (bri // HEAD) == (bci // HEAD)
    same_head2 = jnp.concatenate([same_head, same_head], axis=1)
    eye = bri == bci
    zc = jnp.zeros((c_len, LANES), F32)
    nt_dims = (((1,), (1,)), ((), ()))

    def precompute(gi, between_stages):
        chunks = [gi * group + cc for cc in range(group)]
        rows = [pl.ds(pl.multiple_of(c * c_len, c_len), c_len) for c in chunks]
        lws = [lw_s[rw, :] for rw in rows]
        gcums = [_dot_exact_lhs(l_cum, lw) for lw in lws]
        g_ends = [g[c_len - 1:c_len, :] for g in gcums]
        kts = [kap_s[rw, :] * jnp.exp(g - lw) for rw, g, lw in zip(rows, gcums, lws)]
        rts = [r_s[rw, :] * jnp.exp(g) for rw, g in zip(rows, gcums)]
        e_negs = [jnp.exp(-g) for g in gcums]
        lhss = [jnp.concatenate([kt, rt], axis=0) for kt, rt in zip(kts, rts)]
        o_bs = [lax.dot_general(lhs, _head_split(bet_s[rw, :] * en, lane_head), nt_dims, preferred_element_type=F32)
                for lhs, rw, en in zip(lhss, rows, e_negs)]
        o_ks = [lax.dot_general(lhs, _head_split(k_s[rw, :] * en, lane_head), nt_dims, preferred_element_type=F32)
                for lhs, rw, en in zip(lhss, rows, e_negs)]
        n_mats = [jnp.where(strict, o[0:c_len], 0.0) for o in o_bs]
        v_bds = [_head_split(v_s[rw, :], lane_head) for rw in rows]
        avs = [jnp.dot(jnp.where(strict, o[0:c_len], 0.0), v_bd, preferred_element_type=F32)
               for o, v_bd in zip(o_ks, v_bds)]
        rhss = [jnp.concatenate([kt, av], axis=1) for kt, av in zip(kts, avs)]
        a_rs = [jnp.concatenate([jnp.where(incl, -ob[c_len:], 0.0), jnp.where(incl, ok[c_len:], 0.0)], axis=1)
                for ob, ok in zip(o_bs, o_ks)]
        e_hats = [jnp.exp(ge - g) for ge, g in zip(g_ends, gcums)]
        bk_ts = [jnp.concatenate([bet_s[rw, :] * eh, k_s[rw, :] * eh], axis=0).T for rw, eh in zip(rows, e_hats)]
        xs = _solve_unit_lower(n_mats, rhss, lane_head2, xbd_s, between_stages)
        wus = [-x for x in xs]
        tops = [jnp.dot(a_r, jnp.concatenate([xbd_s[i],
                                              jnp.concatenate([jnp.zeros((LANES, LANES), F32), v_bd], axis=1)],
                                             axis=0), preferred_element_type=F32)
                for i, (a_r, v_bd) in enumerate(zip(a_rs, v_bds))]
        bots = [jnp.dot(bk_t, jnp.concatenate([wu, jnp.concatenate([zc, v_s[rw, :]], axis=1)], axis=0),
                        preferred_element_type=F32)
                for bk_t, wu, rw in zip(bk_ts, wus, rows)]
        for c, top, bot, rt, ge in zip(chunks, tops, bots, rts, g_ends):
            bot = jnp.where(same_head2, bot, 0.0)
            base = pl.multiple_of((c + group) * slot, SUBLANES)
            lm_s[pl.ds(base, c_len), :] = top[:, 0:LANES] + rt
            lm_s[pl.ds(base + c_len, LANES), :] = bot[:, 0:LANES] + jnp.where(eye, jnp.exp(ge), 0.0)
            yb_s[pl.ds(base, c_len), :] = top[:, LANES:]
            yb_s[pl.ds(base + c_len, LANES), :] = bot[:, LANES:]

    def advance(c, h_bd):
        base = pl.multiple_of((c + group) * slot, SUBLANES)
        res = _dot_3pass(lm_s[pl.ds(base, slot), :], h_bd) + yb_s[pl.ds(base, slot), :]
        y_s[pl.ds(pl.multiple_of((c + group) * c_len, c_len), c_len), :] = res[0:c_len]
        return res[c_len:]

    def phase3(c, carry):
        r0 = pl.multiple_of(c * rc, rc)
        o = _rwkv_post(y_s[pl.ds(r0 + group * c_len, rc), :], bon_s[pl.ds(r0, rc), :], g_s[pl.ds(r0, rc), :],
                       gnw_ref[...], gnb_ref[...], ones_bd)
        ob_ref[pl.ds(r0, rc), :] = o.astype(ob_ref.dtype)
        return carry

    n_stages = c_len // _SOLVE_ROWS
    per_stage = -(-group // n_stages)
    n_groups = t_len // (c_len * group)
    lm_s[0:group * slot, :] = jnp.zeros((group * slot, LANES), F32)
    yb_s[0:group * slot, :] = jnp.zeros((group * slot, LANES), F32)

    def fused(gi, h):
        state = [h]

        def between_stages(blk):
            for cc in range(blk * per_stage, min((blk + 1) * per_stage, group)):
                state[0] = advance((gi - 1) * group + cc, state[0])

        precompute(gi, between_stages)
        return state[0]

    h_bd = lax.fori_loop(0, n_groups, fused, jnp.zeros((LANES, LANES), F32))
    n_ready = min(group, ((n_groups - 1) * group * c_len) // rc)
    for blk in range(group):
        h_bd = advance(jnp.int32((n_groups - 1) * group + blk), h_bd)
        if blk < n_ready:
            phase3(jnp.int32(blk), 0)
    s_bd = h_bd.T
    s_ref[0, 0] = s_bd[0:HEAD, 0:HEAD]
    s_ref[0, 1] = s_bd[HEAD:, HEAD:]
    lax.fori_loop(n_ready, n_ch, phase3, 0)


def _rwkv_specs(row_block, rows, q_rows, lay):
    pw, pa, pg, nb = lay["pw"], lay["pa"], lay["pg"], lay["nb"]

    def im(col_fn, on_rows=False):
        if row_block is None:
            return lambda b, hp: (b if on_rows else 0, col_fn(hp))
        return lambda hp: (row_block if on_rows else 0, col_fn(hp))

    def triple(n_rows, on_rows, c_r, c_k, c_v, c_wd, c_ad, c_gd):
        return [
            pl.BlockSpec((n_rows, LANES), im(lambda hp: c_r + hp, on_rows)),
            pl.BlockSpec((n_rows, LANES), im(lambda hp: c_k + hp, on_rows)),
            pl.BlockSpec((n_rows, LANES), im(lambda hp: c_v + hp, on_rows)),
            pl.BlockSpec((n_rows, pw * LANES), im(lambda hp: c_wd // pw, on_rows)),
            pl.BlockSpec((n_rows, pa * LANES), im(lambda hp: c_ad // pa, on_rows)),
            pl.BlockSpec((n_rows, pg * LANES), im(lambda hp: c_gd // pg, on_rows)),
        ]

    specs = triple(rows, True, lay["r"], lay["k"], lay["v"], lay["wd"], lay["ad"], lay["gd"])
    if q_rows:
        specs += triple(q_rows, False, 0, nb, 2 * nb, 0, pw, pw + pa)
    specs += triple(1, False, 0, nb, 2 * nb, 0, pw, pw + pa)
    per_pair = im(lambda hp: hp)
    specs += [
        pl.BlockSpec((1, LANES), per_pair),
        pl.BlockSpec((pw * LANES, LANES), per_pair),
        pl.BlockSpec((1, LANES), per_pair),
        pl.BlockSpec((pa * LANES, LANES), per_pair),
        pl.BlockSpec((pg * LANES, LANES), per_pair),
        pl.BlockSpec((1, LANES), per_pair),
        pl.BlockSpec((1, LANES), per_pair),
        pl.BlockSpec((1, LANES), per_pair),
        pl.BlockSpec((1, LANES), per_pair),
        pl.BlockSpec((1, LANES), per_pair),
    ]
    return specs


def _rwkv_prompt(p_main, p_tail, mu_rkv, mu_lora, params, *, bp, tp, d, lay):
    n_hp = d // LANES
    pw, pa, pg = lay["pw"], lay["pa"], lay["pg"]
    rc = 256 if tp % 256 == 0 else tp
    n_heads = d // HEAD
    n_chunks = tp // CHUNK
    group = next(g for g in (8, 4, 2, 1) if n_chunks % g == 0)
    kern = functools.partial(_rwkv_prompt_kernel, t_len=tp, rc=rc, group=group)
    vm = pltpu.VMEM((tp, LANES), F32)
    return pl.pallas_call(
        kern,
        grid=(bp, n_hp),
        in_specs=_rwkv_specs(None, tp, 0, lay),
        out_specs=[
            pl.BlockSpec((tp, LANES), lambda b, hp: (b, hp)),
            pl.BlockSpec((1, 2, HEAD, HEAD), lambda b, hp: (b, hp, 0, 0)),
        ],
        out_shape=[
            jax.ShapeDtypeStruct((bp * tp, d), BF16),
            jax.ShapeDtypeStruct((bp, n_heads, HEAD, HEAD), F32),
        ],
        scratch_shapes=[vm] * 8 + [
            pltpu.VMEM((tp + group * CHUNK, LANES), F32),
            pltpu.VMEM((tp, pw * LANES), BF16),
            pltpu.VMEM((tp, pa * LANES), BF16),
            pltpu.VMEM((tp, pg * LANES), BF16),
            pltpu.VMEM(((n_chunks + group) * (CHUNK + LANES), LANES), F32),
            pltpu.VMEM(((n_chunks + group) * (CHUNK + LANES), LANES), F32),
            pltpu.VMEM((group, 2 * CHUNK, 2 * LANES), F32),
        ],
        compiler_params=_cparams(2),
        name="rwkv_prompt",
    )(p_main, p_main, p_main, p_tail, p_tail, p_tail,
      mu_rkv, mu_rkv, mu_rkv, mu_lora, mu_lora, mu_lora, *params)


def _rwkv_sample_kernel(pr_ref, pk_ref, pv_ref, pwd_ref, pad_ref, pgd_ref,
                        qr_ref, qk_ref, qv_ref, qwd_ref, qad_ref, qgd_ref,
                        mur_ref, muk_ref, muv_ref, muwd_ref, muad_ref, mugd_ref,
                        w0_ref, w2_ref, a0_ref, a2_ref, g2_ref, kkw_ref, kaw_ref, rkw_ref, gnw_ref, gnb_ref,
                        st_ref, ob_ref, so_ref, y_s, tv_s, *, ts, bs):
    ones_bd = _head_ones()
    refs = (pr_ref, pk_ref, pv_ref, pwd_ref, pad_ref, pgd_ref)
    prevs = (qr_ref, qk_ref, qv_ref, qwd_ref, qad_ref, qgd_ref)
    mus = (mur_ref, muk_ref, muv_ref, muwd_ref, muad_ref, mugd_ref)
    zs = []
    for ref, q, mu in zip(refs, prevs, mus):
        p = ref[...].astype(F32)
        pp = jnp.concatenate([q[...].astype(F32), p[0:(ts - 1) * bs, :]], axis=0)
        zs.append(p + mu[...] * (pp - p))
    zr, zk, zv, zwd, zad, zgd = zs
    lw, g, kk, k2, beta, bonus = _rwkv_prep(
        zr, zk, zv, jnp.tanh(zwd).astype(BF16), zad.astype(BF16), jax.nn.sigmoid(zgd).astype(BF16),
        w0_ref[...], w2_ref[...], a0_ref[...], a2_ref[...], g2_ref[...],
        kkw_ref[...], kaw_ref[...], rkw_ref[...], ones_bd)
    w_dec = jnp.exp(lw)

    for t in range(ts):
        rows = slice(t * bs, (t + 1) * bs)
        for kind, a in enumerate((w_dec, kk, beta, k2, zr, zv)):
            tv_s[t, kind] = a[rows, :].T

    def total(p):
        p = p + pltpu.roll(p, 4, 0)
        p = p + pltpu.roll(p, 2, 0)
        return p + pltpu.roll(p, 1, 0)

    groups = HEAD // SUBLANES
    subi = lax.broadcasted_iota(jnp.int32, (SUBLANES, bs), 0)

    def row_group(gidx, carry):
        h = gidx // groups
        i0 = (gidx % groups) * SUBLANES
        lo = pl.multiple_of(h * HEAD, HEAD)
        row0 = pl.multiple_of(gidx * SUBLANES, SUBLANES)
        y_tiles = [jnp.zeros((SUBLANES, bs), F32) for _ in range(ts)]
        for half in range(0, SUBLANES, _SAMPLE_ROWS):
            s_rows = [st_ref[h, i0 + half + i].reshape(groups, SUBLANES, bs) for i in range(_SAMPLE_ROWS)]
            for t in range(ts):
                wt, kkt, bt, kt, rt = (tv_s[t, kind, pl.ds(lo, HEAD), :].reshape(groups, SUBLANES, bs)
                                       for kind in range(5))
                vt8 = tv_s[t, 5, pl.ds(row0, SUBLANES), :]
                for i in range(_SAMPLE_ROWS):
                    s = s_rows[i]
                    sa = -total(jnp.sum(s * kkt, axis=0))
                    s = s * wt + sa[None] * bt + vt8[half + i:half + i + 1, :][None] * kt
                    s_rows[i] = s
                    y_tiles[t] = jnp.where(subi == half + i, total(jnp.sum(s * rt, axis=0)), y_tiles[t])
            for i in range(_SAMPLE_ROWS):
                so_ref[h, i0 + half + i] = s_rows[i].reshape(HEAD, bs)
        for t in range(ts):
            y_s[t, pl.ds(row0, SUBLANES), :] = y_tiles[t]
        return carry

    lax.fori_loop(0, 2 * groups, row_group, 0)
    for t in range(ts):
        yt = y_s[t].T
        rows = slice(t * bs, (t + 1) * bs)
        o = _rwkv_post(yt, bonus[rows], g[rows], gnw_ref[...], gnb_ref[...], ones_bd)
        ob_ref[rows, :] = o.astype(ob_ref.dtype)


def _rwkv_sample(p_main, p_tail, q_rkv, q_lora, mu_rkv, mu_lora, params, st, *, ts, bs, d, mp, lay):
    n_hp = d // LANES
    rows = ts * bs
    rblk = mp // rows
    in_specs = _rwkv_specs(rblk, rows, bs, lay)
    in_specs.append(pl.BlockSpec((2, HEAD, HEAD, bs), lambda hp: (hp, 0, 0, 0)))
    kern = functools.partial(_rwkv_sample_kernel, ts=ts, bs=bs)
    return pl.pallas_call(
        kern,
        grid=(n_hp,),
        in_specs=in_specs,
        out_specs=[
            pl.BlockSpec((rows, LANES), lambda hp: (0, hp)),
            pl.BlockSpec((2, HEAD, HEAD, bs), lambda hp: (hp, 0, 0, 0)),
        ],
        out_shape=[
            jax.ShapeDtypeStruct((rows, d), BF16),
            jax.ShapeDtypeStruct(st.shape, F32),
        ],
        scratch_shapes=[pltpu.VMEM((ts, LANES, bs), F32), pltpu.VMEM((ts, 6, LANES, bs), F32)],
        compiler_params=_cparams(1),
        name="rwkv_sample",
    )(p_main, p_main, p_main, p_tail, p_tail, p_tail, q_rkv, q_rkv, q_rkv, q_lora, q_lora, q_lora,
      mu_rkv, mu_rkv, mu_rkv, mu_lora, mu_lora, mu_lora, *params, st)


def _pad_cols(a, width):
    return jnp.pad(a, ((0, 0), (0, width - a.shape[1])))


def _pad_rows(a, height):
    return jnp.pad(a, ((0, height - a.shape[0]), (0, 0)))


def _tiles(d, d_ff, n_tail):
    pick = lambda n, cands: next(c for c in cands if n % c == 0)
    return dict(
        tm=512, tm_wide=256,
        main=pick(5 * d, (1280, 1024, 512, 256, 128)),
        tail=pick(n_tail, (1536, 1024, 512, 256, 128)),
        prev=pick(d, (1024, 512, 256, 128)),
        up=pick(d_ff, (1024, 512, 256, 128)),
        down=pick(d, (512, 256, 128)),
    )


def kernel(x_prompt, x_sample, p_prompt, p_sample, state_rg_h, state_rg_conv, state_rwkv, state_shift,
           norm_mix, w_in, conv_w, conv_b, rg_wa, rg_ba, rg_wx, rg_bx, rg_lam, w_rg_o,
           mu_shift, rw_w0, rw_w2, rw_a0, rw_a2, rw_g2, rw_kk, rw_ka, rw_rk, rw_gn_w, rw_gn_b,
           w_rw_o, w_o, norm_ffn, w_up, w_down, norm_ple, w_ple_gate, w_ple, norm_f):
    bp, tp, d = x_prompt.shape
    bs, ts, _ = x_sample.shape
    depth = w_in.shape[0]
    n_heads, head = rw_rk.shape[1], rw_rk.shape[2]
    r_w, r_a, r_g = rw_w2.shape[1], rw_a2.shape[1], rw_g2.shape[1]
    d_ple = w_ple.shape[1]
    d_ff = w_up.shape[2]
    n_tap = conv_w.shape[1]
    assert head == HEAD and n_heads * HEAD == d and d % LANES == 0 and bs == LANES
    assert rg_wa.shape[2] == LANES and tp % CHUNK == 0 and n_tap == 4
    mp, ms = bp * tp, bs * ts
    nb = d // LANES
    pw, pa, pg = (_round_up(r, LANES) // LANES for r in (r_w, r_a, r_g))
    n_lora = (pw + pa + pg) * LANES
    n_tail = 2 * d + n_lora
    lay = dict(u=0, g=nb, r=2 * nb, k=3 * nb, v=4 * nb, wd=2 * nb, ad=2 * nb + pw, gd=2 * nb + pw + pa,
               pw=pw, pa=pa, pg=pg, nb=nb)
    t = _tiles(d, d_ff, n_tail)
    tm = t["tm"]
    assert mp % tm == 0 and ms % tm == 0 and mp % (ts * bs) == 0 and ms % t["tm_wide"] == 0
    assert lay["wd"] % pw == 0 and lay["ad"] % pa == 0 and lay["gd"] % pg == 0 and pw % pa == 0 and (pw + pa) % pg == 0
    assert (2 * d) % t["prev"] == 0 and (2 * d) % 512 == 0 and n_lora % 512 == 0 and d % 512 == 0
    mm = functools.partial(_mm, mp=mp, ms=ms)

    x = _Rows(x_prompt.reshape(mp, d), jnp.transpose(x_sample, (1, 0, 2)).reshape(ms, d))
    hp_l, cp_l, sp_l, xp_l, hs_l, cs_l, ss_l, xs_l = [], [], [], [], [], [], [], []
    y_p = y_s = None
    for i in range(depth):
        o_rw = 2 * d
        o_lora = o_rw + 3 * d
        o_g = o_lora + r_w + r_a + r_g
        wt = jnp.swapaxes(w_in[i], 0, 1)

        def lora_rows(a):
            return jnp.concatenate([
                _pad_rows(a[o_lora:o_lora + r_w], pw * LANES),
                _pad_rows(a[o_lora + r_w:o_lora + r_w + r_a], pa * LANES),
                _pad_rows(a[o_lora + r_w + r_a:o_g], pg * LANES)], axis=0)

        wt_tail = jnp.concatenate([wt[o_g:], lora_rows(wt)], axis=0)
        mu_all = jnp.concatenate([jnp.zeros((o_rw, 1), F32), mu_shift[i][:, None], jnp.zeros((2 * d, 1), F32)], axis=0)
        mu_rkv = mu_shift[i][None, :3 * d]
        mu_lora = lora_rows(mu_all).reshape(1, n_lora)
        rw_params = (
            rw_w0[i][None], _pad_rows(rw_w2[i], pw * LANES).astype(BF16),
            rw_a0[i][None], _pad_rows(rw_a2[i], pa * LANES).astype(BF16),
            _pad_rows(rw_g2[i], pg * LANES).astype(BF16),
            rw_kk[i][None], rw_ka[i][None], rw_rk[i].reshape(1, d), rw_gn_w[i][None], rw_gn_b[i][None])
        rg_params = (conv_w[i], conv_b[i][None], rg_wa[i].astype(BF16), rg_ba[i][:, None, :],
                     rg_wx[i].astype(BF16), rg_bx[i][:, None, :], rg_lam[i][None])

        xn = _Rows(_rmsnorm(x, norm_mix[i], BF16, tm, mp))
        xa = x.arrays if len(x.arrays) == 2 else (x.arrays[0][:mp], x.arrays[0][mp:])
        x_last = jnp.concatenate([xa[0].reshape(bp, tp, d)[:, -1], xa[1][(ts - 1) * bs:]], axis=0)
        n_last = _round_up(bp + bs, SUBLANES)
        xn_last = _rmsnorm(_Rows(_pad_rows(x_last, n_last)), norm_mix[i], F32, n_last, n_last)
        (p_main,) = mm([xn], [wt], [], _epi_plain, [(F32, False)], tm=tm, tn=t["main"], n_out=5 * d,
                       cast_w=True, w_transposed=True, name="in_proj")
        (p_tail,) = mm([xn], [wt_tail], [], _epi_plain, [(BF16, False)], tm=tm, tn=t["tail"], n_out=n_tail,
                       cast_w=True, w_transposed=True, name="in_proj_tail")
        xprev = _Rows(state_shift[i])
        (q_rkv,) = _mm([xprev], [wt], [], _epi_plain, [(F32, False)], mp=bs, ms=0, tm=bs, tn=t["prev"],
                       n_out=3 * d, w_col_off=o_rw // t["prev"], cast_w=True, w_transposed=True, name="prev_proj")
        (q_lora,) = _mm([xprev], [wt_tail], [], _epi_plain, [(BF16, False)], mp=bs, ms=0, tm=bs, tn=512,
                        n_out=n_lora, w_col_off=2 * d // 512, cast_w=True, w_transposed=True,
                        name="prev_proj_tail")

        cwid = 2 * LANES if nb % 2 == 0 else LANES
        hg_p, h_p, c_p = _rg_prompt(p_main, *rg_params, bp=bp, tp=tp, d=d, cb_u=lay["u"], cb_g=lay["g"], cwid=cwid)
        cbuf_t = jnp.transpose(state_rg_conv[i], (1, 0, 2))
        hg_s, h_s, c_s = _rg_sample(p_main, cbuf_t, state_rg_h[i], *rg_params, ts=ts, bs=bs, d=d, mp=mp,
                                    cb_u=lay["u"], cb_g=lay["g"], cwid=cwid)

        ob_p, s_p = _rwkv_prompt(p_main, p_tail, mu_rkv, mu_lora, rw_params, bp=bp, tp=tp, d=d, lay=lay)
        st = jnp.transpose(state_rwkv[i], (1, 2, 3, 0))
        ob_s, st_new = _rwkv_sample(p_main, p_tail, q_rkv, q_lora, mu_rkv, mu_lora, rw_params, st,
                                    ts=ts, bs=bs, d=d, mp=mp, lay=lay)
        s_s = jnp.transpose(st_new, (3, 0, 1, 2))

        tw = t["tm_wide"]
        (merged,) = mm([_Rows(hg_p, hg_s), _Rows(ob_p, ob_s)], [w_rg_o[i].astype(BF16), w_rw_o[i].astype(BF16)],
                       [("tile", _Rows(p_tail), 0), ("tile", _Rows(p_tail), 1)],
                       _epi_merge, [(BF16, False)], tm=tw, tn=d, n_out=d, single_buffer_w=True, name="merge")
        x1, xn2 = mm([_Rows(merged)], [w_o[i]], [("tile", x, 0), ("row", norm_ffn[i][None], 0)],
                     _epi_residual_norm, [(F32, False), (BF16, False)], tm=tw, tn=d, n_out=d, cast_w=True,
                     single_buffer_w=True, name="out_proj")
        (hf,) = mm([_Rows(xn2)], [w_up[i]], [], _epi_relu2, [(BF16, False)], tm=tm, tn=t["up"], n_out=d_ff,
                   cast_w=True, name="mlp_up")
        (x2,) = mm([_Rows(hf)], [w_down[i].astype(BF16)], [("tile", _Rows(x1), 0)], _epi_residual, [(F32, False)],
                   tm=tm, tn=t["down"], n_out=d, name="mlp_down")
        p_rows = _Rows(p_prompt[i].reshape(mp, d_ple), jnp.transpose(p_sample[i], (1, 0, 2)).reshape(ms, d_ple))
        ple_ws = [w_ple_gate[i], w_ple[i]]
        g_ple = ("row", norm_ple[i][None], 0)
        if i == depth - 1:
            y_p, y_s = mm([("norm_of", 0, 2), p_rows], ple_ws,
                          [("tile", _Rows(x2), 0), ("row", norm_f[None], 0), g_ple],
                          _epi_ple_final, [(F32, True)], tm=tw, tn=d, n_out=d, cast_w=True, single_buffer_w=True,
                          name="ple_final")
        else:
            x = _Rows(*mm([("norm_of", 0, 1), p_rows], ple_ws, [("tile", _Rows(x2), 0), g_ple], _epi_ple,
                          [(F32, False)], tm=tw, tn=d, n_out=d, cast_w=True, single_buffer_w=True, name="ple"))

        hp_l.append(h_p[:, 0])
        cp_l.append(c_p[:, SUBLANES - (n_tap - 1):])
        sp_l.append(s_p)
        xp_l.append(xn_last[:bp])
        hs_l.append(h_s)
        cs_l.append(jnp.transpose(c_s, (1, 0, 2)))
        ss_l.append(s_s)
        xs_l.append(xn_last[bp:bp + bs])

    y_prompt = y_p.reshape(bp, tp, d)
    y_sample = jnp.transpose(y_s.reshape(ts, bs, d), (1, 0, 2))
    return (y_prompt, y_sample,
            jnp.stack(hp_l), jnp.stack(cp_l), jnp.stack(sp_l), jnp.stack(xp_l),
            jnp.stack(hs_l), jnp.stack(cs_l), jnp.stack(ss_l), jnp.stack(xs_l))
```

```python
import functools

import jax
import jax.numpy as jnp
from jax import lax
from jax.experimental import pallas as pl
from jax.experimental.pallas import tpu as pltpu

F32 = jnp.float32
BF16 = jnp.bfloat16
LANES = 128
SUBLANES = 8
HEAD = 64
CHUNK = 64
_SAMPLE_ROWS = 4
EPS = 1e-6
GN_EPS = 64e-5
RG_C = 8.0
VMEM_LIMIT = 56 * 1024 * 1024


def _cparams(n_grid, vmem=VMEM_LIMIT):
    return pltpu.CompilerParams(dimension_semantics=("arbitrary",) * n_grid, vmem_limit_bytes=vmem)


def _round_up(x, m):
    return (x + m - 1) // m * m


def _rms(x, g):
    ms = jnp.mean(x * x, axis=-1, keepdims=True)
    return (x * lax.rsqrt(ms + EPS)) * g


class _Rows:
    def __init__(self, *arrays):
        self.arrays = arrays

    @property
    def width(self):
        return self.arrays[0].shape[1]

    def specs(self, tm, n_p, col):
        if len(self.arrays) == 1:
            return [lambda wid: pl.BlockSpec((tm, wid), lambda n, i: (i, col(n)))]
        return [lambda wid: pl.BlockSpec((tm, wid), lambda n, i: (jnp.minimum(i, n_p - 1), col(n))),
                lambda wid: pl.BlockSpec((tm, wid), lambda n, i: (jnp.maximum(i - n_p, 0), col(n)))]


def _load_rows(refs, is_prompt):
    if len(refs) == 1:
        return refs[0][...]
    return jnp.where(is_prompt, refs[0][...], refs[1][...])


def _rmsnorm_kernel(*refs, n_src, n_p):
    g_ref, o_ref = refs[n_src], refs[n_src + 1]
    x = _load_rows(refs[:n_src], pl.program_id(1) < n_p)
    o_ref[...] = _rms(x, g_ref[...]).astype(o_ref.dtype)


def _rmsnorm(x, g, out_dtype, tm, mp):
    d = x.width
    m = sum(a.shape[0] for a in x.arrays)
    n_p = mp // tm
    in_specs = [mk(d) for mk in x.specs(tm, n_p, lambda n: 0)] + [pl.BlockSpec((1, d), lambda n, i: (0, 0))]
    return pl.pallas_call(
        functools.partial(_rmsnorm_kernel, n_src=len(x.arrays), n_p=n_p),
        grid=(1, m // tm),
        in_specs=in_specs,
        out_specs=pl.BlockSpec((tm, d), lambda n, i: (i, 0)),
        out_shape=jax.ShapeDtypeStruct((m, d), out_dtype),
        compiler_params=_cparams(2),
        name="rmsnorm",
    )(*x.arrays, g.reshape(1, d))


def _mm_kernel(*refs, x_counts, x_norm, n_w, cast_w, w_transposed, extra_counts, out_counts, n_p, epilogue):
    pos = 0
    x_refs = []
    for cnt in x_counts:
        x_refs.append(refs[pos:pos + cnt])
        pos += cnt
    w_refs = refs[pos:pos + n_w]
    pos += n_w
    e_refs = []
    for cnt in extra_counts:
        e_refs.append(refs[pos:pos + cnt])
        pos += cnt
    o_refs = []
    for cnt in out_counts:
        o_refs.append(refs[pos:pos + cnt])
        pos += cnt
    w_scratch = refs[pos:]
    i = pl.program_id(1)
    is_prompt = i < n_p
    if cast_w:
        @pl.when(i == 0)
        def _():
            for w_ref, s_ref in zip(w_refs, w_scratch):
                s_ref[...] = w_ref[...].astype(BF16)
        ws = [s[...] for s in w_scratch]
    else:
        ws = [w[...] for w in w_refs]
    dims = (((1,), (1 if w_transposed else 0,)), ((), ()))
    e_vals = [_load_rows(er, is_prompt) for er in e_refs]
    x_vals = []
    for idx, xr in enumerate(x_refs):
        if xr:
            x_vals.append(_load_rows(xr, is_prompt).astype(BF16))
        else:
            e_idx, g_idx = x_norm[idx]
            x_vals.append(_rms(e_vals[e_idx], e_vals[g_idx]).astype(BF16))
    accs = [lax.dot_general(x, w, dims, preferred_element_type=F32) for x, w in zip(x_vals, ws)]
    outs = epilogue(accs, e_vals)
    for refs_o, tile in zip(o_refs, outs):
        if len(refs_o) == 1:
            refs_o[0][...] = tile.astype(refs_o[0].dtype)
        else:
            @pl.when(is_prompt)
            def _():
                refs_o[0][...] = tile.astype(refs_o[0].dtype)

            @pl.when(jnp.logical_not(is_prompt))
            def _():
                refs_o[1][...] = tile.astype(refs_o[1].dtype)


def _mm(xs, ws, extras, epilogue, outs, *, mp, ms, tm, tn, n_out, w_col_off=0, w_row_start=None, cast_w=False,
        w_transposed=False, single_buffer_w=False, name="mm"):
    n_p = mp // tm
    m = mp + ms
    grid = (n_out // tn, m // tm)
    in_specs, args, x_counts, extra_counts, x_norm = [], [], [], [], {}
    for idx, x in enumerate(xs):
        if isinstance(x, tuple):
            x_norm[idx] = (x[1], x[2])
            x_counts.append(0)
            continue
        in_specs += [mk(x.width) for mk in x.specs(tm, n_p, lambda n: 0)]
        args += list(x.arrays)
        x_counts.append(len(x.arrays))
    w_mode = dict(pipeline_mode=pl.Buffered(1)) if single_buffer_w else {}
    if w_transposed and w_row_start is not None:
        w_blocks = [(tn, w.shape[1]) for w in ws]
        in_specs += [pl.BlockSpec((pl.Element(tn), pl.Element(w.shape[1])),
                                  lambda n, i: (pl.multiple_of(w_row_start + n * tn, SUBLANES), 0), **w_mode)
                     for w in ws]
    elif w_transposed:
        w_blocks = [(tn, w.shape[1]) for w in ws]
        in_specs += [pl.BlockSpec(blk, lambda n, i: (n + w_col_off, 0), **w_mode) for blk in w_blocks]
    else:
        w_blocks = [(w.shape[0], tn) for w in ws]
        in_specs += [pl.BlockSpec(blk, lambda n, i: (0, n + w_col_off), **w_mode) for blk in w_blocks]
    args += list(ws)
    for kind, src, off in extras:
        if kind == "tile":
            in_specs += [mk(tn) for mk in src.specs(tm, n_p, lambda n, off=off: n + off)]
            args += list(src.arrays)
            extra_counts.append(len(src.arrays))
        elif kind == "row":
            in_specs.append(pl.BlockSpec((1, tn), lambda n, i, off=off: (0, n + off)))
            args.append(src)
            extra_counts.append(1)
        else:
            raise ValueError(kind)
    out_specs, out_shape, out_counts = [], [], []
    for dtype, split in outs:
        if split:
            out_specs += [pl.BlockSpec((tm, tn), lambda n, i: (jnp.minimum(i, n_p - 1), n)),
                          pl.BlockSpec((tm, tn), lambda n, i: (jnp.maximum(i - n_p, 0), n))]
            out_shape += [jax.ShapeDtypeStruct((mp, n_out), dtype), jax.ShapeDtypeStruct((ms, n_out), dtype)]
            out_counts.append(2)
        else:
            out_specs.append(pl.BlockSpec((tm, tn), lambda n, i: (i, n)))
            out_shape.append(jax.ShapeDtypeStruct((m, n_out), dtype))
            out_counts.append(1)
    kern = functools.partial(_mm_kernel, x_counts=tuple(x_counts), x_norm=x_norm, n_w=len(ws), cast_w=cast_w,
                             w_transposed=w_transposed, extra_counts=tuple(extra_counts),
                             out_counts=tuple(out_counts), n_p=n_p, epilogue=epilogue)
    scratch = [pltpu.VMEM(blk, BF16) for blk in w_blocks] if cast_w else []
    res = pl.pallas_call(
        kern,
        grid=grid,
        in_specs=in_specs,
        out_specs=out_specs,
        out_shape=out_shape,
        scratch_shapes=scratch,
        compiler_params=_cparams(2),
        name=name,
    )(*args)
    return res


def _epi_plain(accs, extras):
    return (accs[0],)


def _epi_residual(accs, extras):
    return (extras[0] + accs[0],)


def _epi_merge(accs, extras):
    ga, gb = extras
    return (jax.nn.sigmoid(ga.astype(F32)) * accs[0] + jax.nn.sigmoid(gb.astype(F32)) * accs[1],)


def _epi_residual_norm(accs, extras):
    x = extras[0] + accs[0]
    return x, _rms(x, extras[1])


def _epi_relu2(accs, extras):
    h = jnp.maximum(accs[0], 0.0)
    return (h * h,)


def _epi_ple(accs, extras):
    return (extras[0] + jax.nn.sigmoid(accs[0]) * accs[1],)


def _epi_ple_final(accs, extras):
    return (_rms(extras[0] + jax.nn.sigmoid(accs[0]) * accs[1], extras[1]),)


def _softplus(x):
    return jnp.maximum(x, 0.0) + jnp.log1p(jnp.exp(-jnp.abs(x)))


def _split_bf16(x, parts):
    out = []
    rem = x
    for _ in range(parts):
        p = rem.astype(BF16)
        out.append(p)
        rem = rem - p.astype(F32)
    return out


def _dot_exact_lhs(a_bf16, x, parts=3):
    acc = None
    for p in _split_bf16(x, parts):
        t = jnp.dot(a_bf16, p, preferred_element_type=F32)
        acc = t if acc is None else acc + t
    return acc


def _head_sum(x, ones_bd):
    return jnp.dot(x.astype(BF16), ones_bd, preferred_element_type=F32)


def _sigmoid(x):
    return 0.5 * jnp.tanh(0.5 * x) + 0.5


def _dot_3pass(a, b):
    a_hi, a_lo = _split_bf16(a, 2)
    b_hi, b_lo = _split_bf16(b, 2)
    return (jnp.dot(a_hi, b_hi, preferred_element_type=F32)
            + jnp.dot(a_hi, b_lo, preferred_element_type=F32)
            + jnp.dot(a_lo, b_hi, preferred_element_type=F32))


def _head_ones():
    r = lax.broadcasted_iota(jnp.int32, (LANES, LANES), 0) // HEAD
    c = lax.broadcasted_iota(jnp.int32, (LANES, LANES), 1) // HEAD
    return (r == c).astype(BF16)


def _rwkv_prep(zr, zk, zv, tw, za, sg, w0, w2, a0, a2, g2, kkw, kaw, rkw, ones_bd):
    wlog = -_softplus(-(w0 + jnp.dot(tw, w2, preferred_element_type=F32))) - 0.5
    lw = -jnp.exp(wlog)
    a = _sigmoid(a0 + jnp.dot(za, a2, preferred_element_type=F32))
    g = jnp.dot(sg, g2, preferred_element_type=F32)
    kk = zk * kkw
    ss = _head_sum(kk * kk, ones_bd)
    kk = kk * lax.rsqrt(jnp.maximum(ss, 1e-24))
    k2 = zk * (1.0 + (a - 1.0) * kaw)
    beta = kk * a
    bonus = _head_sum(zr * k2 * rkw, ones_bd) * zv
    return lw, g, kk, k2, beta, bonus


def _rwkv_post(y, bonus, g, gnw, gnb, ones_bd):
    mu = _head_sum(y, ones_bd) * (1.0 / HEAD)
    d = y - mu
    var = _head_sum(d * d, ones_bd) * (1.0 / HEAD)
    yn = (d * lax.rsqrt(var + GN_EPS)) * gnw + gnb
    return (yn + bonus) * g


def _rg_gates(xc, wa, ba, wx, bx, sp):
    xcb = xc.astype(BF16)
    r = _sigmoid(jnp.dot(xcb, wa, preferred_element_type=F32) + ba)
    i = _sigmoid(jnp.dot(xcb, wx, preferred_element_type=F32) + bx)
    log_a = (-RG_C * r) * sp
    a = jnp.exp(log_a)
    mult = jnp.sqrt(-jnp.tanh(log_a) * (a * a + 1.0))
    return a, mult, i * xc


def _rg_prompt_kernel(u_ref, gt_ref, cw_ref, cb_ref, wa_ref, ba_ref, wx_ref, bx_ref, lam_ref,
                      hg_ref, hl_ref, cs_ref, us_ref, a_ref, b_ref, *, t_len, rc):
    cwid = u_ref.shape[1]
    nb = cwid // LANES
    us_ref[0:SUBLANES, :] = jnp.zeros((SUBLANES, cwid), F32)
    us_ref[SUBLANES:, :] = u_ref[...]
    sp = _softplus(-lam_ref[...])
    w0 = cw_ref[0:1, :]
    w1 = cw_ref[1:2, :]
    w2 = cw_ref[2:3, :]
    w3 = cw_ref[3:4, :]
    cb = cb_ref[...]
    n_ch = t_len // rc

    def phase1(c, carry):
        r0 = pl.multiple_of(c * rc, rc)
        e = us_ref[pl.ds(r0, rc + SUBLANES), :]
        u0 = e[SUBLANES:]
        u1 = pltpu.roll(e, 1, 0)[SUBLANES:]
        u2 = pltpu.roll(e, 2, 0)[SUBLANES:]
        u3 = pltpu.roll(e, 3, 0)[SUBLANES:]
        xc = cb + (u3 * w0 + u2 * w1 + u1 * w2 + u0 * w3)
        row = lax.broadcasted_iota(jnp.int32, (rc, LANES), 0) + r0
        for n in range(nb):
            sl = slice(n * LANES, (n + 1) * LANES)
            a, mult, ix = _rg_gates(xc[:, sl], wa_ref[n], ba_ref[n], wx_ref[n], bx_ref[n], sp[:, sl])
            mult = jnp.where(row == 0, 1.0, mult)
            a_ref[n, pl.ds(r0, rc), :] = a
            b_ref[n, pl.ds(r0, rc), :] = mult * ix
        return carry

    lax.fori_loop(0, n_ch, phase1, 0)

    rowi = lax.broadcasted_iota(jnp.int32, (SUBLANES, LANES), 0)

    def scan(i, h_prev):
        r0 = pl.multiple_of(i * SUBLANES, SUBLANES)
        out = []
        for n in range(nb):
            a = a_ref[n, pl.ds(r0, SUBLANES), :]
            b = b_ref[n, pl.ds(r0, SUBLANES), :]
            for d in (1, 2, 4):
                a_sh = jnp.where(rowi < d, 1.0, pltpu.roll(a, d, 0))
                b_sh = jnp.where(rowi < d, 0.0, pltpu.roll(b, d, 0))
                b = a * b_sh + b
                a = a * a_sh
            h = b + a * h_prev[n]
            b_ref[n, pl.ds(r0, SUBLANES), :] = h
            out.append(h[SUBLANES - 1:SUBLANES, :])
        return tuple(out)

    h_last = lax.fori_loop(0, t_len // SUBLANES, scan, tuple(jnp.zeros((1, LANES), F32) for _ in range(nb)),
                           unroll=4)
    hl_ref[0] = jnp.concatenate(list(h_last), axis=1)
    cs_ref[0] = u_ref[pl.ds(t_len - SUBLANES, SUBLANES), :]

    def phase3(c, carry):
        r0 = pl.multiple_of(c * rc, rc)
        for n in range(nb):
            sl = slice(n * LANES, (n + 1) * LANES)
            gate = gt_ref[pl.ds(r0, rc), sl]
            hg_ref[pl.ds(r0, rc), sl] = (b_ref[n, pl.ds(r0, rc), :] * jax.nn.gelu(gate)).astype(hg_ref.dtype)
        return carry

    lax.fori_loop(0, n_ch, phase3, 0)


def _rg_prompt(p, conv_w, conv_b, wa, ba, wx, bx, lam, *, bp, tp, d, cb_u, cb_g, cwid):
    nb = cwid // LANES
    nblk = d // cwid
    rc = 256 if tp % 256 == 0 else tp
    kern = functools.partial(_rg_prompt_kernel, t_len=tp, rc=rc)
    return pl.pallas_call(
        kern,
        grid=(bp, nblk),
        in_specs=[
            pl.BlockSpec((tp, cwid), lambda b, n: (b, cb_u * LANES // cwid + n)),
            pl.BlockSpec((tp, cwid), lambda b, n: (b, cb_g * LANES // cwid + n)),
            pl.BlockSpec((4, cwid), lambda b, n: (0, n)),
            pl.BlockSpec((1, cwid), lambda b, n: (0, n)),
            pl.BlockSpec((nb, LANES, LANES), lambda b, n: (n, 0, 0)),
            pl.BlockSpec((nb, 1, LANES), lambda b, n: (n, 0, 0)),
            pl.BlockSpec((nb, LANES, LANES), lambda b, n: (n, 0, 0)),
            pl.BlockSpec((nb, 1, LANES), lambda b, n: (n, 0, 0)),
            pl.BlockSpec((1, cwid), lambda b, n: (0, n)),
        ],
        out_specs=[
            pl.BlockSpec((tp, cwid), lambda b, n: (b, n)),
            pl.BlockSpec((1, 1, cwid), lambda b, n: (b, 0, n)),
            pl.BlockSpec((1, SUBLANES, cwid), lambda b, n: (b, 0, n)),
        ],
        out_shape=[
            jax.ShapeDtypeStruct((bp * tp, d), BF16),
            jax.ShapeDtypeStruct((bp, 1, d), F32),
            jax.ShapeDtypeStruct((bp, SUBLANES, d), F32),
        ],
        scratch_shapes=[
            pltpu.VMEM((tp + SUBLANES, cwid), F32),
            pltpu.VMEM((nb, tp, LANES), F32),
            pltpu.VMEM((nb, tp, LANES), F32),
        ],
        compiler_params=_cparams(2),
        name="rg_prompt",
    )(p, p, conv_w, conv_b, wa, ba, wx, bx, lam)


def _rg_sample_kernel(u_ref, gt_ref, cbuf_ref, h0_ref, cw_ref, cb_ref, wa_ref, ba_ref, wx_ref, bx_ref, lam_ref,
                      hg_ref, hl_ref, cs_ref, *, ts, bs):
    cwid = u_ref.shape[1]
    nb = cwid // LANES
    n_tap = cw_ref.shape[0]
    sp = _softplus(-lam_ref[...])
    cb = cb_ref[...]
    ext = [cbuf_ref[j] for j in range(n_tap - 1)] + [u_ref[t * bs:(t + 1) * bs, :] for t in range(ts)]
    for j in range(n_tap - 1):
        cs_ref[j] = ext[len(ext) - (n_tap - 1) + j]
    h = h0_ref[...]
    for t in range(ts):
        conv = ext[t] * cw_ref[0:1, :]
        for j in range(1, n_tap):
            conv = conv + ext[t + j] * cw_ref[j:j + 1, :]
        xc = cb + conv
        pieces = []
        for n in range(nb):
            sl = slice(n * LANES, (n + 1) * LANES)
            a, mult, ix = _rg_gates(xc[:, sl], wa_ref[n], ba_ref[n], wx_ref[n], bx_ref[n], sp[:, sl])
            pieces.append(a * h[:, sl] + mult * ix)
        h = pieces[0] if nb == 1 else jnp.concatenate(pieces, axis=1)
        gate = gt_ref[t * bs:(t + 1) * bs, :]
        hg_ref[t * bs:(t + 1) * bs, :] = (h * jax.nn.gelu(gate)).astype(hg_ref.dtype)
    hl_ref[...] = h


def _rg_sample(p, cbuf_t, h0, conv_w, conv_b, wa, ba, wx, bx, lam, *, ts, bs, d, mp, cb_u, cb_g, cwid):
    nb = cwid // LANES
    nblk = d // cwid
    rows = ts * bs
    rblk = mp // rows
    kern = functools.partial(_rg_sample_kernel, ts=ts, bs=bs)
    n_tap = conv_w.shape[0]
    return pl.pallas_call(
        kern,
        grid=(nblk,),
        in_specs=[
            pl.BlockSpec((rows, cwid), lambda n: (rblk, cb_u * LANES // cwid + n)),
            pl.BlockSpec((rows, cwid), lambda n: (rblk, cb_g * LANES // cwid + n)),
            pl.BlockSpec((n_tap - 1, bs, cwid), lambda n: (0, 0, n)),
            pl.BlockSpec((bs, cwid), lambda n: (0, n)),
            pl.BlockSpec((n_tap, cwid), lambda n: (0, n)),
            pl.BlockSpec((1, cwid), lambda n: (0, n)),
            pl.BlockSpec((nb, LANES, LANES), lambda n: (n, 0, 0)),
            pl.BlockSpec((nb, 1, LANES), lambda n: (n, 0, 0)),
            pl.BlockSpec((nb, LANES, LANES), lambda n: (n, 0, 0)),
            pl.BlockSpec((nb, 1, LANES), lambda n: (n, 0, 0)),
            pl.BlockSpec((1, cwid), lambda n: (0, n)),
        ],
        out_specs=[
            pl.BlockSpec((rows, cwid), lambda n: (0, n)),
            pl.BlockSpec((bs, cwid), lambda n: (0, n)),
            pl.BlockSpec((n_tap - 1, bs, cwid), lambda n: (0, 0, n)),
        ],
        out_shape=[
            jax.ShapeDtypeStruct((rows, d), BF16),
            jax.ShapeDtypeStruct((bs, d), F32),
            jax.ShapeDtypeStruct((n_tap - 1, bs, d), F32),
        ],
        compiler_params=_cparams(1),
        name="rg_sample",
    )(p, p, cbuf_t, h0, conv_w, conv_b, wa, ba, wx, bx, lam)


def _shift_rows(x, prev_row, rowi):
    return jnp.where(rowi == 0, prev_row, pltpu.roll(x, 1, 0))


def _head_split(z, lane_head):
    return jnp.concatenate([jnp.where(lane_head == 0, z, 0.0), jnp.where(lane_head == 1, z, 0.0)], axis=0)


def _solve_unit_lower(n_mats, rhss, lane_head, xbd_ref, between_stages=None):
    n_sys = len(n_mats)
    c = n_mats[0].shape[0]
    br = SUBLANES
    xbd_ref[...] = jnp.zeros(xbd_ref.shape, F32)
    done = [[] for _ in range(n_sys)]
    for blk in range(c // br):
        lo = blk * br
        rs = []
        for i in range(n_sys):
            r = rhss[i][lo:lo + br, :]
            if blk > 0:
                r = r - jnp.dot(n_mats[i][lo:lo + br, :], xbd_ref[i], preferred_element_type=F32)
            rs.append(r)
        if between_stages is not None:
            between_stages(blk)
        for j in range(br - 1):
            for i in range(n_sys):
                nrow = n_mats[i][lo:lo + br, :]
                mult = jnp.where(lane_head == 0, nrow[:, lo + j:lo + j + 1], nrow[:, HEAD + lo + j:HEAD + lo + j + 1])
                rs[i] = rs[i] - mult * rs[i][j:j + 1, :]
        for i in range(n_sys):
            done[i].append(rs[i])
            xbd_ref[i, lo:lo + br, :] = jnp.where(lane_head == 0, rs[i], 0.0)
            xbd_ref[i, c + lo:c + lo + br, :] = jnp.where(lane_head == 1, rs[i], 0.0)
    return [jnp.concatenate(b, axis=0) for b in done]


def _rwkv_prompt_kernel(pr_ref, pk_ref, pv_ref, pwd_ref, pad_ref, pgd_ref,
                        mur_ref, muk_ref, muv_ref, muwd_ref, muad_ref, mugd_ref,
                        w0_ref, w2_ref, a0_ref, a2_ref, g2_ref, kkw_ref, kaw_ref, rkw_ref, gnw_ref, gnb_ref,
                        ob_ref, s_ref,
                        r_s, lw_s, k_s, v_s, kap_s, bet_s, g_s, bon_s, y_s, tw_s, za_s, sg_s, lm_s, yb_s, xbd_s,
                        *, t_len, rc, group):
    ones_bd = _head_ones()
    n_ch = t_len // rc
    rowi = lax.broadcasted_iota(jnp.int32, (rc, 1), 0)

    def shifted(refs, mus, carry, r0):
        zs, lasts = [], []
        for ref, mu, prev in zip(refs, mus, carry):
            p = ref[pl.ds(r0, rc), :].astype(F32)
            zs.append(p + mu[...] * (_shift_rows(p, prev, rowi) - p))
            lasts.append(p[rc - 1:rc, :])
        return zs, tuple(lasts)

    @pl.when(pl.program_id(1) == 0)
    def _():
        lora_refs = (pwd_ref, pad_ref, pgd_ref)

        def phase0(c, carry):
            r0 = pl.multiple_of(c * rc, rc)
            (zwd, zad, zgd), lasts = shifted(lora_refs, (muwd_ref, muad_ref, mugd_ref), carry, r0)
            rows = pl.ds(r0, rc)
            tw_s[rows, :] = jnp.tanh(zwd).astype(BF16)
            za_s[rows, :] = zad.astype(BF16)
            sg_s[rows, :] = jax.nn.sigmoid(zgd).astype(BF16)
            return lasts

        lax.fori_loop(0, n_ch, phase0, tuple(jnp.zeros((1, ref.shape[1]), F32) for ref in lora_refs))

    def phase1(c, carry):
        r0 = pl.multiple_of(c * rc, rc)
        (zr, zk, zv), lasts = shifted((pr_ref, pk_ref, pv_ref), (mur_ref, muk_ref, muv_ref), carry, r0)
        rows = pl.ds(r0, rc)
        lw, g, kk, k2, beta, bonus = _rwkv_prep(
            zr, zk, zv, tw_s[rows, :], za_s[rows, :], sg_s[rows, :],
            w0_ref[...], w2_ref[...], a0_ref[...], a2_ref[...], g2_ref[...],
            kkw_ref[...], kaw_ref[...], rkw_ref[...], ones_bd)
        r_s[rows, :] = zr
        lw_s[rows, :] = lw
        k_s[rows, :] = k2
        v_s[rows, :] = zv
        kap_s[rows, :] = kk
        bet_s[rows, :] = beta
        g_s[rows, :] = g
        bon_s[rows, :] = bonus
        return lasts

    lax.fori_loop(0, n_ch, phase1, tuple(jnp.zeros((1, LANES), F32) for _ in range(3)))

    c_len = CHUNK
    slot = c_len + LANES
    ti = lax.broadcasted_iota(jnp.int32, (c_len, LANES), 0)
    si = lax.broadcasted_iota(jnp.int32, (c_len, LANES), 1) % HEAD
    strict = ti > si
    incl = ti >= si
    lri = lax.broadcasted_iota(jnp.int32, (c_len, c_len), 0)
    lci = lax.broadcasted_iota(jnp.int32, (c_len, c_len), 1)
    l_cum = (lri >= lci).astype(BF16)
    lane_head = lax.broadcasted_iota(jnp.int32, (1, LANES), 1) // HEAD
    lane_head2 = jnp.concatenate([lane_head, lane_head], axis=1)
    bri = lax.broadcasted_iota(jnp.int32, (LANES, LANES), 0)
    bci = lax.broadcasted_iota(jnp.int32, (LANES, LANES), 1)
    same_head = (bri // HEAD) == (bci // HEAD)
    same_head2 = jnp.concatenate([same_head, same_head], axis=1)
    eye = bri == bci
    zc = jnp.zeros((c_len, LANES), F32)
    nt_dims = (((1,), (1,)), ((), ()))

    def precompute(gi, between_stages):
        chunks = [gi * group + cc for cc in range(group)]
        rows = [pl.ds(pl.multiple_of(c * c_len, c_len), c_len) for c in chunks]
        lws = [lw_s[rw, :] for rw in rows]
        gcums = [_dot_exact_lhs(l_cum, lw) for lw in lws]
        g_ends = [g[c_len - 1:c_len, :] for g in gcums]
        kts = [kap_s[rw, :] * jnp.exp(g - lw) for rw, g, lw in zip(rows, gcums, lws)]
        rts = [r_s[rw, :] * jnp.exp(g) for rw, g in zip(rows, gcums)]
        e_negs = [jnp.exp(-g) for g in gcums]
        lhss = [jnp.concatenate([kt, rt], axis=0) for kt, rt in zip(kts, rts)]
        o_bs = [lax.dot_general(lhs, _head_split(bet_s[rw, :] * en, lane_head), nt_dims, preferred_element_type=F32)
                for lhs, rw, en in zip(lhss, rows, e_negs)]
        o_ks = [lax.dot_general(lhs, _head_split(k_s[rw, :] * en, lane_head), nt_dims, preferred_element_type=F32)
                for lhs, rw, en in zip(lhss, rows, e_negs)]
        n_mats = [jnp.where(strict, o[0:c_len], 0.0) for o in o_bs]
        v_bds = [_head_split(v_s[rw, :], lane_head) for rw in rows]
        avs = [jnp.dot(jnp.where(strict, o[0:c_len], 0.0), v_bd, preferred_element_type=F32)
               for o, v_bd in zip(o_ks, v_bds)]
        rhss = [jnp.concatenate([kt, av], axis=1) for kt, av in zip(kts, avs)]
        a_rs = [jnp.concatenate([jnp.where(incl, -ob[c_len:], 0.0), jnp.where(incl, ok[c_len:], 0.0)], axis=1)
                for ob, ok in zip(o_bs, o_ks)]
        e_hats = [jnp.exp(ge - g) for ge, g in zip(g_ends, gcums)]
        bk_ts = [jnp.concatenate([bet_s[rw, :] * eh, k_s[rw, :] * eh], axis=0).T for rw, eh in zip(rows, e_hats)]
        xs = _solve_unit_lower(n_mats, rhss, lane_head2, xbd_s, between_stages)
        wus = [-x for x in xs]
        tops = [jnp.dot(a_r, jnp.concatenate([xbd_s[i],
                                              jnp.concatenate([jnp.zeros((LANES, LANES), F32), v_bd], axis=1)],
                                             axis=0), preferred_element_type=F32)
                for i, (a_r, v_bd) in enumerate(zip(a_rs, v_bds))]
        bots = [jnp.dot(bk_t, jnp.concatenate([wu, jnp.concatenate([zc, v_s[rw, :]], axis=1)], axis=0),
                        preferred_element_type=F32)
                for bk_t, wu, rw in zip(bk_ts, wus, rows)]
        for c, top, bot, rt, ge in zip(chunks, tops, bots, rts, g_ends):
            bot = jnp.where(same_head2, bot, 0.0)
            base = pl.multiple_of((c + group) * slot, SUBLANES)
            lm_s[pl.ds(base, c_len), :] = top[:, 0:LANES] + rt
            lm_s[pl.ds(base + c_len, LANES), :] = bot[:, 0:LANES] + jnp.where(eye, jnp.exp(ge), 0.0)
            yb_s[pl.ds(base, c_len), :] = top[:, LANES:]
            yb_s[pl.ds(base + c_len, LANES), :] = bot[:, LANES:]

    def advance(c, h_bd):
        base = pl.multiple_of((c + group) * slot, SUBLANES)
        res = _dot_3pass(lm_s[pl.ds(base, slot), :], h_bd) + yb_s[pl.ds(base, slot), :]
        y_s[pl.ds(pl.multiple_of((c + group) * c_len, c_len), c_len), :] = res[0:c_len]
        return res[c_len:]

    def phase3(c, carry):
        r0 = pl.multiple_of(c * rc, rc)
        o = _rwkv_post(y_s[pl.ds(r0 + group * c_len, rc), :], bon_s[pl.ds(r0, rc), :], g_s[pl.ds(r0, rc), :],
                       gnw_ref[...], gnb_ref[...], ones_bd)
        ob_ref[pl.ds(r0, rc), :] = o.astype(ob_ref.dtype)
        return carry

    n_stages = c_len // SUBLANES
    per_stage = -(-group // n_stages)
    n_groups = t_len // (c_len * group)
    lm_s[0:group * slot, :] = jnp.zeros((group * slot, LANES), F32)
    yb_s[0:group * slot, :] = jnp.zeros((group * slot, LANES), F32)

    def fused(gi, h):
        state = [h]

        def between_stages(blk):
            for cc in range(blk * per_stage, min((blk + 1) * per_stage, group)):
                state[0] = advance((gi - 1) * group + cc, state[0])

        precompute(gi, between_stages)
        return state[0]

    h_bd = lax.fori_loop(0, n_groups, fused, jnp.zeros((LANES, LANES), F32))
    n_ready = min(group, ((n_groups - 1) * group * c_len) // rc)
    for blk in range(group):
        h_bd = advance(jnp.int32((n_groups - 1) * group + blk), h_bd)
        if blk < n_ready:
            phase3(jnp.int32(blk), 0)
    s_bd = h_bd.T
    s_ref[0, 0] = s_bd[0:HEAD, 0:HEAD]
    s_ref[0, 1] = s_bd[HEAD:, HEAD:]
    lax.fori_loop(n_ready, n_ch, phase3, 0)


def _rwkv_specs(row_block, rows, q_rows, lay):
    pw, pa, pg, nb = lay["pw"], lay["pa"], lay["pg"], lay["nb"]

    def im(col_fn, on_rows=False):
        if row_block is None:
            return lambda b, hp: (b if on_rows else 0, col_fn(hp))
        return lambda hp: (row_block if on_rows else 0, col_fn(hp))

    def triple(n_rows, on_rows, c_r, c_k, c_v, c_wd, c_ad, c_gd):
        return [
            pl.BlockSpec((n_rows, LANES), im(lambda hp: c_r + hp, on_rows)),
            pl.BlockSpec((n_rows, LANES), im(lambda hp: c_k + hp, on_rows)),
            pl.BlockSpec((n_rows, LANES), im(lambda hp: c_v + hp, on_rows)),
            pl.BlockSpec((n_rows, pw * LANES), im(lambda hp: c_wd // pw, on_rows)),
            pl.BlockSpec((n_rows, pa * LANES), im(lambda hp: c_ad // pa, on_rows)),
            pl.BlockSpec((n_rows, pg * LANES), im(lambda hp: c_gd // pg, on_rows)),
        ]

    specs = triple(rows, True, lay["r"], lay["k"], lay["v"], lay["wd"], lay["ad"], lay["gd"])
    if q_rows:
        specs += triple(q_rows, False, 0, nb, 2 * nb, 0, pw, pw + pa)
    specs += triple(1, False, 0, nb, 2 * nb, 0, pw, pw + pa)
    per_pair = im(lambda hp: hp)
    specs += [
        pl.BlockSpec((1, LANES), per_pair),
        pl.BlockSpec((pw * LANES, LANES), per_pair),
        pl.BlockSpec((1, LANES), per_pair),
        pl.BlockSpec((pa * LANES, LANES), per_pair),
        pl.BlockSpec((pg * LANES, LANES), per_pair),
        pl.BlockSpec((1, LANES), per_pair),
        pl.BlockSpec((1, LANES), per_pair),
        pl.BlockSpec((1, LANES), per_pair),
        pl.BlockSpec((1, LANES), per_pair),
        pl.BlockSpec((1, LANES), per_pair),
    ]
    return specs


def _rwkv_prompt(p_main, p_tail, mu_rkv, mu_lora, params, *, bp, tp, d, lay):
    n_hp = d // LANES
    pw, pa, pg = lay["pw"], lay["pa"], lay["pg"]
    rc = 256 if tp % 256 == 0 else tp
    n_heads = d // HEAD
    n_chunks = tp // CHUNK
    group = next(g for g in (8, 4, 2, 1) if n_chunks % g == 0)
    kern = functools.partial(_rwkv_prompt_kernel, t_len=tp, rc=rc, group=group)
    vm = pltpu.VMEM((tp, LANES), F32)
    return pl.pallas_call(
        kern,
        grid=(bp, n_hp),
        in_specs=_rwkv_specs(None, tp, 0, lay),
        out_specs=[
            pl.BlockSpec((tp, LANES), lambda b, hp: (b, hp)),
            pl.BlockSpec((1, 2, HEAD, HEAD), lambda b, hp: (b, hp, 0, 0)),
        ],
        out_shape=[
            jax.ShapeDtypeStruct((bp * tp, d), BF16),
            jax.ShapeDtypeStruct((bp, n_heads, HEAD, HEAD), F32),
        ],
        scratch_shapes=[vm] * 8 + [
            pltpu.VMEM((tp + group * CHUNK, LANES), F32),
            pltpu.VMEM((tp, pw * LANES), BF16),
            pltpu.VMEM((tp, pa * LANES), BF16),
            pltpu.VMEM((tp, pg * LANES), BF16),
            pltpu.VMEM(((n_chunks + group) * (CHUNK + LANES), LANES), F32),
            pltpu.VMEM(((n_chunks + group) * (CHUNK + LANES), LANES), F32),
            pltpu.VMEM((group, 2 * CHUNK, 2 * LANES), F32),
        ],
        compiler_params=_cparams(2),
        name="rwkv_prompt",
    )(p_main, p_main, p_main, p_tail, p_tail, p_tail,
      mu_rkv, mu_rkv, mu_rkv, mu_lora, mu_lora, mu_lora, *params)


def _rwkv_sample_kernel(pr_ref, pk_ref, pv_ref, pwd_ref, pad_ref, pgd_ref,
                        qr_ref, qk_ref, qv_ref, qwd_ref, qad_ref, qgd_ref,
                        mur_ref, muk_ref, muv_ref, muwd_ref, muad_ref, mugd_ref,
                        w0_ref, w2_ref, a0_ref, a2_ref, g2_ref, kkw_ref, kaw_ref, rkw_ref, gnw_ref, gnb_ref,
                        st_ref, ob_ref, so_ref, y_s, tv_s, *, ts, bs):
    ones_bd = _head_ones()
    refs = (pr_ref, pk_ref, pv_ref, pwd_ref, pad_ref, pgd_ref)
    prevs = (qr_ref, qk_ref, qv_ref, qwd_ref, qad_ref, qgd_ref)
    mus = (mur_ref, muk_ref, muv_ref, muwd_ref, muad_ref, mugd_ref)
    zs = []
    for ref, q, mu in zip(refs, prevs, mus):
        p = ref[...].astype(F32)
        pp = jnp.concatenate([q[...].astype(F32), p[0:(ts - 1) * bs, :]], axis=0)
        zs.append(p + mu[...] * (pp - p))
    zr, zk, zv, zwd, zad, zgd = zs
    lw, g, kk, k2, beta, bonus = _rwkv_prep(
        zr, zk, zv, jnp.tanh(zwd).astype(BF16), zad.astype(BF16), jax.nn.sigmoid(zgd).astype(BF16),
        w0_ref[...], w2_ref[...], a0_ref[...], a2_ref[...], g2_ref[...],
        kkw_ref[...], kaw_ref[...], rkw_ref[...], ones_bd)
    w_dec = jnp.exp(lw)

    for t in range(ts):
        rows = slice(t * bs, (t + 1) * bs)
        for kind, a in enumerate((w_dec, kk, beta, k2, zr, zv)):
            tv_s[t, kind] = a[rows, :].T

    def total(p):
        p = p + pltpu.roll(p, 4, 0)
        p = p + pltpu.roll(p, 2, 0)
        return p + pltpu.roll(p, 1, 0)

    groups = HEAD // SUBLANES
    subi = lax.broadcasted_iota(jnp.int32, (SUBLANES, bs), 0)

    def row_group(gidx, carry):
        h = gidx // groups
        i0 = (gidx % groups) * SUBLANES
        lo = pl.multiple_of(h * HEAD, HEAD)
        row0 = pl.multiple_of(gidx * SUBLANES, SUBLANES)
        y_tiles = [jnp.zeros((SUBLANES, bs), F32) for _ in range(ts)]
        for half in range(0, SUBLANES, _SAMPLE_ROWS):
            s_rows = [st_ref[h, i0 + half + i].reshape(groups, SUBLANES, bs) for i in range(_SAMPLE_ROWS)]
            for t in range(ts):
                wt, kkt, bt, kt, rt = (tv_s[t, kind, pl.ds(lo, HEAD), :].reshape(groups, SUBLANES, bs)
                                       for kind in range(5))
                vt8 = tv_s[t, 5, pl.ds(row0, SUBLANES), :]
                for i in range(_SAMPLE_ROWS):
                    s = s_rows[i]
                    sa = -total(jnp.sum(s * kkt, axis=0))
                    s = s * wt + sa[None] * bt + vt8[half + i:half + i + 1, :][None] * kt
                    s_rows[i] = s
                    y_tiles[t] = jnp.where(subi == half + i, total(jnp.sum(s * rt, axis=0)), y_tiles[t])
            for i in range(_SAMPLE_ROWS):
                so_ref[h, i0 + half + i] = s_rows[i].reshape(HEAD, bs)
        for t in range(ts):
            y_s[t, pl.ds(row0, SUBLANES), :] = y_tiles[t]
        return carry

    lax.fori_loop(0, 2 * groups, row_group, 0)
    for t in range(ts):
        yt = y_s[t].T
        rows = slice(t * bs, (t + 1) * bs)
        o = _rwkv_post(yt, bonus[rows], g[rows], gnw_ref[...], gnb_ref[...], ones_bd)
        ob_ref[rows, :] = o.astype(ob_ref.dtype)


def _rwkv_sample(p_main, p_tail, q_rkv, q_lora, mu_rkv, mu_lora, params, st, *, ts, bs, d, mp, lay):
    n_hp = d // LANES
    rows = ts * bs
    rblk = mp // rows
    in_specs = _rwkv_specs(rblk, rows, bs, lay)
    in_specs.append(pl.BlockSpec((2, HEAD, HEAD, bs), lambda hp: (hp, 0, 0, 0)))
    kern = functools.partial(_rwkv_sample_kernel, ts=ts, bs=bs)
    return pl.pallas_call(
        kern,
        grid=(n_hp,),
        in_specs=in_specs,
        out_specs=[
            pl.BlockSpec((rows, LANES), lambda hp: (0, hp)),
            pl.BlockSpec((2, HEAD, HEAD, bs), lambda hp: (hp, 0, 0, 0)),
        ],
        out_shape=[
            jax.ShapeDtypeStruct((rows, d), BF16),
            jax.ShapeDtypeStruct(st.shape, F32),
        ],
        scratch_shapes=[pltpu.VMEM((ts, LANES, bs), F32), pltpu.VMEM((ts, 6, LANES, bs), F32)],
        compiler_params=_cparams(1),
        name="rwkv_sample",
    )(p_main, p_main, p_main, p_tail, p_tail, p_tail, q_rkv, q_rkv, q_rkv, q_lora, q_lora, q_lora,
      mu_rkv, mu_rkv, mu_rkv, mu_lora, mu_lora, mu_lora, *params, st)


def _pad_cols(a, width):
    return jnp.pad(a, ((0, 0), (0, width - a.shape[1])))


def _pad_rows(a, height):
    return jnp.pad(a, ((0, height - a.shape[0]), (0, 0)))


def _tiles(d, d_ff):
    pick = lambda n, cands: next(c for c in cands if n % c == 0)
    return dict(
        tm=512, tm_wide=256,
        main=pick(5 * d, (1280, 1024, 512, 256, 128)),
        gates=pick(2 * d, (1024, 512, 256, 128)),
        prev=pick(d, (1024, 512, 256, 128)),
        up=pick(d_ff, (1024, 512, 256, 128)),
        down=pick(d, (512, 256, 128)),
    )


def kernel(x_prompt, x_sample, p_prompt, p_sample, state_rg_h, state_rg_conv, state_rwkv, state_shift,
           norm_mix, w_in, conv_w, conv_b, rg_wa, rg_ba, rg_wx, rg_bx, rg_lam, w_rg_o,
           mu_shift, rw_w0, rw_w2, rw_a0, rw_a2, rw_g2, rw_kk, rw_ka, rw_rk, rw_gn_w, rw_gn_b,
           w_rw_o, w_o, norm_ffn, w_up, w_down, norm_ple, w_ple_gate, w_ple, norm_f):
    bp, tp, d = x_prompt.shape
    bs, ts, _ = x_sample.shape
    depth = w_in.shape[0]
    n_heads, head = rw_rk.shape[1], rw_rk.shape[2]
    r_w, r_a, r_g = rw_w2.shape[1], rw_a2.shape[1], rw_g2.shape[1]
    d_ple = w_ple.shape[1]
    d_ff = w_up.shape[2]
    n_tap = conv_w.shape[1]
    assert head == HEAD and n_heads * HEAD == d and d % LANES == 0 and bs == LANES
    assert rg_wa.shape[2] == LANES and tp % CHUNK == 0 and n_tap == 4
    mp, ms = bp * tp, bs * ts
    nb = d // LANES
    pw, pa, pg = (_round_up(r, LANES) // LANES for r in (r_w, r_a, r_g))
    n_lora = (pw + pa + pg) * LANES
    lay = dict(u=0, g=nb, r=2 * nb, k=3 * nb, v=4 * nb, wd=0, ad=pw, gd=pw + pa, pw=pw, pa=pa, pg=pg, nb=nb)
    t = _tiles(d, d_ff)
    tm = t["tm"]
    assert mp % tm == 0 and ms % tm == 0 and mp % (ts * bs) == 0 and ms % t["tm_wide"] == 0
    assert lay["wd"] % pw == 0 and lay["ad"] % pa == 0 and lay["gd"] % pg == 0 and pw % pa == 0 and (pw + pa) % pg == 0
    assert (2 * d) % t["prev"] == 0 and (5 * d + r_w + r_a + r_g) % SUBLANES == 0
    mm = functools.partial(_mm, mp=mp, ms=ms)

    x = _Rows(x_prompt.reshape(mp, d), jnp.transpose(x_sample, (1, 0, 2)).reshape(ms, d))
    hp_l, cp_l, sp_l, xp_l, hs_l, cs_l, ss_l, xs_l = [], [], [], [], [], [], [], []
    y_p = y_s = None
    for i in range(depth):
        o_rw = 2 * d
        o_lora = o_rw + 3 * d
        o_g = o_lora + r_w + r_a + r_g
        wt = jnp.swapaxes(w_in[i], 0, 1)

        def lora_rows(a):
            return jnp.concatenate([
                _pad_rows(a[o_lora:o_lora + r_w], pw * LANES),
                _pad_rows(a[o_lora + r_w:o_lora + r_w + r_a], pa * LANES),
                _pad_rows(a[o_lora + r_w + r_a:o_g], pg * LANES)], axis=0)

        wt_lora = lora_rows(wt)
        mu_all = jnp.concatenate([jnp.zeros((o_rw, 1), F32), mu_shift[i][:, None], jnp.zeros((2 * d, 1), F32)], axis=0)
        mu_rkv = mu_shift[i][None, :3 * d]
        mu_lora = lora_rows(mu_all).reshape(1, n_lora)
        rw_params = (
            rw_w0[i][None], _pad_rows(rw_w2[i], pw * LANES).astype(BF16),
            rw_a0[i][None], _pad_rows(rw_a2[i], pa * LANES).astype(BF16),
            _pad_rows(rw_g2[i], pg * LANES).astype(BF16),
            rw_kk[i][None], rw_ka[i][None], rw_rk[i].reshape(1, d), rw_gn_w[i][None], rw_gn_b[i][None])
        rg_params = (conv_w[i], conv_b[i][None], rg_wa[i].astype(BF16), rg_ba[i][:, None, :],
                     rg_wx[i].astype(BF16), rg_bx[i][:, None, :], rg_lam[i][None])

        xn = _Rows(_rmsnorm(x, norm_mix[i], BF16, tm, mp))
        xa = x.arrays if len(x.arrays) == 2 else (x.arrays[0][:mp], x.arrays[0][mp:])
        x_last = jnp.concatenate([xa[0].reshape(bp, tp, d)[:, -1], xa[1][(ts - 1) * bs:]], axis=0)
        n_last = _round_up(bp + bs, SUBLANES)
        xn_last = _rmsnorm(_Rows(_pad_rows(x_last, n_last)), norm_mix[i], F32, n_last, n_last)
        (p_main,) = mm([xn], [wt], [], _epi_plain, [(F32, False)], tm=tm, tn=t["main"], n_out=5 * d,
                       cast_w=True, w_transposed=True, name="in_proj")
        (p_gate,) = mm([xn], [wt], [], _epi_plain, [(BF16, False)], tm=tm, tn=t["gates"], n_out=2 * d,
                       w_row_start=o_g, cast_w=True, w_transposed=True, name="in_proj_gates")
        (p_lora,) = mm([xn], [wt_lora], [], _epi_plain, [(BF16, False)], tm=tm, tn=n_lora, n_out=n_lora,
                       cast_w=True, w_transposed=True, name="in_proj_lora")
        xprev = _Rows(state_shift[i])
        (q_rkv,) = _mm([xprev], [wt], [], _epi_plain, [(F32, False)], mp=bs, ms=0, tm=bs, tn=t["prev"],
                       n_out=3 * d, w_col_off=o_rw // t["prev"], cast_w=True, w_transposed=True, name="prev_proj")
        (q_lora,) = _mm([xprev], [wt_lora], [], _epi_plain, [(BF16, False)], mp=bs, ms=0, tm=bs, tn=n_lora,
                        n_out=n_lora, cast_w=True, w_transposed=True, name="prev_proj_lora")

        cwid = 2 * LANES if nb % 2 == 0 else LANES
        hg_p, h_p, c_p = _rg_prompt(p_main, *rg_params, bp=bp, tp=tp, d=d, cb_u=lay["u"], cb_g=lay["g"], cwid=cwid)
        cbuf_t = jnp.transpose(state_rg_conv[i], (1, 0, 2))
        hg_s, h_s, c_s = _rg_sample(p_main, cbuf_t, state_rg_h[i], *rg_params, ts=ts, bs=bs, d=d, mp=mp,
                                    cb_u=lay["u"], cb_g=lay["g"], cwid=cwid)

        ob_p, s_p = _rwkv_prompt(p_main, p_lora, mu_rkv, mu_lora, rw_params, bp=bp, tp=tp, d=d, lay=lay)
        st = jnp.transpose(state_rwkv[i], (1, 2, 3, 0))
        ob_s, st_new = _rwkv_sample(p_main, p_lora, q_rkv, q_lora, mu_rkv, mu_lora, rw_params, st,
                                    ts=ts, bs=bs, d=d, mp=mp, lay=lay)
        s_s = jnp.transpose(st_new, (3, 0, 1, 2))

        tw = t["tm_wide"]
        (merged,) = mm([_Rows(hg_p, hg_s), _Rows(ob_p, ob_s)], [w_rg_o[i].astype(BF16), w_rw_o[i].astype(BF16)],
                       [("tile", _Rows(p_gate), 0), ("tile", _Rows(p_gate), 1)],
                       _epi_merge, [(BF16, False)], tm=tw, tn=d, n_out=d, single_buffer_w=True, name="merge")
        x1, xn2 = mm([_Rows(merged)], [w_o[i]], [("tile", x, 0), ("row", norm_ffn[i][None], 0)],
                     _epi_residual_norm, [(F32, False), (BF16, False)], tm=tw, tn=d, n_out=d, cast_w=True,
                     single_buffer_w=True, name="out_proj")
        (hf,) = mm([_Rows(xn2)], [w_up[i]], [], _epi_relu2, [(BF16, False)], tm=tm, tn=t["up"], n_out=d_ff,
                   cast_w=True, name="mlp_up")
        (x2,) = mm([_Rows(hf)], [w_down[i].astype(BF16)], [("tile", _Rows(x1), 0)], _epi_residual, [(F32, False)],
                   tm=tm, tn=t["down"], n_out=d, name="mlp_down")
        p_rows = _Rows(p_prompt[i].reshape(mp, d_ple), jnp.transpose(p_sample[i], (1, 0, 2)).reshape(ms, d_ple))
        ple_ws = [w_ple_gate[i], w_ple[i]]
        g_ple = ("row", norm_ple[i][None], 0)
        if i == depth - 1:
            y_p, y_s = mm([("norm_of", 0, 2), p_rows], ple_ws,
                          [("tile", _Rows(x2), 0), ("row", norm_f[None], 0), g_ple],
                          _epi_ple_final, [(F32, True)], tm=tw, tn=d, n_out=d, cast_w=True, single_buffer_w=True,
                          name="ple_final")
        else:
            x = _Rows(*mm([("norm_of", 0, 1), p_rows], ple_ws, [("tile", _Rows(x2), 0), g_ple], _epi_ple,
                          [(F32, False)], tm=tw, tn=d, n_out=d, cast_w=True, single_buffer_w=True, name="ple"))

        hp_l.append(h_p[:, 0])
        cp_l.append(c_p[:, SUBLANES - (n_tap - 1):])
        sp_l.append(s_p)
        xp_l.append(xn_last[:bp])
        hs_l.append(h_s)
        cs_l.append(jnp.transpose(c_s, (1, 0, 2)))
        ss_l.append(s_s)
        xs_l.append(xn_last[bp:bp + bs])

    y_prompt = y_p.reshape(bp, tp, d)
    y_sample = jnp.transpose(y_s.reshape(ts, bs, d), (1, 0, 2))
    return (y_prompt, y_sample,
            jnp.stack(hp_l), jnp.stack(cp_l), jnp.stack(sp_l), jnp.stack(xp_l),
            jnp.stack(hs_l), jnp.stack(cs_l), jnp.stack(ss_l), jnp.stack(xs_l))
```

```python
import functools

import jax
import jax.numpy as jnp
from jax import lax
from jax.experimental import pallas as pl
from jax.experimental.pallas import tpu as pltpu

F32 = jnp.float32
BF16 = jnp.bfloat16
LANES = 128
SUBLANES = 8
HEAD = 64
CHUNK = 64
_SAMPLE_ROWS = 4
EPS = 1e-6
GN_EPS = 64e-5
RG_C = 8.0
VMEM_LIMIT = 56 * 1024 * 1024


def _cparams(n_grid, vmem=VMEM_LIMIT):
    return pltpu.CompilerParams(dimension_semantics=("arbitrary",) * n_grid, vmem_limit_bytes=vmem)


def _round_up(x, m):
    return (x + m - 1) // m * m


def _rms(x, g):
    ms = jnp.mean(x * x, axis=-1, keepdims=True)
    return (x * lax.rsqrt(ms + EPS)) * g


class _Rows:
    def __init__(self, *arrays):
        self.arrays = arrays

    @property
    def width(self):
        return self.arrays[0].shape[1]

    def specs(self, tm, n_p, col):
        if len(self.arrays) == 1:
            return [lambda wid: pl.BlockSpec((tm, wid), lambda n, i: (i, col(n)))]
        return [lambda wid: pl.BlockSpec((tm, wid), lambda n, i: (jnp.minimum(i, n_p - 1), col(n))),
                lambda wid: pl.BlockSpec((tm, wid), lambda n, i: (jnp.maximum(i - n_p, 0), col(n)))]


def _load_rows(refs, is_prompt):
    if len(refs) == 1:
        return refs[0][...]
    return jnp.where(is_prompt, refs[0][...], refs[1][...])


def _rmsnorm_kernel(*refs, n_src, n_p):
    g_ref, o_ref = refs[n_src], refs[n_src + 1]
    x = _load_rows(refs[:n_src], pl.program_id(1) < n_p)
    o_ref[...] = _rms(x, g_ref[...]).astype(o_ref.dtype)


def _rmsnorm(x, g, out_dtype, tm, mp):
    d = x.width
    m = sum(a.shape[0] for a in x.arrays)
    n_p = mp // tm
    in_specs = [mk(d) for mk in x.specs(tm, n_p, lambda n: 0)] + [pl.BlockSpec((1, d), lambda n, i: (0, 0))]
    return pl.pallas_call(
        functools.partial(_rmsnorm_kernel, n_src=len(x.arrays), n_p=n_p),
        grid=(1, m // tm),
        in_specs=in_specs,
        out_specs=pl.BlockSpec((tm, d), lambda n, i: (i, 0)),
        out_shape=jax.ShapeDtypeStruct((m, d), out_dtype),
        compiler_params=_cparams(2),
        name="rmsnorm",
    )(*x.arrays, g.reshape(1, d))


def _mm_kernel(*refs, x_counts, x_norm, n_w, cast_w, w_transposed, extra_counts, out_counts, n_p, epilogue):
    pos = 0
    x_refs = []
    for cnt in x_counts:
        x_refs.append(refs[pos:pos + cnt])
        pos += cnt
    w_refs = refs[pos:pos + n_w]
    pos += n_w
    e_refs = []
    for cnt in extra_counts:
        e_refs.append(refs[pos:pos + cnt])
        pos += cnt
    o_refs = []
    for cnt in out_counts:
        o_refs.append(refs[pos:pos + cnt])
        pos += cnt
    w_scratch = refs[pos:]
    i = pl.program_id(1)
    is_prompt = i < n_p
    if cast_w:
        @pl.when(i == 0)
        def _():
            for w_ref, s_ref in zip(w_refs, w_scratch):
                s_ref[...] = w_ref[...].astype(BF16)
        ws = [s[...] for s in w_scratch]
    else:
        ws = [w[...] for w in w_refs]
    dims = (((1,), (1 if w_transposed else 0,)), ((), ()))
    e_vals = [_load_rows(er, is_prompt) for er in e_refs]
    x_vals = []
    for idx, xr in enumerate(x_refs):
        if xr:
            x_vals.append(_load_rows(xr, is_prompt).astype(BF16))
        else:
            e_idx, g_idx = x_norm[idx]
            x_vals.append(_rms(e_vals[e_idx], e_vals[g_idx]).astype(BF16))
    accs = [lax.dot_general(x, w, dims, preferred_element_type=F32) for x, w in zip(x_vals, ws)]
    outs = epilogue(accs, e_vals)
    for refs_o, tile in zip(o_refs, outs):
        if len(refs_o) == 1:
            refs_o[0][...] = tile.astype(refs_o[0].dtype)
        else:
            @pl.when(is_prompt)
            def _():
                refs_o[0][...] = tile.astype(refs_o[0].dtype)

            @pl.when(jnp.logical_not(is_prompt))
            def _():
                refs_o[1][...] = tile.astype(refs_o[1].dtype)


def _mm(xs, ws, extras, epilogue, outs, *, mp, ms, tm, tn, n_out, w_col_off=0, w_row_start=None, cast_w=False,
        w_transposed=False, single_buffer_w=False, name="mm"):
    n_p = mp // tm
    m = mp + ms
    grid = (n_out // tn, m // tm)
    in_specs, args, x_counts, extra_counts, x_norm = [], [], [], [], {}
    for idx, x in enumerate(xs):
        if isinstance(x, tuple):
            x_norm[idx] = (x[1], x[2])
            x_counts.append(0)
            continue
        in_specs += [mk(x.width) for mk in x.specs(tm, n_p, lambda n: 0)]
        args += list(x.arrays)
        x_counts.append(len(x.arrays))
    w_mode = dict(pipeline_mode=pl.Buffered(1)) if single_buffer_w else {}
    if w_transposed and w_row_start is not None:
        w_blocks = [(tn, w.shape[1]) for w in ws]
        in_specs += [pl.BlockSpec((pl.Element(tn), pl.Element(w.shape[1])),
                                  lambda n, i: (pl.multiple_of(w_row_start + n * tn, SUBLANES), 0), **w_mode)
                     for w in ws]
    elif w_transposed:
        w_blocks = [(tn, w.shape[1]) for w in ws]
        in_specs += [pl.BlockSpec(blk, lambda n, i: (n + w_col_off, 0), **w_mode) for blk in w_blocks]
    else:
        w_blocks = [(w.shape[0], tn) for w in ws]
        in_specs += [pl.BlockSpec(blk, lambda n, i: (0, n + w_col_off), **w_mode) for blk in w_blocks]
    args += list(ws)
    for kind, src, off in extras:
        if kind == "tile":
            in_specs += [mk(tn) for mk in src.specs(tm, n_p, lambda n, off=off: n + off)]
            args += list(src.arrays)
            extra_counts.append(len(src.arrays))
        elif kind == "row":
            in_specs.append(pl.BlockSpec((1, tn), lambda n, i, off=off: (0, n + off)))
            args.append(src)
            extra_counts.append(1)
        else:
            raise ValueError(kind)
    out_specs, out_shape, out_counts = [], [], []
    for dtype, split in outs:
        if split:
            out_specs += [pl.BlockSpec((tm, tn), lambda n, i: (jnp.minimum(i, n_p - 1), n)),
                          pl.BlockSpec((tm, tn), lambda n, i: (jnp.maximum(i - n_p, 0), n))]
            out_shape += [jax.ShapeDtypeStruct((mp, n_out), dtype), jax.ShapeDtypeStruct((ms, n_out), dtype)]
            out_counts.append(2)
        else:
            out_specs.append(pl.BlockSpec((tm, tn), lambda n, i: (i, n)))
            out_shape.append(jax.ShapeDtypeStruct((m, n_out), dtype))
            out_counts.append(1)
    kern = functools.partial(_mm_kernel, x_counts=tuple(x_counts), x_norm=x_norm, n_w=len(ws), cast_w=cast_w,
                             w_transposed=w_transposed, extra_counts=tuple(extra_counts),
                             out_counts=tuple(out_counts), n_p=n_p, epilogue=epilogue)
    scratch = [pltpu.VMEM(blk, BF16) for blk in w_blocks] if cast_w else []
    res = pl.pallas_call(
        kern,
        grid=grid,
        in_specs=in_specs,
        out_specs=out_specs,
        out_shape=out_shape,
        scratch_shapes=scratch,
        compiler_params=_cparams(2),
        name=name,
    )(*args)
    return res


def _epi_plain(accs, extras):
    return (accs[0],)


def _epi_residual(accs, extras):
    return (extras[0] + accs[0],)


def _epi_merge(accs, extras):
    ga, gb = extras
    return (jax.nn.sigmoid(ga.astype(F32)) * accs[0] + jax.nn.sigmoid(gb.astype(F32)) * accs[1],)


def _epi_residual_norm(accs, extras):
    x = extras[0] + accs[0]
    return x, _rms(x, extras[1])


def _epi_relu2(accs, extras):
    h = jnp.maximum(accs[0], 0.0)
    return (h * h,)


def _epi_ple(accs, extras):
    return (extras[0] + jax.nn.sigmoid(accs[0]) * accs[1],)


def _epi_ple_final(accs, extras):
    return (_rms(extras[0] + jax.nn.sigmoid(accs[0]) * accs[1], extras[1]),)


def _softplus(x):
    return jnp.maximum(x, 0.0) + jnp.log1p(jnp.exp(-jnp.abs(x)))


def _split_bf16(x, parts):
    out = []
    rem = x
    for _ in range(parts):
        p = rem.astype(BF16)
        out.append(p)
        rem = rem - p.astype(F32)
    return out


def _dot_exact_lhs(a_bf16, x, parts=3):
    acc = None
    for p in _split_bf16(x, parts):
        t = jnp.dot(a_bf16, p, preferred_element_type=F32)
        acc = t if acc is None else acc + t
    return acc


def _head_sum(x, ones_bd):
    return jnp.dot(x.astype(BF16), ones_bd, preferred_element_type=F32)


def _sigmoid(x):
    return 0.5 * jnp.tanh(0.5 * x) + 0.5


def _dot_3pass(a, b):
    a_hi, a_lo = _split_bf16(a, 2)
    b_hi, b_lo = _split_bf16(b, 2)
    return (jnp.dot(a_hi, b_hi, preferred_element_type=F32)
            + jnp.dot(a_hi, b_lo, preferred_element_type=F32)
            + jnp.dot(a_lo, b_hi, preferred_element_type=F32))


def _head_ones():
    r = lax.broadcasted_iota(jnp.int32, (LANES, LANES), 0) // HEAD
    c = lax.broadcasted_iota(jnp.int32, (LANES, LANES), 1) // HEAD
    return (r == c).astype(BF16)


def _rwkv_prep(zr, zk, zv, tw, za, sg, w0, w2, a0, a2, g2, kkw, kaw, rkw, ones_bd):
    wlog = -_softplus(-(w0 + jnp.dot(tw, w2, preferred_element_type=F32))) - 0.5
    lw = -jnp.exp(wlog)
    a = _sigmoid(a0 + jnp.dot(za, a2, preferred_element_type=F32))
    g = jnp.dot(sg, g2, preferred_element_type=F32)
    kk = zk * kkw
    ss = _head_sum(kk * kk, ones_bd)
    kk = kk * lax.rsqrt(jnp.maximum(ss, 1e-24))
    k2 = zk * (1.0 + (a - 1.0) * kaw)
    beta = kk * a
    bonus = _head_sum(zr * k2 * rkw, ones_bd) * zv
    return lw, g, kk, k2, beta, bonus


def _rwkv_post(y, bonus, g, gnw, gnb, ones_bd):
    mu = _head_sum(y, ones_bd) * (1.0 / HEAD)
    d = y - mu
    var = _head_sum(d * d, ones_bd) * (1.0 / HEAD)
    yn = (d * lax.rsqrt(var + GN_EPS)) * gnw + gnb
    return (yn + bonus) * g


def _rg_gates(xc, wa, ba, wx, bx, sp):
    xcb = xc.astype(BF16)
    r = _sigmoid(jnp.dot(xcb, wa, preferred_element_type=F32) + ba)
    i = _sigmoid(jnp.dot(xcb, wx, preferred_element_type=F32) + bx)
    log_a = (-RG_C * r) * sp
    a = jnp.exp(log_a)
    mult = jnp.sqrt(-jnp.tanh(log_a) * (a * a + 1.0))
    return a, mult, i * xc


def _rg_prompt_kernel(u_ref, gt_ref, cw_ref, cb_ref, wa_ref, ba_ref, wx_ref, bx_ref, lam_ref,
                      hg_ref, hl_ref, cs_ref, us_ref, a_ref, b_ref, *, t_len, rc):
    cwid = u_ref.shape[1]
    nb = cwid // LANES
    us_ref[0:SUBLANES, :] = jnp.zeros((SUBLANES, cwid), F32)
    us_ref[SUBLANES:, :] = u_ref[...]
    sp = _softplus(-lam_ref[...])
    w0 = cw_ref[0:1, :]
    w1 = cw_ref[1:2, :]
    w2 = cw_ref[2:3, :]
    w3 = cw_ref[3:4, :]
    cb = cb_ref[...]
    n_ch = t_len // rc

    def phase1(c, carry):
        r0 = pl.multiple_of(c * rc, rc)
        e = us_ref[pl.ds(r0, rc + SUBLANES), :]
        u0 = e[SUBLANES:]
        u1 = pltpu.roll(e, 1, 0)[SUBLANES:]
        u2 = pltpu.roll(e, 2, 0)[SUBLANES:]
        u3 = pltpu.roll(e, 3, 0)[SUBLANES:]
        xc = cb + (u3 * w0 + u2 * w1 + u1 * w2 + u0 * w3)
        row = lax.broadcasted_iota(jnp.int32, (rc, LANES), 0) + r0
        for n in range(nb):
            sl = slice(n * LANES, (n + 1) * LANES)
            a, mult, ix = _rg_gates(xc[:, sl], wa_ref[n], ba_ref[n], wx_ref[n], bx_ref[n], sp[:, sl])
            mult = jnp.where(row == 0, 1.0, mult)
            a_ref[n, pl.ds(r0, rc), :] = a
            b_ref[n, pl.ds(r0, rc), :] = mult * ix
        return carry

    lax.fori_loop(0, n_ch, phase1, 0)

    rowi = lax.broadcasted_iota(jnp.int32, (SUBLANES, LANES), 0)

    def scan(i, h_prev):
        r0 = pl.multiple_of(i * SUBLANES, SUBLANES)
        out = []
        for n in range(nb):
            a = a_ref[n, pl.ds(r0, SUBLANES), :]
            b = b_ref[n, pl.ds(r0, SUBLANES), :]
            for d in (1, 2, 4):
                a_sh = jnp.where(rowi < d, 1.0, pltpu.roll(a, d, 0))
                b_sh = jnp.where(rowi < d, 0.0, pltpu.roll(b, d, 0))
                b = a * b_sh + b
                a = a * a_sh
            h = b + a * h_prev[n]
            b_ref[n, pl.ds(r0, SUBLANES), :] = h
            out.append(h[SUBLANES - 1:SUBLANES, :])
        return tuple(out)

    h_last = lax.fori_loop(0, t_len // SUBLANES, scan, tuple(jnp.zeros((1, LANES), F32) for _ in range(nb)),
                           unroll=4)
    hl_ref[0] = jnp.concatenate(list(h_last), axis=1)
    cs_ref[0] = u_ref[pl.ds(t_len - SUBLANES, SUBLANES), :]

    def phase3(c, carry):
        r0 = pl.multiple_of(c * rc, rc)
        for n in range(nb):
            sl = slice(n * LANES, (n + 1) * LANES)
            gate = gt_ref[pl.ds(r0, rc), sl]
            hg_ref[pl.ds(r0, rc), sl] = (b_ref[n, pl.ds(r0, rc), :] * jax.nn.gelu(gate)).astype(hg_ref.dtype)
        return carry

    lax.fori_loop(0, n_ch, phase3, 0)


def _rg_prompt(p, conv_w, conv_b, wa, ba, wx, bx, lam, *, bp, tp, d, cb_u, cb_g, cwid):
    nb = cwid // LANES
    nblk = d // cwid
    rc = 256 if tp % 256 == 0 else tp
    kern = functools.partial(_rg_prompt_kernel, t_len=tp, rc=rc)
    return pl.pallas_call(
        kern,
        grid=(bp, nblk),
        in_specs=[
            pl.BlockSpec((tp, cwid), lambda b, n: (b, cb_u * LANES // cwid + n)),
            pl.BlockSpec((tp, cwid), lambda b, n: (b, cb_g * LANES // cwid + n)),
            pl.BlockSpec((4, cwid), lambda b, n: (0, n)),
            pl.BlockSpec((1, cwid), lambda b, n: (0, n)),
            pl.BlockSpec((nb, LANES, LANES), lambda b, n: (n, 0, 0)),
            pl.BlockSpec((nb, 1, LANES), lambda b, n: (n, 0, 0)),
            pl.BlockSpec((nb, LANES, LANES), lambda b, n: (n, 0, 0)),
            pl.BlockSpec((nb, 1, LANES), lambda b, n: (n, 0, 0)),
            pl.BlockSpec((1, cwid), lambda b, n: (0, n)),
        ],
        out_specs=[
            pl.BlockSpec((tp, cwid), lambda b, n: (b, n)),
            pl.BlockSpec((1, 1, cwid), lambda b, n: (b, 0, n)),
            pl.BlockSpec((1, SUBLANES, cwid), lambda b, n: (b, 0, n)),
        ],
        out_shape=[
            jax.ShapeDtypeStruct((bp * tp, d), BF16),
            jax.ShapeDtypeStruct((bp, 1, d), F32),
            jax.ShapeDtypeStruct((bp, SUBLANES, d), F32),
        ],
        scratch_shapes=[
            pltpu.VMEM((tp + SUBLANES, cwid), F32),
            pltpu.VMEM((nb, tp, LANES), F32),
            pltpu.VMEM((nb, tp, LANES), F32),
        ],
        compiler_params=_cparams(2),
        name="rg_prompt",
    )(p, p, conv_w, conv_b, wa, ba, wx, bx, lam)


def _rg_sample_kernel(u_ref, gt_ref, cbuf_ref, h0_ref, cw_ref, cb_ref, wa_ref, ba_ref, wx_ref, bx_ref, lam_ref,
                      hg_ref, hl_ref, cs_ref, *, ts, bs):
    cwid = u_ref.shape[1]
    nb = cwid // LANES
    n_tap = cw_ref.shape[0]
    sp = _softplus(-lam_ref[...])
    cb = cb_ref[...]
    ext = [cbuf_ref[j] for j in range(n_tap - 1)] + [u_ref[t * bs:(t + 1) * bs, :] for t in range(ts)]
    for j in range(n_tap - 1):
        cs_ref[j] = ext[len(ext) - (n_tap - 1) + j]
    h = h0_ref[...]
    for t in range(ts):
        conv = ext[t] * cw_ref[0:1, :]
        for j in range(1, n_tap):
            conv = conv + ext[t + j] * cw_ref[j:j + 1, :]
        xc = cb + conv
        pieces = []
        for n in range(nb):
            sl = slice(n * LANES, (n + 1) * LANES)
            a, mult, ix = _rg_gates(xc[:, sl], wa_ref[n], ba_ref[n], wx_ref[n], bx_ref[n], sp[:, sl])
            pieces.append(a * h[:, sl] + mult * ix)
        h = pieces[0] if nb == 1 else jnp.concatenate(pieces, axis=1)
        gate = gt_ref[t * bs:(t + 1) * bs, :]
        hg_ref[t * bs:(t + 1) * bs, :] = (h * jax.nn.gelu(gate)).astype(hg_ref.dtype)
    hl_ref[...] = h


def _rg_sample(p, cbuf_t, h0, conv_w, conv_b, wa, ba, wx, bx, lam, *, ts, bs, d, mp, cb_u, cb_g, cwid):
    nb = cwid // LANES
    nblk = d // cwid
    rows = ts * bs
    rblk = mp // rows
    kern = functools.partial(_rg_sample_kernel, ts=ts, bs=bs)
    n_tap = conv_w.shape[0]
    return pl.pallas_call(
        kern,
        grid=(nblk,),
        in_specs=[
            pl.BlockSpec((rows, cwid), lambda n: (rblk, cb_u * LANES // cwid + n)),
            pl.BlockSpec((rows, cwid), lambda n: (rblk, cb_g * LANES // cwid + n)),
            pl.BlockSpec((n_tap - 1, bs, cwid), lambda n: (0, 0, n)),
            pl.BlockSpec((bs, cwid), lambda n: (0, n)),
            pl.BlockSpec((n_tap, cwid), lambda n: (0, n)),
            pl.BlockSpec((1, cwid), lambda n: (0, n)),
            pl.BlockSpec((nb, LANES, LANES), lambda n: (n, 0, 0)),
            pl.BlockSpec((nb, 1, LANES), lambda n: (n, 0, 0)),
            pl.BlockSpec((nb, LANES, LANES), lambda n: (n, 0, 0)),
            pl.BlockSpec((nb, 1, LANES), lambda n: (n, 0, 0)),
            pl.BlockSpec((1, cwid), lambda n: (0, n)),
        ],
        out_specs=[
            pl.BlockSpec((rows, cwid), lambda n: (0, n)),
            pl.BlockSpec((bs, cwid), lambda n: (0, n)),
            pl.BlockSpec((n_tap - 1, bs, cwid), lambda n: (0, 0, n)),
        ],
        out_shape=[
            jax.ShapeDtypeStruct((rows, d), BF16),
            jax.ShapeDtypeStruct((bs, d), F32),
            jax.ShapeDtypeStruct((n_tap - 1, bs, d), F32),
        ],
        compiler_params=_cparams(1),
        name="rg_sample",
    )(p, p, cbuf_t, h0, conv_w, conv_b, wa, ba, wx, bx, lam)


def _shift_rows(x, prev_row, rowi):
    return jnp.where(rowi == 0, prev_row, pltpu.roll(x, 1, 0))


def _head_split(z, lane_head):
    return jnp.concatenate([jnp.where(lane_head == 0, z, 0.0), jnp.where(lane_head == 1, z, 0.0)], axis=0)


def _solve_unit_lower(n_mats, rhss, lane_head, xbd_ref, between_stages=None):
    n_sys = len(n_mats)
    c = n_mats[0].shape[0]
    br = SUBLANES
    xbd_ref[...] = jnp.zeros(xbd_ref.shape, F32)
    done = [[] for _ in range(n_sys)]
    for blk in range(c // br):
        lo = blk * br
        rs = []
        for i in range(n_sys):
            r = rhss[i][lo:lo + br, :]
            if blk > 0:
                r = r - jnp.dot(n_mats[i][lo:lo + br, :], xbd_ref[i], preferred_element_type=F32)
            rs.append(r)
        if between_stages is not None:
            between_stages(blk)
        for j in range(br - 1):
            for i in range(n_sys):
                nrow = n_mats[i][lo:lo + br, :]
                mult = jnp.where(lane_head == 0, nrow[:, lo + j:lo + j + 1], nrow[:, HEAD + lo + j:HEAD + lo + j + 1])
                rs[i] = rs[i] - mult * rs[i][j:j + 1, :]
        for i in range(n_sys):
            done[i].append(rs[i])
            xbd_ref[i, lo:lo + br, :] = jnp.where(lane_head == 0, rs[i], 0.0)
            xbd_ref[i, c + lo:c + lo + br, :] = jnp.where(lane_head == 1, rs[i], 0.0)
    return [jnp.concatenate(b, axis=0) for b in done]


def _rwkv_prompt_kernel(pr_ref, pk_ref, pv_ref, pwd_ref, pad_ref, pgd_ref,
                        mur_ref, muk_ref, muv_ref, muwd_ref, muad_ref, mugd_ref,
                        w0_ref, w2_ref, a0_ref, a2_ref, g2_ref, kkw_ref, kaw_ref, rkw_ref, gnw_ref, gnb_ref,
                        ob_ref, s_ref,
                        r_s, lw_s, k_s, v_s, kap_s, bet_s, g_s, bon_s, y_s, tw_s, za_s, sg_s, lm_s, yb_s, xbd_s,
                        *, t_len, rc, group):
    ones_bd = _head_ones()
    n_ch = t_len // rc
    rowi = lax.broadcasted_iota(jnp.int32, (rc, 1), 0)

    def shifted(refs, mus, carry, r0):
        zs, lasts = [], []
        for ref, mu, prev in zip(refs, mus, carry):
            p = ref[pl.ds(r0, rc), :].astype(F32)
            zs.append(p + mu[...] * (_shift_rows(p, prev, rowi) - p))
            lasts.append(p[rc - 1:rc, :])
        return zs, tuple(lasts)

    @pl.when(pl.program_id(1) == 0)
    def _():
        lora_refs = (pwd_ref, pad_ref, pgd_ref)

        def phase0(c, carry):
            r0 = pl.multiple_of(c * rc, rc)
            (zwd, zad, zgd), lasts = shifted(lora_refs, (muwd_ref, muad_ref, mugd_ref), carry, r0)
            rows = pl.ds(r0, rc)
            tw_s[rows, :] = jnp.tanh(zwd).astype(BF16)
            za_s[rows, :] = zad.astype(BF16)
            sg_s[rows, :] = jax.nn.sigmoid(zgd).astype(BF16)
            return lasts

        lax.fori_loop(0, n_ch, phase0, tuple(jnp.zeros((1, ref.shape[1]), F32) for ref in lora_refs))

    def phase1(c, carry):
        r0 = pl.multiple_of(c * rc, rc)
        (zr, zk, zv), lasts = shifted((pr_ref, pk_ref, pv_ref), (mur_ref, muk_ref, muv_ref), carry, r0)
        rows = pl.ds(r0, rc)
        lw, g, kk, k2, beta, bonus = _rwkv_prep(
            zr, zk, zv, tw_s[rows, :], za_s[rows, :], sg_s[rows, :],
            w0_ref[...], w2_ref[...], a0_ref[...], a2_ref[...], g2_ref[...],
            kkw_ref[...], kaw_ref[...], rkw_ref[...], ones_bd)
        r_s[rows, :] = zr
        lw_s[rows, :] = lw
        k_s[rows, :] = k2
        v_s[rows, :] = zv
        kap_s[rows, :] = kk
        bet_s[rows, :] = beta
        g_s[rows, :] = g
        bon_s[rows, :] = bonus
        return lasts

    lax.fori_loop(0, n_ch, phase1, tuple(jnp.zeros((1, LANES), F32) for _ in range(3)))

    c_len = CHUNK
    slot = c_len + LANES
    ti = lax.broadcasted_iota(jnp.int32, (c_len, LANES), 0)
    si = lax.broadcasted_iota(jnp.int32, (c_len, LANES), 1) % HEAD
    strict = ti > si
    incl = ti >= si
    lri = lax.broadcasted_iota(jnp.int32, (c_len, c_len), 0)
    lci = lax.broadcasted_iota(jnp.int32, (c_len, c_len), 1)
    l_cum = (lri >= lci).astype(BF16)
    lane_head = lax.broadcasted_iota(jnp.int32, (1, LANES), 1) // HEAD
    lane_head2 = jnp.concatenate([lane_head, lane_head], axis=1)
    bri = lax.broadcasted_iota(jnp.int32, (LANES, LANES), 0)
    bci = lax.broadcasted_iota(jnp.int32, (LANES, LANES), 1)
    same_head = (bri // HEAD) == (bci // HEAD)
    same_head2 = jnp.concatenate([same_head, same_head], axis=1)
    eye = bri == bci
    zc = jnp.zeros((c_len, LANES), F32)
    nt_dims = (((1,), (1,)), ((), ()))

    def precompute(gi, between_stages):
        chunks = [gi * group + cc for cc in range(group)]
        rows = [pl.ds(pl.multiple_of(c * c_len, c_len), c_len) for c in chunks]
        lws = [lw_s[rw, :] for rw in rows]
        gcums = [_dot_exact_lhs(l_cum, lw) for lw in lws]
        g_ends = [g[c_len - 1:c_len, :] for g in gcums]
        kts = [kap_s[rw, :] * jnp.exp(g - lw) for rw, g, lw in zip(rows, gcums, lws)]
        rts = [r_s[rw, :] * jnp.exp(g) for rw, g in zip(rows, gcums)]
        e_negs = [jnp.exp(-g) for g in gcums]
        lhss = [jnp.concatenate([kt, rt], axis=0) for kt, rt in zip(kts, rts)]
        o_bs = [lax.dot_general(lhs, _head_split(bet_s[rw, :] * en, lane_head), nt_dims, preferred_element_type=F32)
                for lhs, rw, en in zip(lhss, rows, e_negs)]
        o_ks = [lax.dot_general(lhs, _head_split(k_s[rw, :] * en, lane_head), nt_dims, preferred_element_type=F32)
                for lhs, rw, en in zip(lhss, rows, e_negs)]
        n_mats = [jnp.where(strict, o[0:c_len], 0.0) for o in o_bs]
        v_bds = [_head_split(v_s[rw, :], lane_head) for rw in rows]
        avs = [jnp.dot(jnp.where(strict, o[0:c_len], 0.0), v_bd, preferred_element_type=F32)
               for o, v_bd in zip(o_ks, v_bds)]
        rhss = [jnp.concatenate([kt, av], axis=1) for kt, av in zip(kts, avs)]
        a_rs = [jnp.concatenate([jnp.where(incl, -ob[c_len:], 0.0), jnp.where(incl, ok[c_len:], 0.0)], axis=1)
                for ob, ok in zip(o_bs, o_ks)]
        e_hats = [jnp.exp(ge - g) for ge, g in zip(g_ends, gcums)]
        bk_ts = [jnp.concatenate([bet_s[rw, :] * eh, k_s[rw, :] * eh], axis=0).T for rw, eh in zip(rows, e_hats)]
        xs = _solve_unit_lower(n_mats, rhss, lane_head2, xbd_s, between_stages)
        wus = [-x for x in xs]
        tops = [jnp.dot(a_r, jnp.concatenate([xbd_s[i],
                                              jnp.concatenate([jnp.zeros((LANES, LANES), F32), v_bd], axis=1)],
                                             axis=0), preferred_element_type=F32)
                for i, (a_r, v_bd) in enumerate(zip(a_rs, v_bds))]
        bots = [jnp.dot(bk_t, jnp.concatenate([wu, jnp.concatenate([zc, v_s[rw, :]], axis=1)], axis=0),
                        preferred_element_type=F32)
                for bk_t, wu, rw in zip(bk_ts, wus, rows)]
        for c, top, bot, rt, ge in zip(chunks, tops, bots, rts, g_ends):
            bot = jnp.where(same_head2, bot, 0.0)
            base = pl.multiple_of((c + group) * slot, SUBLANES)
            lm_s[pl.ds(base, c_len), :] = top[:, 0:LANES] + rt
            lm_s[pl.ds(base + c_len, LANES), :] = bot[:, 0:LANES] + jnp.where(eye, jnp.exp(ge), 0.0)
            yb_s[pl.ds(base, c_len), :] = top[:, LANES:]
            yb_s[pl.ds(base + c_len, LANES), :] = bot[:, LANES:]

    def advance(c, h_bd):
        base = pl.multiple_of((c + group) * slot, SUBLANES)
        res = _dot_3pass(lm_s[pl.ds(base, slot), :], h_bd) + yb_s[pl.ds(base, slot), :]
        y_s[pl.ds(pl.multiple_of((c + group) * c_len, c_len), c_len), :] = res[0:c_len]
        return res[c_len:]

    def phase3(c, carry):
        r0 = pl.multiple_of(c * rc, rc)
        o = _rwkv_post(y_s[pl.ds(r0 + group * c_len, rc), :], bon_s[pl.ds(r0, rc), :], g_s[pl.ds(r0, rc), :],
                       gnw_ref[...], gnb_ref[...], ones_bd)
        ob_ref[pl.ds(r0, rc), :] = o.astype(ob_ref.dtype)
        return carry

    n_stages = c_len // SUBLANES
    per_stage = -(-group // n_stages)
    n_groups = t_len // (c_len * group)
    lm_s[0:group * slot, :] = jnp.zeros((group * slot, LANES), F32)
    yb_s[0:group * slot, :] = jnp.zeros((group * slot, LANES), F32)

    def fused(gi, h):
        state = [h]

        def between_stages(blk):
            for cc in range(blk * per_stage, min((blk + 1) * per_stage, group)):
                state[0] = advance((gi - 1) * group + cc, state[0])

        precompute(gi, between_stages)
        return state[0]

    h_bd = lax.fori_loop(0, n_groups, fused, jnp.zeros((LANES, LANES), F32))
    n_ready = min(group, ((n_groups - 1) * group * c_len) // rc)
    for blk in range(group):
        h_bd = advance(jnp.int32((n_groups - 1) * group + blk), h_bd)
        if blk < n_ready:
            phase3(jnp.int32(blk), 0)
    s_bd = h_bd.T
    s_ref[0, 0] = s_bd[0:HEAD, 0:HEAD]
    s_ref[0, 1] = s_bd[HEAD:, HEAD:]
    lax.fori_loop(n_ready, n_ch, phase3, 0)


def _rwkv_specs(row_block, rows, q_rows, lay):
    pw, pa, pg, nb = lay["pw"], lay["pa"], lay["pg"], lay["nb"]

    def im(col_fn, on_rows=False):
        if row_block is None:
            return lambda b, hp: (b if on_rows else 0, col_fn(hp))
        return lambda hp: (row_block if on_rows else 0, col_fn(hp))

    def triple(n_rows, on_rows, c_r, c_k, c_v, c_wd, c_ad, c_gd):
        return [
            pl.BlockSpec((n_rows, LANES), im(lambda hp: c_r + hp, on_rows)),
            pl.BlockSpec((n_rows, LANES), im(lambda hp: c_k + hp, on_rows)),
            pl.BlockSpec((n_rows, LANES), im(lambda hp: c_v + hp, on_rows)),
            pl.BlockSpec((n_rows, pw * LANES), im(lambda hp: c_wd // pw, on_rows)),
            pl.BlockSpec((n_rows, pa * LANES), im(lambda hp: c_ad // pa, on_rows)),
            pl.BlockSpec((n_rows, pg * LANES), im(lambda hp: c_gd // pg, on_rows)),
        ]

    specs = triple(rows, True, lay["r"], lay["k"], lay["v"], lay["wd"], lay["ad"], lay["gd"])
    if q_rows:
        specs += triple(q_rows, False, 0, nb, 2 * nb, 0, pw, pw + pa)
    specs += triple(1, False, 0, nb, 2 * nb, 0, pw, pw + pa)
    per_pair = im(lambda hp: hp)
    specs += [
        pl.BlockSpec((1, LANES), per_pair),
        pl.BlockSpec((pw * LANES, LANES), per_pair),
        pl.BlockSpec((1, LANES), per_pair),
        pl.BlockSpec((pa * LANES, LANES), per_pair),
        pl.BlockSpec((pg * LANES, LANES), per_pair),
        pl.BlockSpec((1, LANES), per_pair),
        pl.BlockSpec((1, LANES), per_pair),
        pl.BlockSpec((1, LANES), per_pair),
        pl.BlockSpec((1, LANES), per_pair),
        pl.BlockSpec((1, LANES), per_pair),
    ]
    return specs


def _rwkv_prompt(p_main, p_tail, mu_rkv, mu_lora, params, *, bp, tp, d, lay):
    n_hp = d // LANES
    pw, pa, pg = lay["pw"], lay["pa"], lay["pg"]
    rc = 256 if tp % 256 == 0 else tp
    n_heads = d // HEAD
    n_chunks = tp // CHUNK
    group = next(g for g in (8, 4, 2, 1) if n_chunks % g == 0)
    kern = functools.partial(_rwkv_prompt_kernel, t_len=tp, rc=rc, group=group)
    vm = pltpu.VMEM((tp, LANES), F32)
    return pl.pallas_call(
        kern,
        grid=(bp, n_hp),
        in_specs=_rwkv_specs(None, tp, 0, lay),
        out_specs=[
            pl.BlockSpec((tp, LANES), lambda b, hp: (b, hp)),
            pl.BlockSpec((1, 2, HEAD, HEAD), lambda b, hp: (b, hp, 0, 0)),
        ],
        out_shape=[
            jax.ShapeDtypeStruct((bp * tp, d), BF16),
            jax.ShapeDtypeStruct((bp, n_heads, HEAD, HEAD), F32),
        ],
        scratch_shapes=[vm] * 8 + [
            pltpu.VMEM((tp + group * CHUNK, LANES), F32),
            pltpu.VMEM((tp, pw * LANES), BF16),
            pltpu.VMEM((tp, pa * LANES), BF16),
            pltpu.VMEM((tp, pg * LANES), BF16),
            pltpu.VMEM(((n_chunks + group) * (CHUNK + LANES), LANES), F32),
            pltpu.VMEM(((n_chunks + group) * (CHUNK + LANES), LANES), F32),
            pltpu.VMEM((group, 2 * CHUNK, 2 * LANES), F32),
        ],
        compiler_params=_cparams(2),
        name="rwkv_prompt",
    )(p_main, p_main, p_main, p_tail, p_tail, p_tail,
      mu_rkv, mu_rkv, mu_rkv, mu_lora, mu_lora, mu_lora, *params)


def _rwkv_sample_kernel(pr_ref, pk_ref, pv_ref, pwd_ref, pad_ref, pgd_ref,
                        qr_ref, qk_ref, qv_ref, qwd_ref, qad_ref, qgd_ref,
                        mur_ref, muk_ref, muv_ref, muwd_ref, muad_ref, mugd_ref,
                        w0_ref, w2_ref, a0_ref, a2_ref, g2_ref, kkw_ref, kaw_ref, rkw_ref, gnw_ref, gnb_ref,
                        st_ref, ob_ref, so_ref, y_s, tv_s, *, ts, bs):
    ones_bd = _head_ones()
    refs = (pr_ref, pk_ref, pv_ref, pwd_ref, pad_ref, pgd_ref)
    prevs = (qr_ref, qk_ref, qv_ref, qwd_ref, qad_ref, qgd_ref)
    mus = (mur_ref, muk_ref, muv_ref, muwd_ref, muad_ref, mugd_ref)
    zs = []
    for ref, q, mu in zip(refs, prevs, mus):
        p = ref[...].astype(F32)
        pp = jnp.concatenate([q[...].astype(F32), p[0:(ts - 1) * bs, :]], axis=0)
        zs.append(p + mu[...] * (pp - p))
    zr, zk, zv, zwd, zad, zgd = zs
    lw, g, kk, k2, beta, bonus = _rwkv_prep(
        zr, zk, zv, jnp.tanh(zwd).astype(BF16), zad.astype(BF16), jax.nn.sigmoid(zgd).astype(BF16),
        w0_ref[...], w2_ref[...], a0_ref[...], a2_ref[...], g2_ref[...],
        kkw_ref[...], kaw_ref[...], rkw_ref[...], ones_bd)
    w_dec = jnp.exp(lw)

    for t in range(ts):
        rows = slice(t * bs, (t + 1) * bs)
        for kind, a in enumerate((w_dec, kk, beta, k2, zr, zv)):
            tv_s[t, kind] = a[rows, :].T

    def total(p):
        p = p + pltpu.roll(p, 4, 0)
        p = p + pltpu.roll(p, 2, 0)
        return p + pltpu.roll(p, 1, 0)

    groups = HEAD // SUBLANES
    subi = lax.broadcasted_iota(jnp.int32, (SUBLANES, bs), 0)

    def row_group(gidx, carry):
        h = gidx // groups
        i0 = (gidx % groups) * SUBLANES
        lo = pl.multiple_of(h * HEAD, HEAD)
        row0 = pl.multiple_of(gidx * SUBLANES, SUBLANES)
        y_tiles = [jnp.zeros((SUBLANES, bs), F32) for _ in range(ts)]
        for half in range(0, SUBLANES, _SAMPLE_ROWS):
            s_rows = [st_ref[h, i0 + half + i].reshape(groups, SUBLANES, bs) for i in range(_SAMPLE_ROWS)]
            for t in range(ts):
                wt, kkt, bt, kt, rt = (tv_s[t, kind, pl.ds(lo, HEAD), :].reshape(groups, SUBLANES, bs)
                                       for kind in range(5))
                vt8 = tv_s[t, 5, pl.ds(row0, SUBLANES), :]
                for i in range(_SAMPLE_ROWS):
                    s = s_rows[i]
                    sa = -total(jnp.sum(s * kkt, axis=0))
                    s = s * wt + sa[None] * bt + vt8[half + i:half + i + 1, :][None] * kt
                    s_rows[i] = s
                    y_tiles[t] = jnp.where(subi == half + i, total(jnp.sum(s * rt, axis=0)), y_tiles[t])
            for i in range(_SAMPLE_ROWS):
                so_ref[h, i0 + half + i] = s_rows[i].reshape(HEAD, bs)
        for t in range(ts):
            y_s[t, pl.ds(row0, SUBLANES), :] = y_tiles[t]
        return carry

    lax.fori_loop(0, 2 * groups, row_group, 0)
    for t in range(ts):
        yt = y_s[t].T
        rows = slice(t * bs, (t + 1) * bs)
        o = _rwkv_post(yt, bonus[rows], g[rows], gnw_ref[...], gnb_ref[...], ones_bd)
        ob_ref[rows, :] = o.astype(ob_ref.dtype)


def _rwkv_sample(p_main, p_tail, q_rkv, q_lora, mu_rkv, mu_lora, params, st, *, ts, bs, d, mp, lay):
    n_hp = d // LANES
    rows = ts * bs
    rblk = mp // rows
    in_specs = _rwkv_specs(rblk, rows, bs, lay)
    in_specs.append(pl.BlockSpec((2, HEAD, HEAD, bs), lambda hp: (hp, 0, 0, 0)))
    kern = functools.partial(_rwkv_sample_kernel, ts=ts, bs=bs)
    return pl.pallas_call(
        kern,
        grid=(n_hp,),
        in_specs=in_specs,
        out_specs=[
            pl.BlockSpec((rows, LANES), lambda hp: (0, hp)),
            pl.BlockSpec((2, HEAD, HEAD, bs), lambda hp: (hp, 0, 0, 0)),
        ],
        out_shape=[
            jax.ShapeDtypeStruct((rows, d), BF16),
            jax.ShapeDtypeStruct(st.shape, F32),
        ],
        scratch_shapes=[pltpu.VMEM((ts, LANES, bs), F32), pltpu.VMEM((ts, 6, LANES, bs), F32)],
        compiler_params=_cparams(1),
        name="rwkv_sample",
    )(p_main, p_main, p_main, p_tail, p_tail, p_tail, q_rkv, q_rkv, q_rkv, q_lora, q_lora, q_lora,
      mu_rkv, mu_rkv, mu_rkv, mu_lora, mu_lora, mu_lora, *params, st)


def _pad_cols(a, width):
    return jnp.pad(a, ((0, 0), (0, width - a.shape[1])))


def _pad_rows(a, height):
    return jnp.pad(a, ((0, height - a.shape[0]), (0, 0)))


def _tiles(d, d_ff):
    pick = lambda n, cands: next(c for c in cands if n % c == 0)
    return dict(
        tm=512, tm_wide=256,
        main=pick(5 * d, (2560, 1280, 1024, 512, 256, 128)),
        gates=pick(2 * d, (2048, 1024, 512, 256, 128)),
        prev=pick(d, (1024, 512, 256, 128)),
        up=pick(d_ff, (2048, 1024, 512, 256, 128)),
        down=pick(d, (512, 256, 128)),
    )


def kernel(x_prompt, x_sample, p_prompt, p_sample, state_rg_h, state_rg_conv, state_rwkv, state_shift,
           norm_mix, w_in, conv_w, conv_b, rg_wa, rg_ba, rg_wx, rg_bx, rg_lam, w_rg_o,
           mu_shift, rw_w0, rw_w2, rw_a0, rw_a2, rw_g2, rw_kk, rw_ka, rw_rk, rw_gn_w, rw_gn_b,
           w_rw_o, w_o, norm_ffn, w_up, w_down, norm_ple, w_ple_gate, w_ple, norm_f):
    bp, tp, d = x_prompt.shape
    bs, ts, _ = x_sample.shape
    depth = w_in.shape[0]
    n_heads, head = rw_rk.shape[1], rw_rk.shape[2]
    r_w, r_a, r_g = rw_w2.shape[1], rw_a2.shape[1], rw_g2.shape[1]
    d_ple = w_ple.shape[1]
    d_ff = w_up.shape[2]
    n_tap = conv_w.shape[1]
    assert head == HEAD and n_heads * HEAD == d and d % LANES == 0 and bs == LANES
    assert rg_wa.shape[2] == LANES and tp % CHUNK == 0 and n_tap == 4
    mp, ms = bp * tp, bs * ts
    nb = d // LANES
    pw, pa, pg = (_round_up(r, LANES) // LANES for r in (r_w, r_a, r_g))
    n_lora = (pw + pa + pg) * LANES
    lay = dict(u=0, g=nb, r=2 * nb, k=3 * nb, v=4 * nb, wd=0, ad=pw, gd=pw + pa, pw=pw, pa=pa, pg=pg, nb=nb)
    t = _tiles(d, d_ff)
    tm = t["tm"]
    assert mp % tm == 0 and ms % tm == 0 and mp % (ts * bs) == 0 and ms % t["tm_wide"] == 0
    assert lay["wd"] % pw == 0 and lay["ad"] % pa == 0 and lay["gd"] % pg == 0 and pw % pa == 0 and (pw + pa) % pg == 0
    assert (2 * d) % t["prev"] == 0 and (5 * d + r_w + r_a + r_g) % SUBLANES == 0
    mm = functools.partial(_mm, mp=mp, ms=ms)

    x = _Rows(x_prompt.reshape(mp, d), jnp.transpose(x_sample, (1, 0, 2)).reshape(ms, d))
    hp_l, cp_l, sp_l, xp_l, hs_l, cs_l, ss_l, xs_l = [], [], [], [], [], [], [], []
    y_p = y_s = None
    for i in range(depth):
        o_rw = 2 * d
        o_lora = o_rw + 3 * d
        o_g = o_lora + r_w + r_a + r_g
        wt = jnp.swapaxes(w_in[i], 0, 1)

        def lora_rows(a):
            return jnp.concatenate([
                _pad_rows(a[o_lora:o_lora + r_w], pw * LANES),
                _pad_rows(a[o_lora + r_w:o_lora + r_w + r_a], pa * LANES),
                _pad_rows(a[o_lora + r_w + r_a:o_g], pg * LANES)], axis=0)

        wt_lora = lora_rows(wt)
        mu_all = jnp.concatenate([jnp.zeros((o_rw, 1), F32), mu_shift[i][:, None], jnp.zeros((2 * d, 1), F32)], axis=0)
        mu_rkv = mu_shift[i][None, :3 * d]
        mu_lora = lora_rows(mu_all).reshape(1, n_lora)
        rw_params = (
            rw_w0[i][None], _pad_rows(rw_w2[i], pw * LANES).astype(BF16),
            rw_a0[i][None], _pad_rows(rw_a2[i], pa * LANES).astype(BF16),
            _pad_rows(rw_g2[i], pg * LANES).astype(BF16),
            rw_kk[i][None], rw_ka[i][None], rw_rk[i].reshape(1, d), rw_gn_w[i][None], rw_gn_b[i][None])
        rg_params = (conv_w[i], conv_b[i][None], rg_wa[i].astype(BF16), rg_ba[i][:, None, :],
                     rg_wx[i].astype(BF16), rg_bx[i][:, None, :], rg_lam[i][None])

        xn = _Rows(_rmsnorm(x, norm_mix[i], BF16, tm, mp))
        xa = x.arrays if len(x.arrays) == 2 else (x.arrays[0][:mp], x.arrays[0][mp:])
        x_last = jnp.concatenate([xa[0].reshape(bp, tp, d)[:, -1], xa[1][(ts - 1) * bs:]], axis=0)
        n_last = _round_up(bp + bs, SUBLANES)
        xn_last = _rmsnorm(_Rows(_pad_rows(x_last, n_last)), norm_mix[i], F32, n_last, n_last)
        (p_main,) = mm([xn], [wt], [], _epi_plain, [(F32, False)], tm=tm, tn=t["main"], n_out=5 * d,
                       cast_w=True, w_transposed=True, single_buffer_w=True, name="in_proj")
        (p_gate,) = mm([xn], [wt], [], _epi_plain, [(BF16, False)], tm=tm, tn=t["gates"], n_out=2 * d,
                       w_row_start=o_g, cast_w=True, w_transposed=True, single_buffer_w=True, name="in_proj_gates")
        (p_lora,) = mm([xn], [wt_lora], [], _epi_plain, [(BF16, False)], tm=tm, tn=n_lora, n_out=n_lora,
                       cast_w=True, w_transposed=True, name="in_proj_lora")
        xprev = _Rows(state_shift[i])
        (q_rkv,) = _mm([xprev], [wt], [], _epi_plain, [(F32, False)], mp=bs, ms=0, tm=bs, tn=t["prev"],
                       n_out=3 * d, w_col_off=o_rw // t["prev"], cast_w=True, w_transposed=True, name="prev_proj")
        (q_lora,) = _mm([xprev], [wt_lora], [], _epi_plain, [(BF16, False)], mp=bs, ms=0, tm=bs, tn=n_lora,
                        n_out=n_lora, cast_w=True, w_transposed=True, name="prev_proj_lora")

        cwid = 2 * LANES if nb % 2 == 0 else LANES
        hg_p, h_p, c_p = _rg_prompt(p_main, *rg_params, bp=bp, tp=tp, d=d, cb_u=lay["u"], cb_g=lay["g"], cwid=cwid)
        cbuf_t = jnp.transpose(state_rg_conv[i], (1, 0, 2))
        hg_s, h_s, c_s = _rg_sample(p_main, cbuf_t, state_rg_h[i], *rg_params, ts=ts, bs=bs, d=d, mp=mp,
                                    cb_u=lay["u"], cb_g=lay["g"], cwid=cwid)

        ob_p, s_p = _rwkv_prompt(p_main, p_lora, mu_rkv, mu_lora, rw_params, bp=bp, tp=tp, d=d, lay=lay)
        st = jnp.transpose(state_rwkv[i], (1, 2, 3, 0))
        ob_s, st_new = _rwkv_sample(p_main, p_lora, q_rkv, q_lora, mu_rkv, mu_lora, rw_params, st,
                                    ts=ts, bs=bs, d=d, mp=mp, lay=lay)
        s_s = jnp.transpose(st_new, (3, 0, 1, 2))

        tw = t["tm_wide"]
        (merged,) = mm([_Rows(hg_p, hg_s), _Rows(ob_p, ob_s)], [w_rg_o[i].astype(BF16), w_rw_o[i].astype(BF16)],
                       [("tile", _Rows(p_gate), 0), ("tile", _Rows(p_gate), 1)],
                       _epi_merge, [(BF16, False)], tm=tw, tn=d, n_out=d, single_buffer_w=True, name="merge")
        x1, xn2 = mm([_Rows(merged)], [w_o[i]], [("tile", x, 0), ("row", norm_ffn[i][None], 0)],
                     _epi_residual_norm, [(F32, False), (BF16, False)], tm=tw, tn=d, n_out=d, cast_w=True,
                     single_buffer_w=True, name="out_proj")
        (hf,) = mm([_Rows(xn2)], [w_up[i]], [], _epi_relu2, [(BF16, False)], tm=tm, tn=t["up"], n_out=d_ff,
                   cast_w=True, single_buffer_w=True, name="mlp_up")
        (x2,) = mm([_Rows(hf)], [w_down[i].astype(BF16)], [("tile", _Rows(x1), 0)], _epi_residual, [(F32, False)],
                   tm=tm, tn=t["down"], n_out=d, name="mlp_down")
        p_rows = _Rows(p_prompt[i].reshape(mp, d_ple), jnp.transpose(p_sample[i], (1, 0, 2)).reshape(ms, d_ple))
        ple_ws = [w_ple_gate[i], w_ple[i]]
        g_ple = ("row", norm_ple[i][None], 0)
        if i == depth - 1:
            y_p, y_s = mm([("norm_of", 0, 2), p_rows], ple_ws,
                          [("tile", _Rows(x2), 0), ("row", norm_f[None], 0), g_ple],
                          _epi_ple_final, [(F32, True)], tm=tw, tn=d, n_out=d, cast_w=True, single_buffer_w=True,
                          name="ple_final")
        else:
            x = _Rows(*mm([("norm_of", 0, 1), p_rows], ple_ws, [("tile", _Rows(x2), 0), g_ple], _epi_ple,
                          [(F32, False)], tm=tw, tn=d, n_out=d, cast_w=True, single_buffer_w=True, name="ple"))

        hp_l.append(h_p[:, 0])
        cp_l.append(c_p[:, SUBLANES - (n_tap - 1):])
        sp_l.append(s_p)
        xp_l.append(xn_last[:bp])
        hs_l.append(h_s)
        cs_l.append(jnp.transpose(c_s, (1, 0, 2)))
        ss_l.append(s_s)
        xs_l.append(xn_last[bp:bp + bs])

    y_prompt = y_p.reshape(bp, tp, d)
    y_sample = jnp.transpose(y_s.reshape(ts, bs, d), (1, 0, 2))
    return (y_prompt, y_sample,
            jnp.stack(hp_l), jnp.stack(cp_l), jnp.stack(sp_l), jnp.stack(xp_l),
            jnp.stack(hs_l), jnp.stack(cs_l), jnp.stack(ss_l), jnp.stack(xs_l))
```

```python
import functools

import jax
import jax.numpy as jnp
from jax import lax
from jax.experimental import pallas as pl
from jax.experimental.pallas import tpu as pltpu

F32 = jnp.float32
BF16 = jnp.bfloat16
LANES = 128
SUBLANES = 8
HEAD = 64
CHUNK = 64
_SAMPLE_ROWS = 4
EPS = 1e-6
GN_EPS = 64e-5
RG_C = 8.0
VMEM_LIMIT = 56 * 1024 * 1024


def _cparams(n_grid, vmem=VMEM_LIMIT):
    return pltpu.CompilerParams(dimension_semantics=("arbitrary",) * n_grid, vmem_limit_bytes=vmem)


def _round_up(x, m):
    return (x + m - 1) // m * m


def _rms(x, g):
    ms = jnp.mean(x * x, axis=-1, keepdims=True)
    return (x * lax.rsqrt(ms + EPS)) * g


class _Rows:
    def __init__(self, *arrays):
        self.arrays = arrays

    @property
    def width(self):
        return self.arrays[0].shape[1]

    def specs(self, tm, n_p, col):
        if len(self.arrays) == 1:
            return [lambda wid: pl.BlockSpec((tm, wid), lambda n, i: (i, col(n)))]
        return [lambda wid: pl.BlockSpec((tm, wid), lambda n, i: (jnp.minimum(i, n_p - 1), col(n))),
                lambda wid: pl.BlockSpec((tm, wid), lambda n, i: (jnp.maximum(i - n_p, 0), col(n)))]


def _load_rows(refs, is_prompt):
    if len(refs) == 1:
        return refs[0][...]
    return jnp.where(is_prompt, refs[0][...], refs[1][...])


def _rmsnorm_kernel(*refs, n_src, n_p):
    g_ref, o_ref = refs[n_src], refs[n_src + 1]
    x = _load_rows(refs[:n_src], pl.program_id(1) < n_p)
    o_ref[...] = _rms(x, g_ref[...]).astype(o_ref.dtype)


def _rmsnorm(x, g, out_dtype, tm, mp):
    d = x.width
    m = sum(a.shape[0] for a in x.arrays)
    n_p = mp // tm
    in_specs = [mk(d) for mk in x.specs(tm, n_p, lambda n: 0)] + [pl.BlockSpec((1, d), lambda n, i: (0, 0))]
    return pl.pallas_call(
        functools.partial(_rmsnorm_kernel, n_src=len(x.arrays), n_p=n_p),
        grid=(1, m // tm),
        in_specs=in_specs,
        out_specs=pl.BlockSpec((tm, d), lambda n, i: (i, 0)),
        out_shape=jax.ShapeDtypeStruct((m, d), out_dtype),
        compiler_params=_cparams(2),
        name="rmsnorm",
    )(*x.arrays, g.reshape(1, d))


def _mm_kernel(*refs, x_counts, x_norm, n_w, cast_w, w_transposed, extra_counts, out_counts, n_p, epilogue):
    pos = 0
    x_refs = []
    for cnt in x_counts:
        x_refs.append(refs[pos:pos + cnt])
        pos += cnt
    w_refs = refs[pos:pos + n_w]
    pos += n_w
    e_refs = []
    for cnt in extra_counts:
        e_refs.append(refs[pos:pos + cnt])
        pos += cnt
    o_refs = []
    for cnt in out_counts:
        o_refs.append(refs[pos:pos + cnt])
        pos += cnt
    w_scratch = refs[pos:]
    i = pl.program_id(1)
    is_prompt = i < n_p
    if cast_w:
        @pl.when(i == 0)
        def _():
            for w_ref, s_ref in zip(w_refs, w_scratch):
                s_ref[...] = w_ref[...].astype(BF16)
        ws = [s[...] for s in w_scratch]
    else:
        ws = [w[...] for w in w_refs]
    dims = (((1,), (1 if w_transposed else 0,)), ((), ()))
    e_vals = [_load_rows(er, is_prompt) for er in e_refs]
    x_vals = []
    for idx, xr in enumerate(x_refs):
        if xr:
            x_vals.append(_load_rows(xr, is_prompt).astype(BF16))
        else:
            e_idx, g_idx = x_norm[idx]
            x_vals.append(_rms(e_vals[e_idx], e_vals[g_idx]).astype(BF16))
    accs = [lax.dot_general(x, w, dims, preferred_element_type=F32) for x, w in zip(x_vals, ws)]
    outs = epilogue(accs, e_vals)
    for refs_o, tile in zip(o_refs, outs):
        if len(refs_o) == 1:
            refs_o[0][...] = tile.astype(refs_o[0].dtype)
        else:
            @pl.when(is_prompt)
            def _():
                refs_o[0][...] = tile.astype(refs_o[0].dtype)

            @pl.when(jnp.logical_not(is_prompt))
            def _():
                refs_o[1][...] = tile.astype(refs_o[1].dtype)


def _mm(xs, ws, extras, epilogue, outs, *, mp, ms, tm, tn, n_out, w_col_off=0, w_row_start=None, cast_w=False,
        w_transposed=False, single_buffer_w=False, name="mm"):
    n_p = mp // tm
    m = mp + ms
    grid = (n_out // tn, m // tm)
    in_specs, args, x_counts, extra_counts, x_norm = [], [], [], [], {}
    for idx, x in enumerate(xs):
        if isinstance(x, tuple):
            x_norm[idx] = (x[1], x[2])
            x_counts.append(0)
            continue
        in_specs += [mk(x.width) for mk in x.specs(tm, n_p, lambda n: 0)]
        args += list(x.arrays)
        x_counts.append(len(x.arrays))
    w_mode = dict(pipeline_mode=pl.Buffered(1)) if single_buffer_w else {}
    if w_transposed and w_row_start is not None:
        w_blocks = [(tn, w.shape[1]) for w in ws]
        in_specs += [pl.BlockSpec((pl.Element(tn), pl.Element(w.shape[1])),
                                  lambda n, i: (pl.multiple_of(w_row_start + n * tn, SUBLANES), 0), **w_mode)
                     for w in ws]
    elif w_transposed:
        w_blocks = [(tn, w.shape[1]) for w in ws]
        in_specs += [pl.BlockSpec(blk, lambda n, i: (n + w_col_off, 0), **w_mode) for blk in w_blocks]
    else:
        w_blocks = [(w.shape[0], tn) for w in ws]
        in_specs += [pl.BlockSpec(blk, lambda n, i: (0, n + w_col_off), **w_mode) for blk in w_blocks]
    args += list(ws)
    for kind, src, off in extras:
        if kind == "tile":
            in_specs += [mk(tn) for mk in src.specs(tm, n_p, lambda n, off=off: n + off)]
            args += list(src.arrays)
            extra_counts.append(len(src.arrays))
        elif kind == "row":
            in_specs.append(pl.BlockSpec((1, tn), lambda n, i, off=off: (0, n + off)))
            args.append(src)
            extra_counts.append(1)
        else:
            raise ValueError(kind)
    out_specs, out_shape, out_counts = [], [], []
    for dtype, split in outs:
        if split:
            out_specs += [pl.BlockSpec((tm, tn), lambda n, i: (jnp.minimum(i, n_p - 1), n)),
                          pl.BlockSpec((tm, tn), lambda n, i: (jnp.maximum(i - n_p, 0), n))]
            out_shape += [jax.ShapeDtypeStruct((mp, n_out), dtype), jax.ShapeDtypeStruct((ms, n_out), dtype)]
            out_counts.append(2)
        else:
            out_specs.append(pl.BlockSpec((tm, tn), lambda n, i: (i, n)))
            out_shape.append(jax.ShapeDtypeStruct((m, n_out), dtype))
            out_counts.append(1)
    kern = functools.partial(_mm_kernel, x_counts=tuple(x_counts), x_norm=x_norm, n_w=len(ws), cast_w=cast_w,
                             w_transposed=w_transposed, extra_counts=tuple(extra_counts),
                             out_counts=tuple(out_counts), n_p=n_p, epilogue=epilogue)
    scratch = [pltpu.VMEM(blk, BF16) for blk in w_blocks] if cast_w else []
    res = pl.pallas_call(
        kern,
        grid=grid,
        in_specs=in_specs,
        out_specs=out_specs,
        out_shape=out_shape,
        scratch_shapes=scratch,
        compiler_params=_cparams(2),
        name=name,
    )(*args)
    return res


def _epi_plain(accs, extras):
    return (accs[0],)


def _epi_residual(accs, extras):
    return (extras[0] + accs[0],)


def _epi_merge(accs, extras):
    ga, gb = extras
    return (jax.nn.sigmoid(ga.astype(F32)) * accs[0] + jax.nn.sigmoid(gb.astype(F32)) * accs[1],)


def _epi_residual_norm(accs, extras):
    x = extras[0] + accs[0]
    return x, _rms(x, extras[1])


def _epi_relu2(accs, extras):
    h = jnp.maximum(accs[0], 0.0)
    return (h * h,)


def _epi_ple(accs, extras):
    return (extras[0] + jax.nn.sigmoid(accs[0]) * accs[1],)


def _epi_ple_final(accs, extras):
    return (_rms(extras[0] + jax.nn.sigmoid(accs[0]) * accs[1], extras[1]),)


def _softplus(x):
    return jnp.maximum(x, 0.0) + jnp.log1p(jnp.exp(-jnp.abs(x)))


def _split_bf16(x, parts):
    out = []
    rem = x
    for _ in range(parts):
        p = rem.astype(BF16)
        out.append(p)
        rem = rem - p.astype(F32)
    return out


def _dot_exact_lhs(a_bf16, x, parts=3):
    acc = None
    for p in _split_bf16(x, parts):
        t = jnp.dot(a_bf16, p, preferred_element_type=F32)
        acc = t if acc is None else acc + t
    return acc


def _head_sum(x, ones_bd):
    return jnp.dot(x.astype(BF16), ones_bd, preferred_element_type=F32)


def _sigmoid(x):
    return 0.5 * jnp.tanh(0.5 * x) + 0.5


def _dot_3pass(a, b):
    a_hi, a_lo = _split_bf16(a, 2)
    b_hi, b_lo = _split_bf16(b, 2)
    return (jnp.dot(a_hi, b_hi, preferred_element_type=F32)
            + jnp.dot(a_hi, b_lo, preferred_element_type=F32)
            + jnp.dot(a_lo, b_hi, preferred_element_type=F32))


def _head_ones():
    r = lax.broadcasted_iota(jnp.int32, (LANES, LANES), 0) // HEAD
    c = lax.broadcasted_iota(jnp.int32, (LANES, LANES), 1) // HEAD
    return (r == c).astype(BF16)


def _rwkv_prep(zr, zk, zv, tw, za, sg, w0, w2, a0, a2, g2, kkw, kaw, rkw, ones_bd):
    wlog = -_softplus(-(w0 + jnp.dot(tw, w2, preferred_element_type=F32))) - 0.5
    lw = -jnp.exp(wlog)
    a = _sigmoid(a0 + jnp.dot(za, a2, preferred_element_type=F32))
    g = jnp.dot(sg, g2, preferred_element_type=F32)
    kk = zk * kkw
    ss = _head_sum(kk * kk, ones_bd)
    kk = kk * lax.rsqrt(jnp.maximum(ss, 1e-24))
    k2 = zk * (1.0 + (a - 1.0) * kaw)
    beta = kk * a
    bonus = _head_sum(zr * k2 * rkw, ones_bd) * zv
    return lw, g, kk, k2, beta, bonus


def _rwkv_post(y, bonus, g, gnw, gnb, ones_bd):
    mu = _head_sum(y, ones_bd) * (1.0 / HEAD)
    d = y - mu
    var = _head_sum(d * d, ones_bd) * (1.0 / HEAD)
    yn = (d * lax.rsqrt(var + GN_EPS)) * gnw + gnb
    return (yn + bonus) * g


def _rg_gates(xc, wa, ba, wx, bx, sp):
    xcb = xc.astype(BF16)
    r = _sigmoid(jnp.dot(xcb, wa, preferred_element_type=F32) + ba)
    i = _sigmoid(jnp.dot(xcb, wx, preferred_element_type=F32) + bx)
    log_a = (-RG_C * r) * sp
    a = jnp.exp(log_a)
    mult = jnp.sqrt(-jnp.tanh(log_a) * (a * a + 1.0))
    return a, mult, i * xc


def _rg_prompt_kernel(u_ref, gt_ref, cw_ref, cb_ref, wa_ref, ba_ref, wx_ref, bx_ref, lam_ref,
                      hg_ref, hl_ref, cs_ref, us_ref, a_ref, b_ref, *, t_len, rc):
    cwid = u_ref.shape[1]
    nb = cwid // LANES
    us_ref[0:SUBLANES, :] = jnp.zeros((SUBLANES, cwid), F32)
    us_ref[SUBLANES:, :] = u_ref[...]
    sp = _softplus(-lam_ref[...])
    w0 = cw_ref[0:1, :]
    w1 = cw_ref[1:2, :]
    w2 = cw_ref[2:3, :]
    w3 = cw_ref[3:4, :]
    cb = cb_ref[...]
    n_ch = t_len // rc

    def phase1(c, carry):
        r0 = pl.multiple_of(c * rc, rc)
        e = us_ref[pl.ds(r0, rc + SUBLANES), :]
        u0 = e[SUBLANES:]
        u1 = pltpu.roll(e, 1, 0)[SUBLANES:]
        u2 = pltpu.roll(e, 2, 0)[SUBLANES:]
        u3 = pltpu.roll(e, 3, 0)[SUBLANES:]
        xc = cb + (u3 * w0 + u2 * w1 + u1 * w2 + u0 * w3)
        row = lax.broadcasted_iota(jnp.int32, (rc, LANES), 0) + r0
        for n in range(nb):
            sl = slice(n * LANES, (n + 1) * LANES)
            a, mult, ix = _rg_gates(xc[:, sl], wa_ref[n], ba_ref[n], wx_ref[n], bx_ref[n], sp[:, sl])
            mult = jnp.where(row == 0, 1.0, mult)
            a_ref[n, pl.ds(r0, rc), :] = a
            b_ref[n, pl.ds(r0, rc), :] = mult * ix
        return carry

    lax.fori_loop(0, n_ch, phase1, 0)

    rowi = lax.broadcasted_iota(jnp.int32, (SUBLANES, LANES), 0)

    def scan(i, h_prev):
        r0 = pl.multiple_of(i * SUBLANES, SUBLANES)
        out = []
        for n in range(nb):
            a = a_ref[n, pl.ds(r0, SUBLANES), :]
            b = b_ref[n, pl.ds(r0, SUBLANES), :]
            for d in (1, 2, 4):
                a_sh = jnp.where(rowi < d, 1.0, pltpu.roll(a, d, 0))
                b_sh = jnp.where(rowi < d, 0.0, pltpu.roll(b, d, 0))
                b = a * b_sh + b
                a = a * a_sh
            h = b + a * h_prev[n]
            b_ref[n, pl.ds(r0, SUBLANES), :] = h
            out.append(h[SUBLANES - 1:SUBLANES, :])
        return tuple(out)

    h_last = lax.fori_loop(0, t_len // SUBLANES, scan, tuple(jnp.zeros((1, LANES), F32) for _ in range(nb)),
                           unroll=4)
    hl_ref[0] = jnp.concatenate(list(h_last), axis=1)
    cs_ref[0] = u_ref[pl.ds(t_len - SUBLANES, SUBLANES), :]

    def phase3(c, carry):
        r0 = pl.multiple_of(c * rc, rc)
        for n in range(nb):
            sl = slice(n * LANES, (n + 1) * LANES)
            gate = gt_ref[pl.ds(r0, rc), sl]
            hg_ref[pl.ds(r0, rc), sl] = (b_ref[n, pl.ds(r0, rc), :] * jax.nn.gelu(gate)).astype(hg_ref.dtype)
        return carry

    lax.fori_loop(0, n_ch, phase3, 0)


def _rg_prompt(p, conv_w, conv_b, wa, ba, wx, bx, lam, *, bp, tp, d, cb_u, cb_g, cwid):
    nb = cwid // LANES
    nblk = d // cwid
    rc = 256 if tp % 256 == 0 else tp
    kern = functools.partial(_rg_prompt_kernel, t_len=tp, rc=rc)
    return pl.pallas_call(
        kern,
        grid=(bp, nblk),
        in_specs=[
            pl.BlockSpec((tp, cwid), lambda b, n: (b, cb_u * LANES // cwid + n)),
            pl.BlockSpec((tp, cwid), lambda b, n: (b, cb_g * LANES // cwid + n)),
            pl.BlockSpec((4, cwid), lambda b, n: (0, n)),
            pl.BlockSpec((1, cwid), lambda b, n: (0, n)),
            pl.BlockSpec((nb, LANES, LANES), lambda b, n: (n, 0, 0)),
            pl.BlockSpec((nb, 1, LANES), lambda b, n: (n, 0, 0)),
            pl.BlockSpec((nb, LANES, LANES), lambda b, n: (n, 0, 0)),
            pl.BlockSpec((nb, 1, LANES), lambda b, n: (n, 0, 0)),
            pl.BlockSpec((1, cwid), lambda b, n: (0, n)),
        ],
        out_specs=[
            pl.BlockSpec((tp, cwid), lambda b, n: (b, n)),
            pl.BlockSpec((1, 1, cwid), lambda b, n: (b, 0, n)),
            pl.BlockSpec((1, SUBLANES, cwid), lambda b, n: (b, 0, n)),
        ],
        out_shape=[
            jax.ShapeDtypeStruct((bp * tp, d), BF16),
            jax.ShapeDtypeStruct((bp, 1, d), F32),
            jax.ShapeDtypeStruct((bp, SUBLANES, d), F32),
        ],
        scratch_shapes=[
            pltpu.VMEM((tp + SUBLANES, cwid), F32),
            pltpu.VMEM((nb, tp, LANES), F32),
            pltpu.VMEM((nb, tp, LANES), F32),
        ],
        compiler_params=_cparams(2),
        name="rg_prompt",
    )(p, p, conv_w, conv_b, wa, ba, wx, bx, lam)


def _rg_sample_kernel(u_ref, gt_ref, cbuf_ref, h0_ref, cw_ref, cb_ref, wa_ref, ba_ref, wx_ref, bx_ref, lam_ref,
                      hg_ref, hl_ref, cs_ref, *, ts, bs):
    cwid = u_ref.shape[1]
    nb = cwid // LANES
    n_tap = cw_ref.shape[0]
    sp = _softplus(-lam_ref[...])
    cb = cb_ref[...]
    ext = [cbuf_ref[j] for j in range(n_tap - 1)] + [u_ref[t * bs:(t + 1) * bs, :] for t in range(ts)]
    for j in range(n_tap - 1):
        cs_ref[j] = ext[len(ext) - (n_tap - 1) + j]
    h = h0_ref[...]
    for t in range(ts):
        conv = ext[t] * cw_ref[0:1, :]
        for j in range(1, n_tap):
            conv = conv + ext[t + j] * cw_ref[j:j + 1, :]
        xc = cb + conv
        pieces = []
        for n in range(nb):
            sl = slice(n * LANES, (n + 1) * LANES)
            a, mult, ix = _rg_gates(xc[:, sl], wa_ref[n], ba_ref[n], wx_ref[n], bx_ref[n], sp[:, sl])
            pieces.append(a * h[:, sl] + mult * ix)
        h = pieces[0] if nb == 1 else jnp.concatenate(pieces, axis=1)
        gate = gt_ref[t * bs:(t + 1) * bs, :]
        hg_ref[t * bs:(t + 1) * bs, :] = (h * jax.nn.gelu(gate)).astype(hg_ref.dtype)
    hl_ref[...] = h


def _rg_sample(p, cbuf_t, h0, conv_w, conv_b, wa, ba, wx, bx, lam, *, ts, bs, d, mp, cb_u, cb_g, cwid):
    nb = cwid // LANES
    nblk = d // cwid
    rows = ts * bs
    rblk = mp // rows
    kern = functools.partial(_rg_sample_kernel, ts=ts, bs=bs)
    n_tap = conv_w.shape[0]
    return pl.pallas_call(
        kern,
        grid=(nblk,),
        in_specs=[
            pl.BlockSpec((rows, cwid), lambda n: (rblk, cb_u * LANES // cwid + n)),
            pl.BlockSpec((rows, cwid), lambda n: (rblk, cb_g * LANES // cwid + n)),
            pl.BlockSpec((n_tap - 1, bs, cwid), lambda n: (0, 0, n)),
            pl.BlockSpec((bs, cwid), lambda n: (0, n)),
            pl.BlockSpec((n_tap, cwid), lambda n: (0, n)),
            pl.BlockSpec((1, cwid), lambda n: (0, n)),
            pl.BlockSpec((nb, LANES, LANES), lambda n: (n, 0, 0)),
            pl.BlockSpec((nb, 1, LANES), lambda n: (n, 0, 0)),
            pl.BlockSpec((nb, LANES, LANES), lambda n: (n, 0, 0)),
            pl.BlockSpec((nb, 1, LANES), lambda n: (n, 0, 0)),
            pl.BlockSpec((1, cwid), lambda n: (0, n)),
        ],
        out_specs=[
            pl.BlockSpec((rows, cwid), lambda n: (0, n)),
            pl.BlockSpec((bs, cwid), lambda n: (0, n)),
            pl.BlockSpec((n_tap - 1, bs, cwid), lambda n: (0, 0, n)),
        ],
        out_shape=[
            jax.ShapeDtypeStruct((rows, d), BF16),
            jax.ShapeDtypeStruct((bs, d), F32),
            jax.ShapeDtypeStruct((n_tap - 1, bs, d), F32),
        ],
        compiler_params=_cparams(1),
        name="rg_sample",
    )(p, p, cbuf_t, h0, conv_w, conv_b, wa, ba, wx, bx, lam)


def _shift_rows(x, prev_row, rowi):
    return jnp.where(rowi == 0, prev_row, pltpu.roll(x, 1, 0))


def _head_split(z, lane_head):
    return jnp.concatenate([jnp.where(lane_head == 0, z, 0.0), jnp.where(lane_head == 1, z, 0.0)], axis=0)


def _solve_unit_lower(n_mats, rhss, lane_head, xbd_ref, between_stages=None):
    n_sys = len(n_mats)
    c = n_mats[0].shape[0]
    br = SUBLANES
    xbd_ref[...] = jnp.zeros(xbd_ref.shape, F32)
    done = [[] for _ in range(n_sys)]
    for blk in range(c // br):
        lo = blk * br
        rs = []
        for i in range(n_sys):
            r = rhss[i][lo:lo + br, :]
            if blk > 0:
                r = r - jnp.dot(n_mats[i][lo:lo + br, :], xbd_ref[i], preferred_element_type=F32)
            rs.append(r)
        if between_stages is not None:
            between_stages(blk)
        for j in range(br - 1):
            for i in range(n_sys):
                nrow = n_mats[i][lo:lo + br, :]
                mult = jnp.where(lane_head == 0, nrow[:, lo + j:lo + j + 1], nrow[:, HEAD + lo + j:HEAD + lo + j + 1])
                rs[i] = rs[i] - mult * rs[i][j:j + 1, :]
        for i in range(n_sys):
            done[i].append(rs[i])
            xbd_ref[i, lo:lo + br, :] = jnp.where(lane_head == 0, rs[i], 0.0)
            xbd_ref[i, c + lo:c + lo + br, :] = jnp.where(lane_head == 1, rs[i], 0.0)
    return [jnp.concatenate(b, axis=0) for b in done]


def _rwkv_prompt_kernel(pr_ref, pk_ref, pv_ref, pwd_ref, pad_ref, pgd_ref,
                        mur_ref, muk_ref, muv_ref, muwd_ref, muad_ref, mugd_ref,
                        w0_ref, w2_ref, a0_ref, a2_ref, g2_ref, kkw_ref, kaw_ref, rkw_ref, gnw_ref, gnb_ref,
                        ob_ref, s_ref,
                        r_s, lw_s, k_s, v_s, kap_s, bet_s, g_s, bon_s, y_s, tw_s, za_s, sg_s, lm_s, yb_s, xbd_s,
                        *, t_len, rc, group):
    ones_bd = _head_ones()
    n_ch = t_len // rc
    rowi = lax.broadcasted_iota(jnp.int32, (rc, 1), 0)

    def shifted(refs, mus, carry, r0):
        zs, lasts = [], []
        for ref, mu, prev in zip(refs, mus, carry):
            p = ref[pl.ds(r0, rc), :].astype(F32)
            zs.append(p + mu[...] * (_shift_rows(p, prev, rowi) - p))
            lasts.append(p[rc - 1:rc, :])
        return zs, tuple(lasts)

    @pl.when(pl.program_id(1) == 0)
    def _():
        lora_refs = (pwd_ref, pad_ref, pgd_ref)

        def phase0(c, carry):
            r0 = pl.multiple_of(c * rc, rc)
            (zwd, zad, zgd), lasts = shifted(lora_refs, (muwd_ref, muad_ref, mugd_ref), carry, r0)
            rows = pl.ds(r0, rc)
            tw_s[rows, :] = jnp.tanh(zwd).astype(BF16)
            za_s[rows, :] = zad.astype(BF16)
            sg_s[rows, :] = jax.nn.sigmoid(zgd).astype(BF16)
            return lasts

        lax.fori_loop(0, n_ch, phase0, tuple(jnp.zeros((1, ref.shape[1]), F32) for ref in lora_refs))

    def phase1(c, carry):
        r0 = pl.multiple_of(c * rc, rc)
        (zr, zk, zv), lasts = shifted((pr_ref, pk_ref, pv_ref), (mur_ref, muk_ref, muv_ref), carry, r0)
        rows = pl.ds(r0, rc)
        lw, g, kk, k2, beta, bonus = _rwkv_prep(
            zr, zk, zv, tw_s[rows, :], za_s[rows, :], sg_s[rows, :],
            w0_ref[...], w2_ref[...], a0_ref[...], a2_ref[...], g2_ref[...],
            kkw_ref[...], kaw_ref[...], rkw_ref[...], ones_bd)
        r_s[rows, :] = zr
        lw_s[rows, :] = lw
        k_s[rows, :] = k2
        v_s[rows, :] = zv
        kap_s[rows, :] = kk
        bet_s[rows, :] = beta
        g_s[rows, :] = g
        bon_s[rows, :] = bonus
        return lasts

    lax.fori_loop(0, n_ch, phase1, tuple(jnp.zeros((1, LANES), F32) for _ in range(3)))

    c_len = CHUNK
    slot = c_len + LANES
    ti = lax.broadcasted_iota(jnp.int32, (c_len, LANES), 0)
    si = lax.broadcasted_iota(jnp.int32, (c_len, LANES), 1) % HEAD
    strict = ti > si
    incl = ti >= si
    lri = lax.broadcasted_iota(jnp.int32, (c_len, c_len), 0)
    lci = lax.broadcasted_iota(jnp.int32, (c_len, c_len), 1)
    l_cum = (lri >= lci).astype(BF16)
    lane_head = lax.broadcasted_iota(jnp.int32, (1, LANES), 1) // HEAD
    lane_head2 = jnp.concatenate([lane_head, lane_head], axis=1)
    bri = lax.broadcasted_iota(jnp.int32, (LANES, LANES), 0)
    bci = lax.broadcasted_iota(jnp.int32, (LANES, LANES), 1)
    same_head = (bri // HEAD) == (bci // HEAD)
    same_head2 = jnp.concatenate([same_head, same_head], axis=1)
    eye = bri == bci
    zc = jnp.zeros((c_len, LANES), F32)
    nt_dims = (((1,), (1,)), ((), ()))

    def precompute(gi, between_stages):
        chunks = [gi * group + cc for cc in range(group)]
        rows = [pl.ds(pl.multiple_of(c * c_len, c_len), c_len) for c in chunks]
        lws = [lw_s[rw, :] for rw in rows]
        gcums = [_dot_exact_lhs(l_cum, lw) for lw in lws]
        g_ends = [g[c_len - 1:c_len, :] for g in gcums]
        kts = [kap_s[rw, :] * jnp.exp(g - lw) for rw, g, lw in zip(rows, gcums, lws)]
        rts = [r_s[rw, :] * jnp.exp(g) for rw, g in zip(rows, gcums)]
        e_negs = [jnp.exp(-g) for g in gcums]
        lhss = [jnp.concatenate([kt, rt], axis=0) for kt, rt in zip(kts, rts)]
        o_bs = [lax.dot_general(lhs, _head_split(bet_s[rw, :] * en, lane_head), nt_dims, preferred_element_type=F32)
                for lhs, rw, en in zip(lhss, rows, e_negs)]
        o_ks = [lax.dot_general(lhs, _head_split(k_s[rw, :] * en, lane_head), nt_dims, preferred_element_type=F32)
                for lhs, rw, en in zip(lhss, rows, e_negs)]
        n_mats = [jnp.where(strict, o[0:c_len], 0.0) for o in o_bs]
        v_bds = [_head_split(v_s[rw, :], lane_head) for rw in rows]
        avs = [jnp.dot(jnp.where(strict, o[0:c_len], 0.0), v_bd, preferred_element_type=F32)
               for o, v_bd in zip(o_ks, v_bds)]
        rhss = [jnp.concatenate([kt, av], axis=1) for kt, av in zip(kts, avs)]
        a_rs = [jnp.concatenate([jnp.where(incl, -ob[c_len:], 0.0), jnp.where(incl, ok[c_len:], 0.0)], axis=1)
                for ob, ok in zip(o_bs, o_ks)]
        e_hats = [jnp.exp(ge - g) for ge, g in zip(g_ends, gcums)]
        bk_ts = [jnp.concatenate([bet_s[rw, :] * eh, k_s[rw, :] * eh], axis=0).T for rw, eh in zip(rows, e_hats)]
        xs = _solve_unit_lower(n_mats, rhss, lane_head2, xbd_s, between_stages)
        wus = [-x for x in xs]
        tops = [jnp.dot(a_r, jnp.concatenate([xbd_s[i],
                                              jnp.concatenate([jnp.zeros((LANES, LANES), F32), v_bd], axis=1)],
                                             axis=0), preferred_element_type=F32)
                for i, (a_r, v_bd) in enumerate(zip(a_rs, v_bds))]
        bots = [jnp.dot(bk_t, jnp.concatenate([wu, jnp.concatenate([zc, v_s[rw, :]], axis=1)], axis=0),
                        preferred_element_type=F32)
                for bk_t, wu, rw in zip(bk_ts, wus, rows)]
        for c, top, bot, rt, ge in zip(chunks, tops, bots, rts, g_ends):
            bot = jnp.where(same_head2, bot, 0.0)
            base = pl.multiple_of((c + group) * slot, SUBLANES)
            lm_s[pl.ds(base, c_len), :] = top[:, 0:LANES] + rt
            lm_s[pl.ds(base + c_len, LANES), :] = bot[:, 0:LANES] + jnp.where(eye, jnp.exp(ge), 0.0)
            yb_s[pl.ds(base, c_len), :] = top[:, LANES:]
            yb_s[pl.ds(base + c_len, LANES), :] = bot[:, LANES:]

    def advance(c, h_bd):
        base = pl.multiple_of((c + group) * slot, SUBLANES)
        res = _dot_3pass(lm_s[pl.ds(base, slot), :], h_bd) + yb_s[pl.ds(base, slot), :]
        y_s[pl.ds(pl.multiple_of((c + group) * c_len, c_len), c_len), :] = res[0:c_len]
        return res[c_len:]

    def phase3(c, carry):
        r0 = pl.multiple_of(c * rc, rc)
        o = _rwkv_post(y_s[pl.ds(r0 + group * c_len, rc), :], bon_s[pl.ds(r0, rc), :], g_s[pl.ds(r0, rc), :],
                       gnw_ref[...], gnb_ref[...], ones_bd)
        ob_ref[pl.ds(r0, rc), :] = o.astype(ob_ref.dtype)
        return carry

    n_stages = c_len // SUBLANES
    per_stage = -(-group // n_stages)
    n_groups = t_len // (c_len * group)
    lm_s[0:group * slot, :] = jnp.zeros((group * slot, LANES), F32)
    yb_s[0:group * slot, :] = jnp.zeros((group * slot, LANES), F32)

    def fused(gi, h):
        state = [h]

        def between_stages(blk):
            for cc in range(blk * per_stage, min((blk + 1) * per_stage, group)):
                state[0] = advance((gi - 1) * group + cc, state[0])

        precompute(gi, between_stages)
        return state[0]

    h_bd = lax.fori_loop(0, n_groups, fused, jnp.zeros((LANES, LANES), F32))
    n_ready = min(group, ((n_groups - 1) * group * c_len) // rc)
    for blk in range(group):
        h_bd = advance(jnp.int32((n_groups - 1) * group + blk), h_bd)
        if blk < n_ready:
            phase3(jnp.int32(blk), 0)
    s_bd = h_bd.T
    s_ref[0, 0] = s_bd[0:HEAD, 0:HEAD]
    s_ref[0, 1] = s_bd[HEAD:, HEAD:]
    lax.fori_loop(n_ready, n_ch, phase3, 0)


def _rwkv_specs(row_block, rows, q_rows, lay):
    pw, pa, pg, nb = lay["pw"], lay["pa"], lay["pg"], lay["nb"]

    def im(col_fn, on_rows=False):
        if row_block is None:
            return lambda b, hp: (b if on_rows else 0, col_fn(hp))
        return lambda hp: (row_block if on_rows else 0, col_fn(hp))

    def triple(n_rows, on_rows, c_r, c_k, c_v, c_wd, c_ad, c_gd):
        return [
            pl.BlockSpec((n_rows, LANES), im(lambda hp: c_r + hp, on_rows)),
            pl.BlockSpec((n_rows, LANES), im(lambda hp: c_k + hp, on_rows)),
            pl.BlockSpec((n_rows, LANES), im(lambda hp: c_v + hp, on_rows)),
            pl.BlockSpec((n_rows, pw * LANES), im(lambda hp: c_wd // pw, on_rows)),
            pl.BlockSpec((n_rows, pa * LANES), im(lambda hp: c_ad // pa, on_rows)),
            pl.BlockSpec((n_rows, pg * LANES), im(lambda hp: c_gd // pg, on_rows)),
        ]

    specs = triple(rows, True, lay["r"], lay["k"], lay["v"], lay["wd"], lay["ad"], lay["gd"])
    if q_rows:
        specs += triple(q_rows, False, 0, nb, 2 * nb, 0, pw, pw + pa)
    specs += triple(1, False, 0, nb, 2 * nb, 0, pw, pw + pa)
    per_pair = im(lambda hp: hp)
    specs += [
        pl.BlockSpec((1, LANES), per_pair),
        pl.BlockSpec((pw * LANES, LANES), per_pair),
        pl.BlockSpec((1, LANES), per_pair),
        pl.BlockSpec((pa * LANES, LANES), per_pair),
        pl.BlockSpec((pg * LANES, LANES), per_pair),
        pl.BlockSpec((1, LANES), per_pair),
        pl.BlockSpec((1, LANES), per_pair),
        pl.BlockSpec((1, LANES), per_pair),
        pl.BlockSpec((1, LANES), per_pair),
        pl.BlockSpec((1, LANES), per_pair),
    ]
    return specs


def _rwkv_prompt(p_main, p_tail, mu_rkv, mu_lora, params, *, bp, tp, d, lay):
    n_hp = d // LANES
    pw, pa, pg = lay["pw"], lay["pa"], lay["pg"]
    rc = 256 if tp % 256 == 0 else tp
    n_heads = d // HEAD
    n_chunks = tp // CHUNK
    group = next(g for g in (8, 4, 2, 1) if n_chunks % g == 0)
    kern = functools.partial(_rwkv_prompt_kernel, t_len=tp, rc=rc, group=group)
    vm = pltpu.VMEM((tp, LANES), F32)
    return pl.pallas_call(
        kern,
        grid=(bp, n_hp),
        in_specs=_rwkv_specs(None, tp, 0, lay),
        out_specs=[
            pl.BlockSpec((tp, LANES), lambda b, hp: (b, hp)),
            pl.BlockSpec((1, 2, HEAD, HEAD), lambda b, hp: (b, hp, 0, 0)),
        ],
        out_shape=[
            jax.ShapeDtypeStruct((bp * tp, d), BF16),
            jax.ShapeDtypeStruct((bp, n_heads, HEAD, HEAD), F32),
        ],
        scratch_shapes=[vm] * 8 + [
            pltpu.VMEM((tp + group * CHUNK, LANES), F32),
            pltpu.VMEM((tp, pw * LANES), BF16),
            pltpu.VMEM((tp, pa * LANES), BF16),
            pltpu.VMEM((tp, pg * LANES), BF16),
            pltpu.VMEM(((n_chunks + group) * (CHUNK + LANES), LANES), F32),
            pltpu.VMEM(((n_chunks + group) * (CHUNK + LANES), LANES), F32),
            pltpu.VMEM((group, 2 * CHUNK, 2 * LANES), F32),
        ],
        compiler_params=_cparams(2),
        name="rwkv_prompt",
    )(p_main, p_main, p_main, p_tail, p_tail, p_tail,
      mu_rkv, mu_rkv, mu_rkv, mu_lora, mu_lora, mu_lora, *params)


def _rwkv_sample_kernel(pr_ref, pk_ref, pv_ref, pwd_ref, pad_ref, pgd_ref,
                        qr_ref, qk_ref, qv_ref, qwd_ref, qad_ref, qgd_ref,
                        mur_ref, muk_ref, muv_ref, muwd_ref, muad_ref, mugd_ref,
                        w0_ref, w2_ref, a0_ref, a2_ref, g2_ref, kkw_ref, kaw_ref, rkw_ref, gnw_ref, gnb_ref,
                        st_ref, ob_ref, so_ref, y_s, tv_s, *, ts, bs):
    ones_bd = _head_ones()
    refs = (pr_ref, pk_ref, pv_ref, pwd_ref, pad_ref, pgd_ref)
    prevs = (qr_ref, qk_ref, qv_ref, qwd_ref, qad_ref, qgd_ref)
    mus = (mur_ref, muk_ref, muv_ref, muwd_ref, muad_ref, mugd_ref)
    zs = []
    for ref, q, mu in zip(refs, prevs, mus):
        p = ref[...].astype(F32)
        pp = jnp.concatenate([q[...].astype(F32), p[0:(ts - 1) * bs, :]], axis=0)
        zs.append(p + mu[...] * (pp - p))
    zr, zk, zv, zwd, zad, zgd = zs
    lw, g, kk, k2, beta, bonus = _rwkv_prep(
        zr, zk, zv, jnp.tanh(zwd).astype(BF16), zad.astype(BF16), jax.nn.sigmoid(zgd).astype(BF16),
        w0_ref[...], w2_ref[...], a0_ref[...], a2_ref[...], g2_ref[...],
        kkw_ref[...], kaw_ref[...], rkw_ref[...], ones_bd)
    w_dec = jnp.exp(lw)

    for t in range(ts):
        rows = slice(t * bs, (t + 1) * bs)
        for kind, a in enumerate((w_dec, kk, beta, k2, zr, zv)):
            tv_s[t, kind] = a[rows, :].T

    def total(p):
        p = p + pltpu.roll(p, 4, 0)
        p = p + pltpu.roll(p, 2, 0)
        return p + pltpu.roll(p, 1, 0)

    groups = HEAD // SUBLANES
    subi = lax.broadcasted_iota(jnp.int32, (SUBLANES, bs), 0)

    def row_group(gidx, carry):
        h = gidx // groups
        i0 = (gidx % groups) * SUBLANES
        lo = pl.multiple_of(h * HEAD, HEAD)
        row0 = pl.multiple_of(gidx * SUBLANES, SUBLANES)
        y_tiles = [jnp.zeros((SUBLANES, bs), F32) for _ in range(ts)]
        for half in range(0, SUBLANES, _SAMPLE_ROWS):
            s_rows = [st_ref[h, i0 + half + i].reshape(groups, SUBLANES, bs) for i in range(_SAMPLE_ROWS)]
            for t in range(ts):
                wt, kkt, bt, kt, rt = (tv_s[t, kind, pl.ds(lo, HEAD), :].reshape(groups, SUBLANES, bs)
                                       for kind in range(5))
                vt8 = tv_s[t, 5, pl.ds(row0, SUBLANES), :]
                for i in range(_SAMPLE_ROWS):
                    s = s_rows[i]
                    sa = -total(jnp.sum(s * kkt, axis=0))
                    s = s * wt + sa[None] * bt + vt8[half + i:half + i + 1, :][None] * kt
                    s_rows[i] = s
                    y_tiles[t] = jnp.where(subi == half + i, total(jnp.sum(s * rt, axis=0)), y_tiles[t])
            for i in range(_SAMPLE_ROWS):
                so_ref[h, i0 + half + i] = s_rows[i].reshape(HEAD, bs)
        for t in range(ts):
            y_s[t, pl.ds(row0, SUBLANES), :] = y_tiles[t]
        return carry

    lax.fori_loop(0, 2 * groups, row_group, 0)
    for t in range(ts):
        yt = y_s[t].T
        rows = slice(t * bs, (t + 1) * bs)
        o = _rwkv_post(yt, bonus[rows], g[rows], gnw_ref[...], gnb_ref[...], ones_bd)
        ob_ref[rows, :] = o.astype(ob_ref.dtype)


def _rwkv_sample(p_main, p_tail, q_rkv, q_lora, mu_rkv, mu_lora, params, st, *, ts, bs, d, mp, lay):
    n_hp = d // LANES
    rows = ts * bs
    rblk = mp // rows
    in_specs = _rwkv_specs(rblk, rows, bs, lay)
    in_specs.append(pl.BlockSpec((2, HEAD, HEAD, bs), lambda hp: (hp, 0, 0, 0)))
    kern = functools.partial(_rwkv_sample_kernel, ts=ts, bs=bs)
    return pl.pallas_call(
        kern,
        grid=(n_hp,),
        in_specs=in_specs,
        out_specs=[
            pl.BlockSpec((rows, LANES), lambda hp: (0, hp)),
            pl.BlockSpec((2, HEAD, HEAD, bs), lambda hp: (hp, 0, 0, 0)),
        ],
        out_shape=[
            jax.ShapeDtypeStruct((rows, d), BF16),
            jax.ShapeDtypeStruct(st.shape, F32),
        ],
        scratch_shapes=[pltpu.VMEM((ts, LANES, bs), F32), pltpu.VMEM((ts, 6, LANES, bs), F32)],
        compiler_params=_cparams(1),
        name="rwkv_sample",
    )(p_main, p_main, p_main, p_tail, p_tail, p_tail, q_rkv, q_rkv, q_rkv, q_lora, q_lora, q_lora,
      mu_rkv, mu_rkv, mu_rkv, mu_lora, mu_lora, mu_lora, *params, st)


def _pad_cols(a, width):
    return jnp.pad(a, ((0, 0), (0, width - a.shape[1])))


def _pad_rows(a, height):
    return jnp.pad(a, ((0, height - a.shape[0]), (0, 0)))


def _tiles(d, d_ff):
    pick = lambda n, cands: next(c for c in cands if n % c == 0)
    return dict(
        tm=512, tm_wide=256,
        main=pick(5 * d, (2560, 1280, 1024, 512, 256, 128)),
        gates=pick(2 * d, (2048, 1024, 512, 256, 128)),
        prev=pick(d, (1024, 512, 256, 128)),
        up=pick(d_ff, (2048, 1024, 512, 256, 128)),
        down=pick(d, (1024, 512, 256, 128)),
    )


def kernel(x_prompt, x_sample, p_prompt, p_sample, state_rg_h, state_rg_conv, state_rwkv, state_shift,
           norm_mix, w_in, conv_w, conv_b, rg_wa, rg_ba, rg_wx, rg_bx, rg_lam, w_rg_o,
           mu_shift, rw_w0, rw_w2, rw_a0, rw_a2, rw_g2, rw_kk, rw_ka, rw_rk, rw_gn_w, rw_gn_b,
           w_rw_o, w_o, norm_ffn, w_up, w_down, norm_ple, w_ple_gate, w_ple, norm_f):
    bp, tp, d = x_prompt.shape
    bs, ts, _ = x_sample.shape
    depth = w_in.shape[0]
    n_heads, head = rw_rk.shape[1], rw_rk.shape[2]
    r_w, r_a, r_g = rw_w2.shape[1], rw_a2.shape[1], rw_g2.shape[1]
    d_ple = w_ple.shape[1]
    d_ff = w_up.shape[2]
    n_tap = conv_w.shape[1]
    assert head == HEAD and n_heads * HEAD == d and d % LANES == 0 and bs == LANES
    assert rg_wa.shape[2] == LANES and tp % CHUNK == 0 and n_tap == 4
    mp, ms = bp * tp, bs * ts
    nb = d // LANES
    pw, pa, pg = (_round_up(r, LANES) // LANES for r in (r_w, r_a, r_g))
    n_lora = (pw + pa + pg) * LANES
    lay = dict(u=0, g=nb, r=2 * nb, k=3 * nb, v=4 * nb, wd=0, ad=pw, gd=pw + pa, pw=pw, pa=pa, pg=pg, nb=nb)
    t = _tiles(d, d_ff)
    tm = t["tm"]
    assert mp % tm == 0 and ms % tm == 0 and mp % (ts * bs) == 0 and ms % t["tm_wide"] == 0
    assert lay["wd"] % pw == 0 and lay["ad"] % pa == 0 and lay["gd"] % pg == 0 and pw % pa == 0 and (pw + pa) % pg == 0
    assert (2 * d) % t["prev"] == 0 and (5 * d + r_w + r_a + r_g) % SUBLANES == 0
    mm = functools.partial(_mm, mp=mp, ms=ms)

    x = _Rows(x_prompt.reshape(mp, d), jnp.transpose(x_sample, (1, 0, 2)).reshape(ms, d))
    hp_l, cp_l, sp_l, xp_l, hs_l, cs_l, ss_l, xs_l = [], [], [], [], [], [], [], []
    y_p = y_s = None
    for i in range(depth):
        o_rw = 2 * d
        o_lora = o_rw + 3 * d
        o_g = o_lora + r_w + r_a + r_g
        wt = jnp.swapaxes(w_in[i], 0, 1)

        def lora_rows(a):
            return jnp.concatenate([
                _pad_rows(a[o_lora:o_lora + r_w], pw * LANES),
                _pad_rows(a[o_lora + r_w:o_lora + r_w + r_a], pa * LANES),
                _pad_rows(a[o_lora + r_w + r_a:o_g], pg * LANES)], axis=0)

        wt_lora = lora_rows(wt)
        mu_all = jnp.concatenate([jnp.zeros((o_rw, 1), F32), mu_shift[i][:, None], jnp.zeros((2 * d, 1), F32)], axis=0)
        mu_rkv = mu_shift[i][None, :3 * d]
        mu_lora = lora_rows(mu_all).reshape(1, n_lora)
        rw_params = (
            rw_w0[i][None], _pad_rows(rw_w2[i], pw * LANES).astype(BF16),
            rw_a0[i][None], _pad_rows(rw_a2[i], pa * LANES).astype(BF16),
            _pad_rows(rw_g2[i], pg * LANES).astype(BF16),
            rw_kk[i][None], rw_ka[i][None], rw_rk[i].reshape(1, d), rw_gn_w[i][None], rw_gn_b[i][None])
        rg_params = (conv_w[i], conv_b[i][None], rg_wa[i].astype(BF16), rg_ba[i][:, None, :],
                     rg_wx[i].astype(BF16), rg_bx[i][:, None, :], rg_lam[i][None])

        xn = _Rows(_rmsnorm(x, norm_mix[i], BF16, tm, mp))
        xa = x.arrays if len(x.arrays) == 2 else (x.arrays[0][:mp], x.arrays[0][mp:])
        x_last = jnp.concatenate([xa[0].reshape(bp, tp, d)[:, -1], xa[1][(ts - 1) * bs:]], axis=0)
        n_last = _round_up(bp + bs, SUBLANES)
        xn_last = _rmsnorm(_Rows(_pad_rows(x_last, n_last)), norm_mix[i], F32, n_last, n_last)
        (p_main,) = mm([xn], [wt], [], _epi_plain, [(F32, False)], tm=tm, tn=t["main"], n_out=5 * d,
                       cast_w=True, w_transposed=True, single_buffer_w=True, name="in_proj")
        (p_gate,) = mm([xn], [wt], [], _epi_plain, [(BF16, False)], tm=tm, tn=t["gates"], n_out=2 * d,
                       w_row_start=o_g, cast_w=True, w_transposed=True, single_buffer_w=True, name="in_proj_gates")
        (p_lora,) = mm([xn], [wt_lora], [], _epi_plain, [(BF16, False)], tm=tm, tn=n_lora, n_out=n_lora,
                       cast_w=True, w_transposed=True, name="in_proj_lora")
        xprev = _Rows(state_shift[i])
        (q_rkv,) = _mm([xprev], [wt], [], _epi_plain, [(F32, False)], mp=bs, ms=0, tm=bs, tn=t["prev"],
                       n_out=3 * d, w_col_off=o_rw // t["prev"], cast_w=True, w_transposed=True, name="prev_proj")
        (q_lora,) = _mm([xprev], [wt_lora], [], _epi_plain, [(BF16, False)], mp=bs, ms=0, tm=bs, tn=n_lora,
                        n_out=n_lora, cast_w=True, w_transposed=True, name="prev_proj_lora")

        cwid = 2 * LANES if nb % 2 == 0 else LANES
        hg_p, h_p, c_p = _rg_prompt(p_main, *rg_params, bp=bp, tp=tp, d=d, cb_u=lay["u"], cb_g=lay["g"], cwid=cwid)
        cbuf_t = jnp.transpose(state_rg_conv[i], (1, 0, 2))
        hg_s, h_s, c_s = _rg_sample(p_main, cbuf_t, state_rg_h[i], *rg_params, ts=ts, bs=bs, d=d, mp=mp,
                                    cb_u=lay["u"], cb_g=lay["g"], cwid=cwid)

        ob_p, s_p = _rwkv_prompt(p_main, p_lora, mu_rkv, mu_lora, rw_params, bp=bp, tp=tp, d=d, lay=lay)
        st = jnp.transpose(state_rwkv[i], (1, 2, 3, 0))
        ob_s, st_new = _rwkv_sample(p_main, p_lora, q_rkv, q_lora, mu_rkv, mu_lora, rw_params, st,
                                    ts=ts, bs=bs, d=d, mp=mp, lay=lay)
        s_s = jnp.transpose(st_new, (3, 0, 1, 2))

        tw = t["tm_wide"]
        (merged,) = mm([_Rows(hg_p, hg_s), _Rows(ob_p, ob_s)], [w_rg_o[i].astype(BF16), w_rw_o[i].astype(BF16)],
                       [("tile", _Rows(p_gate), 0), ("tile", _Rows(p_gate), 1)],
                       _epi_merge, [(BF16, False)], tm=tw, tn=d, n_out=d, single_buffer_w=True, name="merge")
        x1, xn2 = mm([_Rows(merged)], [w_o[i]], [("tile", x, 0), ("row", norm_ffn[i][None], 0)],
                     _epi_residual_norm, [(F32, False), (BF16, False)], tm=tw, tn=d, n_out=d, cast_w=True,
                     single_buffer_w=True, name="out_proj")
        (hf,) = mm([_Rows(xn2)], [w_up[i]], [], _epi_relu2, [(BF16, False)], tm=tm, tn=t["up"], n_out=d_ff,
                   cast_w=True, single_buffer_w=True, name="mlp_up")
        (x2,) = mm([_Rows(hf)], [w_down[i].astype(BF16)], [("tile", _Rows(x1), 0)], _epi_residual, [(F32, False)],
                   tm=tm, tn=t["down"], n_out=d, single_buffer_w=True, name="mlp_down")
        p_rows = _Rows(p_prompt[i].reshape(mp, d_ple), jnp.transpose(p_sample[i], (1, 0, 2)).reshape(ms, d_ple))
        ple_ws = [w_ple_gate[i], w_ple[i]]
        g_ple = ("row", norm_ple[i][None], 0)
        if i == depth - 1:
            y_p, y_s = mm([("norm_of", 0, 2), p_rows], ple_ws,
                          [("tile", _Rows(x2), 0), ("row", norm_f[None], 0), g_ple],
                          _epi_ple_final, [(F32, True)], tm=tw, tn=d, n_out=d, cast_w=True, single_buffer_w=True,
                          name="ple_final")
        else:
            x = _Rows(*mm([("norm_of", 0, 1), p_rows], ple_ws, [("tile", _Rows(x2), 0), g_ple], _epi_ple,
                          [(F32, False)], tm=tw, tn=d, n_out=d, cast_w=True, single_buffer_w=True, name="ple"))

        hp_l.append(h_p[:, 0])
        cp_l.append(c_p[:, SUBLANES - (n_tap - 1):])
        sp_l.append(s_p)
        xp_l.append(xn_last[:bp])
        hs_l.append(h_s)
        cs_l.append(jnp.transpose(c_s, (1, 0, 2)))
        ss_l.append(s_s)
        xs_l.append(xn_last[bp:bp + bs])

    y_prompt = y_p.reshape(bp, tp, d)
    y_sample = jnp.transpose(y_s.reshape(ts, bs, d), (1, 0, 2))
    return (y_prompt, y_sample,
            jnp.stack(hp_l), jnp.stack(cp_l), jnp.stack(sp_l), jnp.stack(xp_l),
            jnp.stack(hs_l), jnp.stack(cs_l), jnp.stack(ss_l), jnp.stack(xs_l))
```

```python
import functools

import jax
import jax.numpy as jnp
from jax import lax
from jax.experimental import pallas as pl
from jax.experimental.pallas import tpu as pltpu

F32 = jnp.float32
BF16 = jnp.bfloat16
LANES = 128
SUBLANES = 8
HEAD = 64
CHUNK = 64
_SAMPLE_ROWS = 4
EPS = 1e-6
GN_EPS = 64e-5
RG_C = 8.0
VMEM_LIMIT = 56 * 1024 * 1024


def _cparams(n_grid, vmem=VMEM_LIMIT):
    return pltpu.CompilerParams(dimension_semantics=("arbitrary",) * n_grid, vmem_limit_bytes=vmem)


def _round_up(x, m):
    return (x + m - 1) // m * m


def _rms(x, g):
    ms = jnp.mean(x * x, axis=-1, keepdims=True)
    return (x * lax.rsqrt(ms + EPS)) * g


class _Rows:
    def __init__(self, *arrays):
        self.arrays = arrays

    @property
    def width(self):
        return self.arrays[0].shape[1]

    def specs(self, tm, n_p, col):
        if len(self.arrays) == 1:
            return [lambda wid: pl.BlockSpec((tm, wid), lambda n, i: (i, col(n)))]
        return [lambda wid: pl.BlockSpec((tm, wid), lambda n, i: (jnp.minimum(i, n_p - 1), col(n))),
                lambda wid: pl.BlockSpec((tm, wid), lambda n, i: (jnp.maximum(i - n_p, 0), col(n)))]


def _load_rows(refs, is_prompt):
    if len(refs) == 1:
        return refs[0][...]
    return jnp.where(is_prompt, refs[0][...], refs[1][...])


def _rmsnorm_kernel(*refs, n_src, n_p):
    g_ref, o_ref = refs[n_src], refs[n_src + 1]
    x = _load_rows(refs[:n_src], pl.program_id(1) < n_p)
    o_ref[...] = _rms(x, g_ref[...]).astype(o_ref.dtype)


def _rmsnorm(x, g, out_dtype, tm, mp):
    d = x.width
    m = sum(a.shape[0] for a in x.arrays)
    n_p = mp // tm
    in_specs = [mk(d) for mk in x.specs(tm, n_p, lambda n: 0)] + [pl.BlockSpec((1, d), lambda n, i: (0, 0))]
    return pl.pallas_call(
        functools.partial(_rmsnorm_kernel, n_src=len(x.arrays), n_p=n_p),
        grid=(1, m // tm),
        in_specs=in_specs,
        out_specs=pl.BlockSpec((tm, d), lambda n, i: (i, 0)),
        out_shape=jax.ShapeDtypeStruct((m, d), out_dtype),
        compiler_params=_cparams(2),
        name="rmsnorm",
    )(*x.arrays, g.reshape(1, d))


def _mm_kernel(*refs, x_counts, x_norm, n_w, cast_w, w_transposed, extra_counts, out_counts, n_p, epilogue):
    pos = 0
    x_refs = []
    for cnt in x_counts:
        x_refs.append(refs[pos:pos + cnt])
        pos += cnt
    w_refs = refs[pos:pos + n_w]
    pos += n_w
    e_refs = []
    for cnt in extra_counts:
        e_refs.append(refs[pos:pos + cnt])
        pos += cnt
    o_refs = []
    for cnt in out_counts:
        o_refs.append(refs[pos:pos + cnt])
        pos += cnt
    w_scratch = refs[pos:]
    i = pl.program_id(1)
    is_prompt = i < n_p
    if cast_w:
        @pl.when(i == 0)
        def _():
            for w_ref, s_ref in zip(w_refs, w_scratch):
                s_ref[...] = w_ref[...].astype(BF16)
        ws = [s[...] for s in w_scratch]
    else:
        ws = [w[...] for w in w_refs]
    dims = (((1,), (1 if w_transposed else 0,)), ((), ()))
    e_vals = [_load_rows(er, is_prompt) for er in e_refs]
    x_vals = []
    for idx, xr in enumerate(x_refs):
        if xr:
            x_vals.append(_load_rows(xr, is_prompt).astype(BF16))
        else:
            e_idx, g_idx = x_norm[idx]
            x_vals.append(_rms(e_vals[e_idx], e_vals[g_idx]).astype(BF16))
    accs = [lax.dot_general(x, w, dims, preferred_element_type=F32) for x, w in zip(x_vals, ws)]
    outs = epilogue(accs, e_vals)
    for refs_o, tile in zip(o_refs, outs):
        if len(refs_o) == 1:
            refs_o[0][...] = tile.astype(refs_o[0].dtype)
        else:
            @pl.when(is_prompt)
            def _():
                refs_o[0][...] = tile.astype(refs_o[0].dtype)

            @pl.when(jnp.logical_not(is_prompt))
            def _():
                refs_o[1][...] = tile.astype(refs_o[1].dtype)


def _mm(xs, ws, extras, epilogue, outs, *, mp, ms, tm, tn, n_out, w_col_off=0, w_row_start=None, cast_w=False,
        w_transposed=False, single_buffer_w=False, name="mm"):
    n_p = mp // tm
    m = mp + ms
    grid = (n_out // tn, m // tm)
    in_specs, args, x_counts, extra_counts, x_norm = [], [], [], [], {}
    for idx, x in enumerate(xs):
        if isinstance(x, tuple):
            x_norm[idx] = (x[1], x[2])
            x_counts.append(0)
            continue
        in_specs += [mk(x.width) for mk in x.specs(tm, n_p, lambda n: 0)]
        args += list(x.arrays)
        x_counts.append(len(x.arrays))
    w_mode = dict(pipeline_mode=pl.Buffered(1)) if single_buffer_w else {}
    if w_transposed and w_row_start is not None:
        w_blocks = [(tn, w.shape[1]) for w in ws]
        in_specs += [pl.BlockSpec((pl.Element(tn), pl.Element(w.shape[1])),
                                  lambda n, i: (pl.multiple_of(w_row_start + n * tn, SUBLANES), 0), **w_mode)
                     for w in ws]
    elif w_transposed:
        w_blocks = [(tn, w.shape[1]) for w in ws]
        in_specs += [pl.BlockSpec(blk, lambda n, i: (n + w_col_off, 0), **w_mode) for blk in w_blocks]
    else:
        w_blocks = [(w.shape[0], tn) for w in ws]
        in_specs += [pl.BlockSpec(blk, lambda n, i: (0, n + w_col_off), **w_mode) for blk in w_blocks]
    args += list(ws)
    for kind, src, off in extras:
        if kind == "tile":
            in_specs += [mk(tn) for mk in src.specs(tm, n_p, lambda n, off=off: n + off)]
            args += list(src.arrays)
            extra_counts.append(len(src.arrays))
        elif kind == "row":
            in_specs.append(pl.BlockSpec((1, tn), lambda n, i, off=off: (0, n + off)))
            args.append(src)
            extra_counts.append(1)
        else:
            raise ValueError(kind)
    out_specs, out_shape, out_counts = [], [], []
    for dtype, split in outs:
        if split:
            out_specs += [pl.BlockSpec((tm, tn), lambda n, i: (jnp.minimum(i, n_p - 1), n)),
                          pl.BlockSpec((tm, tn), lambda n, i: (jnp.maximum(i - n_p, 0), n))]
            out_shape += [jax.ShapeDtypeStruct((mp, n_out), dtype), jax.ShapeDtypeStruct((ms, n_out), dtype)]
            out_counts.append(2)
        else:
            out_specs.append(pl.BlockSpec((tm, tn), lambda n, i: (i, n)))
            out_shape.append(jax.ShapeDtypeStruct((m, n_out), dtype))
            out_counts.append(1)
    kern = functools.partial(_mm_kernel, x_counts=tuple(x_counts), x_norm=x_norm, n_w=len(ws), cast_w=cast_w,
                             w_transposed=w_transposed, extra_counts=tuple(extra_counts),
                             out_counts=tuple(out_counts), n_p=n_p, epilogue=epilogue)
    scratch = [pltpu.VMEM(blk, BF16) for blk in w_blocks] if cast_w else []
    res = pl.pallas_call(
        kern,
        grid=grid,
        in_specs=in_specs,
        out_specs=out_specs,
        out_shape=out_shape,
        scratch_shapes=scratch,
        compiler_params=_cparams(2),
        name=name,
    )(*args)
    return res


def _epi_plain(accs, extras):
    return (accs[0],)


def _epi_residual(accs, extras):
    return (extras[0] + accs[0],)


def _epi_merge(accs, extras):
    ga, gb = extras
    return (jax.nn.sigmoid(ga.astype(F32)) * accs[0] + jax.nn.sigmoid(gb.astype(F32)) * accs[1],)


def _epi_residual_norm(accs, extras):
    x = extras[0] + accs[0]
    return x, _rms(x, extras[1])


def _epi_relu2(accs, extras):
    h = jnp.maximum(accs[0], 0.0)
    return (h * h,)


def _epi_ple(accs, extras):
    return (extras[0] + jax.nn.sigmoid(accs[0]) * accs[1],)


def _epi_ple_final(accs, extras):
    return (_rms(extras[0] + jax.nn.sigmoid(accs[0]) * accs[1], extras[1]),)


def _softplus(x):
    return jnp.maximum(x, 0.0) + jnp.log1p(jnp.exp(-jnp.abs(x)))


def _split_bf16(x, parts):
    out = []
    rem = x
    for _ in range(parts):
        p = rem.astype(BF16)
        out.append(p)
        rem = rem - p.astype(F32)
    return out


def _dot_exact_lhs(a_bf16, x, parts=3):
    acc = None
    for p in _split_bf16(x, parts):
        t = jnp.dot(a_bf16, p, preferred_element_type=F32)
        acc = t if acc is None else acc + t
    return acc


def _head_sum(x, ones_bd):
    return jnp.dot(x.astype(BF16), ones_bd, preferred_element_type=F32)


def _sigmoid(x):
    return 0.5 * jnp.tanh(0.5 * x) + 0.5


def _dot_3pass(a, b):
    a_hi, a_lo = _split_bf16(a, 2)
    b_hi, b_lo = _split_bf16(b, 2)
    return (jnp.dot(a_hi, b_hi, preferred_element_type=F32)
            + jnp.dot(a_hi, b_lo, preferred_element_type=F32)
            + jnp.dot(a_lo, b_hi, preferred_element_type=F32))


def _head_ones():
    r = lax.broadcasted_iota(jnp.int32, (LANES, LANES), 0) // HEAD
    c = lax.broadcasted_iota(jnp.int32, (LANES, LANES), 1) // HEAD
    return (r == c).astype(BF16)


def _rwkv_prep(zr, zk, zv, tw, za, sg, w0, w2, a0, a2, g2, kkw, kaw, rkw, ones_bd):
    wlog = -_softplus(-(w0 + jnp.dot(tw, w2, preferred_element_type=F32))) - 0.5
    lw = -jnp.exp(wlog)
    a = _sigmoid(a0 + jnp.dot(za, a2, preferred_element_type=F32))
    g = jnp.dot(sg, g2, preferred_element_type=F32)
    kk = zk * kkw
    ss = _head_sum(kk * kk, ones_bd)
    kk = kk * lax.rsqrt(jnp.maximum(ss, 1e-24))
    k2 = zk * (1.0 + (a - 1.0) * kaw)
    beta = kk * a
    bonus = _head_sum(zr * k2 * rkw, ones_bd) * zv
    return lw, g, kk, k2, beta, bonus


def _rwkv_post(y, bonus, g, gnw, gnb, ones_bd):
    mu = _head_sum(y, ones_bd) * (1.0 / HEAD)
    d = y - mu
    var = _head_sum(d * d, ones_bd) * (1.0 / HEAD)
    yn = (d * lax.rsqrt(var + GN_EPS)) * gnw + gnb
    return (yn + bonus) * g


def _rg_gates(xc, wa, ba, wx, bx, sp):
    xcb = xc.astype(BF16)
    r = _sigmoid(jnp.dot(xcb, wa, preferred_element_type=F32) + ba)
    i = _sigmoid(jnp.dot(xcb, wx, preferred_element_type=F32) + bx)
    log_a = (-RG_C * r) * sp
    a = jnp.exp(log_a)
    mult = jnp.sqrt(-jnp.tanh(log_a) * (a * a + 1.0))
    return a, mult, i * xc


def _rg_prompt_kernel(u_ref, gt_ref, cw_ref, cb_ref, wa_ref, ba_ref, wx_ref, bx_ref, lam_ref,
                      hg_ref, hl_ref, cs_ref, us_ref, a_ref, b_ref, *, t_len, rc):
    cwid = u_ref.shape[1]
    nb = cwid // LANES
    us_ref[0:SUBLANES, :] = jnp.zeros((SUBLANES, cwid), F32)
    us_ref[SUBLANES:, :] = u_ref[...]
    sp = _softplus(-lam_ref[...])
    w0 = cw_ref[0:1, :]
    w1 = cw_ref[1:2, :]
    w2 = cw_ref[2:3, :]
    w3 = cw_ref[3:4, :]
    cb = cb_ref[...]
    n_ch = t_len // rc

    def phase1(c, carry):
        r0 = pl.multiple_of(c * rc, rc)
        e = us_ref[pl.ds(r0, rc + SUBLANES), :]
        u0 = e[SUBLANES:]
        u1 = pltpu.roll(e, 1, 0)[SUBLANES:]
        u2 = pltpu.roll(e, 2, 0)[SUBLANES:]
        u3 = pltpu.roll(e, 3, 0)[SUBLANES:]
        xc = cb + (u3 * w0 + u2 * w1 + u1 * w2 + u0 * w3)
        row = lax.broadcasted_iota(jnp.int32, (rc, LANES), 0) + r0
        for n in range(nb):
            sl = slice(n * LANES, (n + 1) * LANES)
            a, mult, ix = _rg_gates(xc[:, sl], wa_ref[n], ba_ref[n], wx_ref[n], bx_ref[n], sp[:, sl])
            mult = jnp.where(row == 0, 1.0, mult)
            a_ref[n, pl.ds(r0, rc), :] = a
            b_ref[n, pl.ds(r0, rc), :] = mult * ix
        return carry

    lax.fori_loop(0, n_ch, phase1, 0)

    rowi = lax.broadcasted_iota(jnp.int32, (SUBLANES, LANES), 0)

    def scan(i, h_prev):
        r0 = pl.multiple_of(i * SUBLANES, SUBLANES)
        out = []
        for n in range(nb):
            a = a_ref[n, pl.ds(r0, SUBLANES), :]
            b = b_ref[n, pl.ds(r0, SUBLANES), :]
            for d in (1, 2, 4):
                a_sh = jnp.where(rowi < d, 1.0, pltpu.roll(a, d, 0))
                b_sh = jnp.where(rowi < d, 0.0, pltpu.roll(b, d, 0))
                b = a * b_sh + b
                a = a * a_sh
            h = b + a * h_prev[n]
            b_ref[n, pl.ds(r0, SUBLANES), :] = h
            out.append(h[SUBLANES - 1:SUBLANES, :])
        return tuple(out)

    h_last = lax.fori_loop(0, t_len // SUBLANES, scan, tuple(jnp.zeros((1, LANES), F32) for _ in range(nb)),
                           unroll=4)
    hl_ref[0] = jnp.concatenate(list(h_last), axis=1)
    cs_ref[0] = u_ref[pl.ds(t_len - SUBLANES, SUBLANES), :]

    def phase3(c, carry):
        r0 = pl.multiple_of(c * rc, rc)
        for n in range(nb):
            sl = slice(n * LANES, (n + 1) * LANES)
            gate = gt_ref[pl.ds(r0, rc), sl]
            hg_ref[pl.ds(r0, rc), sl] = (b_ref[n, pl.ds(r0, rc), :] * jax.nn.gelu(gate)).astype(hg_ref.dtype)
        return carry

    lax.fori_loop(0, n_ch, phase3, 0)


def _rg_prompt(p, conv_w, conv_b, wa, ba, wx, bx, lam, *, bp, tp, d, cb_u, cb_g, cwid):
    nb = cwid // LANES
    nblk = d // cwid
    rc = 256 if tp % 256 == 0 else tp
    kern = functools.partial(_rg_prompt_kernel, t_len=tp, rc=rc)
    return pl.pallas_call(
        kern,
        grid=(bp, nblk),
        in_specs=[
            pl.BlockSpec((tp, cwid), lambda b, n: (b, cb_u * LANES // cwid + n)),
            pl.BlockSpec((tp, cwid), lambda b, n: (b, cb_g * LANES // cwid + n)),
            pl.BlockSpec((4, cwid), lambda b, n: (0, n)),
            pl.BlockSpec((1, cwid), lambda b, n: (0, n)),
            pl.BlockSpec((nb, LANES, LANES), lambda b, n: (n, 0, 0)),
            pl.BlockSpec((nb, 1, LANES), lambda b, n: (n, 0, 0)),
            pl.BlockSpec((nb, LANES, LANES), lambda b, n: (n, 0, 0)),
            pl.BlockSpec((nb, 1, LANES), lambda b, n: (n, 0, 0)),
            pl.BlockSpec((1, cwid), lambda b, n: (0, n)),
        ],
        out_specs=[
            pl.BlockSpec((tp, cwid), lambda b, n: (b, n)),
            pl.BlockSpec((1, 1, cwid), lambda b, n: (b, 0, n)),
            pl.BlockSpec((1, SUBLANES, cwid), lambda b, n: (b, 0, n)),
        ],
        out_shape=[
            jax.ShapeDtypeStruct((bp * tp, d), BF16),
            jax.ShapeDtypeStruct((bp, 1, d), F32),
            jax.ShapeDtypeStruct((bp, SUBLANES, d), F32),
        ],
        scratch_shapes=[
            pltpu.VMEM((tp + SUBLANES, cwid), F32),
            pltpu.VMEM((nb, tp, LANES), F32),
            pltpu.VMEM((nb, tp, LANES), F32),
        ],
        compiler_params=_cparams(2),
        name="rg_prompt",
    )(p, p, conv_w, conv_b, wa, ba, wx, bx, lam)


def _rg_sample_kernel(u_ref, gt_ref, cbuf_ref, h0_ref, cw_ref, cb_ref, wa_ref, ba_ref, wx_ref, bx_ref, lam_ref,
                      hg_ref, hl_ref, cs_ref, *, ts, bs):
    cwid = u_ref.shape[1]
    nb = cwid // LANES
    n_tap = cw_ref.shape[0]
    sp = _softplus(-lam_ref[...])
    cb = cb_ref[...]
    ext = [cbuf_ref[j] for j in range(n_tap - 1)] + [u_ref[t * bs:(t + 1) * bs, :] for t in range(ts)]
    for j in range(n_tap - 1):
        cs_ref[j] = ext[len(ext) - (n_tap - 1) + j]
    h = h0_ref[...]
    for t in range(ts):
        conv = ext[t] * cw_ref[0:1, :]
        for j in range(1, n_tap):
            conv = conv + ext[t + j] * cw_ref[j:j + 1, :]
        xc = cb + conv
        pieces = []
        for n in range(nb):
            sl = slice(n * LANES, (n + 1) * LANES)
            a, mult, ix = _rg_gates(xc[:, sl], wa_ref[n], ba_ref[n], wx_ref[n], bx_ref[n], sp[:, sl])
            pieces.append(a * h[:, sl] + mult * ix)
        h = pieces[0] if nb == 1 else jnp.concatenate(pieces, axis=1)
        gate = gt_ref[t * bs:(t + 1) * bs, :]
        hg_ref[t * bs:(t + 1) * bs, :] = (h * jax.nn.gelu(gate)).astype(hg_ref.dtype)
    hl_ref[...] = h


def _rg_sample(p, cbuf_t, h0, conv_w, conv_b, wa, ba, wx, bx, lam, *, ts, bs, d, mp, cb_u, cb_g, cwid):
    nb = cwid // LANES
    nblk = d // cwid
    rows = ts * bs
    rblk = mp // rows
    kern = functools.partial(_rg_sample_kernel, ts=ts, bs=bs)
    n_tap = conv_w.shape[0]
    return pl.pallas_call(
        kern,
        grid=(nblk,),
        in_specs=[
            pl.BlockSpec((rows, cwid), lambda n: (rblk, cb_u * LANES // cwid + n)),
            pl.BlockSpec((rows, cwid), lambda n: (rblk, cb_g * LANES // cwid + n)),
            pl.BlockSpec((n_tap - 1, bs, cwid), lambda n: (0, 0, n)),
            pl.BlockSpec((bs, cwid), lambda n: (0, n)),
            pl.BlockSpec((n_tap, cwid), lambda n: (0, n)),
            pl.BlockSpec((1, cwid), lambda n: (0, n)),
            pl.BlockSpec((nb, LANES, LANES), lambda n: (n, 0, 0)),
            pl.BlockSpec((nb, 1, LANES), lambda n: (n, 0, 0)),
            pl.BlockSpec((nb, LANES, LANES), lambda n: (n, 0, 0)),
            pl.BlockSpec((nb, 1, LANES), lambda n: (n, 0, 0)),
            pl.BlockSpec((1, cwid), lambda n: (0, n)),
        ],
        out_specs=[
            pl.BlockSpec((rows, cwid), lambda n: (0, n)),
            pl.BlockSpec((bs, cwid), lambda n: (0, n)),
            pl.BlockSpec((n_tap - 1, bs, cwid), lambda n: (0, 0, n)),
        ],
        out_shape=[
            jax.ShapeDtypeStruct((rows, d), BF16),
            jax.ShapeDtypeStruct((bs, d), F32),
            jax.ShapeDtypeStruct((n_tap - 1, bs, d), F32),
        ],
        compiler_params=_cparams(1),
        name="rg_sample",
    )(p, p, cbuf_t, h0, conv_w, conv_b, wa, ba, wx, bx, lam)


def _shift_rows(x, prev_row, rowi):
    return jnp.where(rowi == 0, prev_row, pltpu.roll(x, 1, 0))


def _head_split(z, lane_head):
    return jnp.concatenate([jnp.where(lane_head == 0, z, 0.0), jnp.where(lane_head == 1, z, 0.0)], axis=0)


def _solve_unit_lower(n_mats, rhss, lane_head, xbd_ref, between_stages=None):
    n_sys = len(n_mats)
    c = n_mats[0].shape[0]
    br = SUBLANES
    xbd_ref[...] = jnp.zeros(xbd_ref.shape, F32)
    done = [[] for _ in range(n_sys)]
    for blk in range(c // br):
        lo = blk * br
        rs = []
        for i in range(n_sys):
            r = rhss[i][lo:lo + br, :]
            if blk > 0:
                r = r - jnp.dot(n_mats[i][lo:lo + br, :], xbd_ref[i], preferred_element_type=F32)
            rs.append(r)
        if between_stages is not None:
            between_stages(blk)
        for j in range(br - 1):
            for i in range(n_sys):
                nrow = n_mats[i][lo:lo + br, :]
                mult = jnp.where(lane_head == 0, nrow[:, lo + j:lo + j + 1], nrow[:, HEAD + lo + j:HEAD + lo + j + 1])
                rs[i] = rs[i] - mult * rs[i][j:j + 1, :]
        for i in range(n_sys):
            done[i].append(rs[i])
            xbd_ref[i, lo:lo + br, :] = jnp.where(lane_head == 0, rs[i], 0.0)
            xbd_ref[i, c + lo:c + lo + br, :] = jnp.where(lane_head == 1, rs[i], 0.0)
    return [jnp.concatenate(b, axis=0) for b in done]


def _rwkv_prompt_kernel(pr_ref, pk_ref, pv_ref, pwd_ref, pad_ref, pgd_ref,
                        mur_ref, muk_ref, muv_ref, muwd_ref, muad_ref, mugd_ref,
                        w0_ref, w2_ref, a0_ref, a2_ref, g2_ref, kkw_ref, kaw_ref, rkw_ref, gnw_ref, gnb_ref,
                        ob_ref, s_ref,
                        r_s, lw_s, k_s, v_s, kap_s, bet_s, g_s, bon_s, y_s, tw_s, za_s, sg_s, lm_s, yb_s, xbd_s,
                        *, t_len, rc, group):
    ones_bd = _head_ones()
    n_ch = t_len // rc
    rowi = lax.broadcasted_iota(jnp.int32, (rc, 1), 0)

    def shifted(refs, mus, carry, r0):
        zs, lasts = [], []
        for ref, mu, prev in zip(refs, mus, carry):
            p = ref[pl.ds(r0, rc), :].astype(F32)
            zs.append(p + mu[...] * (_shift_rows(p, prev, rowi) - p))
            lasts.append(p[rc - 1:rc, :])
        return zs, tuple(lasts)

    @pl.when(pl.program_id(1) == 0)
    def _():
        lora_refs = (pwd_ref, pad_ref, pgd_ref)

        def phase0(c, carry):
            r0 = pl.multiple_of(c * rc, rc)
            (zwd, zad, zgd), lasts = shifted(lora_refs, (muwd_ref, muad_ref, mugd_ref), carry, r0)
            rows = pl.ds(r0, rc)
            tw_s[rows, :] = jnp.tanh(zwd).astype(BF16)
            za_s[rows, :] = zad.astype(BF16)
            sg_s[rows, :] = jax.nn.sigmoid(zgd).astype(BF16)
            return lasts

        lax.fori_loop(0, n_ch, phase0, tuple(jnp.zeros((1, ref.shape[1]), F32) for ref in lora_refs))

    def phase1(c, carry):
        r0 = pl.multiple_of(c * rc, rc)
        (zr, zk, zv), lasts = shifted((pr_ref, pk_ref, pv_ref), (mur_ref, muk_ref, muv_ref), carry, r0)
        rows = pl.ds(r0, rc)
        lw, g, kk, k2, beta, bonus = _rwkv_prep(
            zr, zk, zv, tw_s[rows, :], za_s[rows, :], sg_s[rows, :],
            w0_ref[...], w2_ref[...], a0_ref[...], a2_ref[...], g2_ref[...],
            kkw_ref[...], kaw_ref[...], rkw_ref[...], ones_bd)
        r_s[rows, :] = zr
        lw_s[rows, :] = lw
        k_s[rows, :] = k2
        v_s[rows, :] = zv
        kap_s[rows, :] = kk
        bet_s[rows, :] = beta
        g_s[rows, :] = g
        bon_s[rows, :] = bonus
        return lasts

    lax.fori_loop(0, n_ch, phase1, tuple(jnp.zeros((1, LANES), F32) for _ in range(3)))

    c_len = CHUNK
    slot = c_len + LANES
    ti = lax.broadcasted_iota(jnp.int32, (c_len, LANES), 0)
    si = lax.broadcasted_iota(jnp.int32, (c_len, LANES), 1) % HEAD
    strict = ti > si
    incl = ti >= si
    lri = lax.broadcasted_iota(jnp.int32, (c_len, c_len), 0)
    lci = lax.broadcasted_iota(jnp.int32, (c_len, c_len), 1)
    l_cum = (lri >= lci).astype(BF16)
    lane_head = lax.broadcasted_iota(jnp.int32, (1, LANES), 1) // HEAD
    lane_head2 = jnp.concatenate([lane_head, lane_head], axis=1)
    bri = lax.broadcasted_iota(jnp.int32, (LANES, LANES), 0)
    bci = lax.broadcasted_iota(jnp.int32, (LANES, LANES), 1)
    same_head = (bri // HEAD) == (bci // HEAD)
    same_head2 = jnp.concatenate([same_head, same_head], axis=1)
    eye = bri == bci
    zc = jnp.zeros((c_len, LANES), F32)
    nt_dims = (((1,), (1,)), ((), ()))

    def precompute(gi, between_stages):
        chunks = [gi * group + cc for cc in range(group)]
        rows = [pl.ds(pl.multiple_of(c * c_len, c_len), c_len) for c in chunks]
        lws = [lw_s[rw, :] for rw in rows]
        gcums = [_dot_exact_lhs(l_cum, lw) for lw in lws]
        g_ends = [g[c_len - 1:c_len, :] for g in gcums]
        kts = [kap_s[rw, :] * jnp.exp(g - lw) for rw, g, lw in zip(rows, gcums, lws)]
        rts = [r_s[rw, :] * jnp.exp(g) for rw, g in zip(rows, gcums)]
        e_negs = [jnp.exp(-g) for g in gcums]
        lhss = [jnp.concatenate([kt, rt], axis=0) for kt, rt in zip(kts, rts)]
        o_bs = [lax.dot_general(lhs, _head_split(bet_s[rw, :] * en, lane_head), nt_dims, preferred_element_type=F32)
                for lhs, rw, en in zip(lhss, rows, e_negs)]
        o_ks = [lax.dot_general(lhs, _head_split(k_s[rw, :] * en, lane_head), nt_dims, preferred_element_type=F32)
                for lhs, rw, en in zip(lhss, rows, e_negs)]
        n_mats = [jnp.where(strict, o[0:c_len], 0.0) for o in o_bs]
        v_bds = [_head_split(v_s[rw, :], lane_head) for rw in rows]
        avs = [jnp.dot(jnp.where(strict, o[0:c_len], 0.0), v_bd, preferred_element_type=F32)
               for o, v_bd in zip(o_ks, v_bds)]
        rhss = [jnp.concatenate([kt, av], axis=1) for kt, av in zip(kts, avs)]
        a_rs = [jnp.concatenate([jnp.where(incl, -ob[c_len:], 0.0), jnp.where(incl, ok[c_len:], 0.0)], axis=1)
                for ob, ok in zip(o_bs, o_ks)]
        e_hats = [jnp.exp(ge - g) for ge, g in zip(g_ends, gcums)]
        bk_ts = [jnp.concatenate([bet_s[rw, :] * eh, k_s[rw, :] * eh], axis=0).T for rw, eh in zip(rows, e_hats)]
        xs = _solve_unit_lower(n_mats, rhss, lane_head2, xbd_s, between_stages)
        wus = [-x for x in xs]
        tops = [jnp.dot(a_r, jnp.concatenate([xbd_s[i],
                                              jnp.concatenate([jnp.zeros((LANES, LANES), F32), v_bd], axis=1)],
                                             axis=0), preferred_element_type=F32)
                for i, (a_r, v_bd) in enumerate(zip(a_rs, v_bds))]
        bots = [jnp.dot(bk_t, jnp.concatenate([wu, jnp.concatenate([zc, v_s[rw, :]], axis=1)], axis=0),
                        preferred_element_type=F32)
                for bk_t, wu, rw in zip(bk_ts, wus, rows)]
        for c, top, bot, rt, ge in zip(chunks, tops, bots, rts, g_ends):
            bot = jnp.where(same_head2, bot, 0.0)
            base = pl.multiple_of((c + group) * slot, SUBLANES)
            lm_s[pl.ds(base, c_len), :] = top[:, 0:LANES] + rt
            lm_s[pl.ds(base + c_len, LANES), :] = bot[:, 0:LANES] + jnp.where(eye, jnp.exp(ge), 0.0)
            yb_s[pl.ds(base, c_len), :] = top[:, LANES:]
            yb_s[pl.ds(base + c_len, LANES), :] = bot[:, LANES:]

    def advance(c, h_bd):
        base = pl.multiple_of((c + group) * slot, SUBLANES)
        res = _dot_3pass(lm_s[pl.ds(base, slot), :], h_bd) + yb_s[pl.ds(base, slot), :]
        y_s[pl.ds(pl.multiple_of((c + group) * c_len, c_len), c_len), :] = res[0:c_len]
        return res[c_len:]

    def phase3(c, carry):
        r0 = pl.multiple_of(c * rc, rc)
        o = _rwkv_post(y_s[pl.ds(r0 + group * c_len, rc), :], bon_s[pl.ds(r0, rc), :], g_s[pl.ds(r0, rc), :],
                       gnw_ref[...], gnb_ref[...], ones_bd)
        ob_ref[pl.ds(r0, rc), :] = o.astype(ob_ref.dtype)
        return carry

    n_stages = c_len // SUBLANES
    per_stage = -(-group // n_stages)
    n_groups = t_len // (c_len * group)
    lm_s[0:group * slot, :] = jnp.zeros((group * slot, LANES), F32)
    yb_s[0:group * slot, :] = jnp.zeros((group * slot, LANES), F32)

    def fused(gi, h):
        state = [h]

        def between_stages(blk):
            for cc in range(blk * per_stage, min((blk + 1) * per_stage, group)):
                state[0] = advance((gi - 1) * group + cc, state[0])

        precompute(gi, between_stages)
        return state[0]

    h_bd = lax.fori_loop(0, n_groups, fused, jnp.zeros((LANES, LANES), F32))
    n_ready = min(group, ((n_groups - 1) * group * c_len) // rc)
    for blk in range(group):
        h_bd = advance(jnp.int32((n_groups - 1) * group + blk), h_bd)
        if blk < n_ready:
            phase3(jnp.int32(blk), 0)
    s_bd = h_bd.T
    s_ref[0, 0] = s_bd[0:HEAD, 0:HEAD]
    s_ref[0, 1] = s_bd[HEAD:, HEAD:]
    lax.fori_loop(n_ready, n_ch, phase3, 0)


def _rwkv_specs(row_block, rows, q_rows, lay):
    pw, pa, pg, nb = lay["pw"], lay["pa"], lay["pg"], lay["nb"]

    def im(col_fn, on_rows=False):
        if row_block is None:
            return lambda b, hp: (b if on_rows else 0, col_fn(hp))
        return lambda hp: (row_block if on_rows else 0, col_fn(hp))

    def triple(n_rows, on_rows, c_r, c_k, c_v, c_wd, c_ad, c_gd):
        return [
            pl.BlockSpec((n_rows, LANES), im(lambda hp: c_r + hp, on_rows)),
            pl.BlockSpec((n_rows, LANES), im(lambda hp: c_k + hp, on_rows)),
            pl.BlockSpec((n_rows, LANES), im(lambda hp: c_v + hp, on_rows)),
            pl.BlockSpec((n_rows, pw * LANES), im(lambda hp: c_wd // pw, on_rows)),
            pl.BlockSpec((n_rows, pa * LANES), im(lambda hp: c_ad // pa, on_rows)),
            pl.BlockSpec((n_rows, pg * LANES), im(lambda hp: c_gd // pg, on_rows)),
        ]

    specs = triple(rows, True, lay["r"], lay["k"], lay["v"], lay["wd"], lay["ad"], lay["gd"])
    if q_rows:
        specs += triple(q_rows, False, 0, nb, 2 * nb, 0, pw, pw + pa)
    specs += triple(1, False, 0, nb, 2 * nb, 0, pw, pw + pa)
    per_pair = im(lambda hp: hp)
    specs += [
        pl.BlockSpec((1, LANES), per_pair),
        pl.BlockSpec((pw * LANES, LANES), per_pair),
        pl.BlockSpec((1, LANES), per_pair),
        pl.BlockSpec((pa * LANES, LANES), per_pair),
        pl.BlockSpec((pg * LANES, LANES), per_pair),
        pl.BlockSpec((1, LANES), per_pair),
        pl.BlockSpec((1, LANES), per_pair),
        pl.BlockSpec((1, LANES), per_pair),
        pl.BlockSpec((1, LANES), per_pair),
        pl.BlockSpec((1, LANES), per_pair),
    ]
    return specs


def _rwkv_prompt(p_main, p_tail, mu_rkv, mu_lora, params, *, bp, tp, d, lay):
    n_hp = d // LANES
    pw, pa, pg = lay["pw"], lay["pa"], lay["pg"]
    rc = 512 if tp % 512 == 0 else tp
    n_heads = d // HEAD
    n_chunks = tp // CHUNK
    group = next(g for g in (8, 4, 2, 1) if n_chunks % g == 0)
    kern = functools.partial(_rwkv_prompt_kernel, t_len=tp, rc=rc, group=group)
    vm = pltpu.VMEM((tp, LANES), F32)
    return pl.pallas_call(
        kern,
        grid=(bp, n_hp),
        in_specs=_rwkv_specs(None, tp, 0, lay),
        out_specs=[
            pl.BlockSpec((tp, LANES), lambda b, hp: (b, hp)),
            pl.BlockSpec((1, 2, HEAD, HEAD), lambda b, hp: (b, hp, 0, 0)),
        ],
        out_shape=[
            jax.ShapeDtypeStruct((bp * tp, d), BF16),
            jax.ShapeDtypeStruct((bp, n_heads, HEAD, HEAD), F32),
        ],
        scratch_shapes=[vm] * 8 + [
            pltpu.VMEM((tp + group * CHUNK, LANES), F32),
            pltpu.VMEM((tp, pw * LANES), BF16),
            pltpu.VMEM((tp, pa * LANES), BF16),
            pltpu.VMEM((tp, pg * LANES), BF16),
            pltpu.VMEM(((n_chunks + group) * (CHUNK + LANES), LANES), F32),
            pltpu.VMEM(((n_chunks + group) * (CHUNK + LANES), LANES), F32),
            pltpu.VMEM((group, 2 * CHUNK, 2 * LANES), F32),
        ],
        compiler_params=_cparams(2),
        name="rwkv_prompt",
    )(p_main, p_main, p_main, p_tail, p_tail, p_tail,
      mu_rkv, mu_rkv, mu_rkv, mu_lora, mu_lora, mu_lora, *params)


def _rwkv_sample_kernel(pr_ref, pk_ref, pv_ref, pwd_ref, pad_ref, pgd_ref,
                        qr_ref, qk_ref, qv_ref, qwd_ref, qad_ref, qgd_ref,
                        mur_ref, muk_ref, muv_ref, muwd_ref, muad_ref, mugd_ref,
                        w0_ref, w2_ref, a0_ref, a2_ref, g2_ref, kkw_ref, kaw_ref, rkw_ref, gnw_ref, gnb_ref,
                        st_ref, ob_ref, so_ref, y_s, tv_s, *, ts, bs):
    ones_bd = _head_ones()
    refs = (pr_ref, pk_ref, pv_ref, pwd_ref, pad_ref, pgd_ref)
    prevs = (qr_ref, qk_ref, qv_ref, qwd_ref, qad_ref, qgd_ref)
    mus = (mur_ref, muk_ref, muv_ref, muwd_ref, muad_ref, mugd_ref)
    zs = []
    for ref, q, mu in zip(refs, prevs, mus):
        p = ref[...].astype(F32)
        pp = jnp.concatenate([q[...].astype(F32), p[0:(ts - 1) * bs, :]], axis=0)
        zs.append(p + mu[...] * (pp - p))
    zr, zk, zv, zwd, zad, zgd = zs
    lw, g, kk, k2, beta, bonus = _rwkv_prep(
        zr, zk, zv, jnp.tanh(zwd).astype(BF16), zad.astype(BF16), jax.nn.sigmoid(zgd).astype(BF16),
        w0_ref[...], w2_ref[...], a0_ref[...], a2_ref[...], g2_ref[...],
        kkw_ref[...], kaw_ref[...], rkw_ref[...], ones_bd)
    w_dec = jnp.exp(lw)

    for t in range(ts):
        rows = slice(t * bs, (t + 1) * bs)
        for kind, a in enumerate((w_dec, kk, beta, k2, zr, zv)):
            tv_s[t, kind] = a[rows, :].T

    def total(p):
        p = p + pltpu.roll(p, 4, 0)
        p = p + pltpu.roll(p, 2, 0)
        return p + pltpu.roll(p, 1, 0)

    groups = HEAD // SUBLANES
    subi = lax.broadcasted_iota(jnp.int32, (SUBLANES, bs), 0)

    def row_group(gidx, carry):
        h = gidx // groups
        i0 = (gidx % groups) * SUBLANES
        lo = pl.multiple_of(h * HEAD, HEAD)
        row0 = pl.multiple_of(gidx * SUBLANES, SUBLANES)
        y_tiles = [jnp.zeros((SUBLANES, bs), F32) for _ in range(ts)]
        for half in range(0, SUBLANES, _SAMPLE_ROWS):
            s_rows = [st_ref[h, i0 + half + i].reshape(groups, SUBLANES, bs) for i in range(_SAMPLE_ROWS)]
            for t in range(ts):
                wt, kkt, bt, kt, rt = (tv_s[t, kind, pl.ds(lo, HEAD), :].reshape(groups, SUBLANES, bs)
                                       for kind in range(5))
                vt8 = tv_s[t, 5, pl.ds(row0, SUBLANES), :]
                for i in range(_SAMPLE_ROWS):
                    s = s_rows[i]
                    sa = -total(jnp.sum(s * kkt, axis=0))
                    s = s * wt + sa[None] * bt + vt8[half + i:half + i + 1, :][None] * kt
                    s_rows[i] = s
                    y_tiles[t] = jnp.where(subi == half + i, total(jnp.sum(s * rt, axis=0)), y_tiles[t])
            for i in range(_SAMPLE_ROWS):
                so_ref[h, i0 + half + i] = s_rows[i].reshape(HEAD, bs)
        for t in range(ts):
            y_s[t, pl.ds(row0, SUBLANES), :] = y_tiles[t]
        return carry

    lax.fori_loop(0, 2 * groups, row_group, 0)
    for t in range(ts):
        yt = y_s[t].T
        rows = slice(t * bs, (t + 1) * bs)
        o = _rwkv_post(yt, bonus[rows], g[rows], gnw_ref[...], gnb_ref[...], ones_bd)
        ob_ref[rows, :] = o.astype(ob_ref.dtype)


def _rwkv_sample(p_main, p_tail, q_rkv, q_lora, mu_rkv, mu_lora, params, st, *, ts, bs, d, mp, lay):
    n_hp = d // LANES
    rows = ts * bs
    rblk = mp // rows
    in_specs = _rwkv_specs(rblk, rows, bs, lay)
    in_specs.append(pl.BlockSpec((2, HEAD, HEAD, bs), lambda hp: (hp, 0, 0, 0)))
    kern = functools.partial(_rwkv_sample_kernel, ts=ts, bs=bs)
    return pl.pallas_call(
        kern,
        grid=(n_hp,),
        in_specs=in_specs,
        out_specs=[
            pl.BlockSpec((rows, LANES), lambda hp: (0, hp)),
            pl.BlockSpec((2, HEAD, HEAD, bs), lambda hp: (hp, 0, 0, 0)),
        ],
        out_shape=[
            jax.ShapeDtypeStruct((rows, d), BF16),
            jax.ShapeDtypeStruct(st.shape, F32),
        ],
        scratch_shapes=[pltpu.VMEM((ts, LANES, bs), F32), pltpu.VMEM((ts, 6, LANES, bs), F32)],
        compiler_params=_cparams(1),
        name="rwkv_sample",
    )(p_main, p_main, p_main, p_tail, p_tail, p_tail, q_rkv, q_rkv, q_rkv, q_lora, q_lora, q_lora,
      mu_rkv, mu_rkv, mu_rkv, mu_lora, mu_lora, mu_lora, *params, st)


def _pad_cols(a, width):
    return jnp.pad(a, ((0, 0), (0, width - a.shape[1])))


def _pad_rows(a, height):
    return jnp.pad(a, ((0, height - a.shape[0]), (0, 0)))


def _tiles(d, d_ff):
    pick = lambda n, cands: next(c for c in cands if n % c == 0)
    return dict(
        tm=512, tm_wide=256,
        main=pick(5 * d, (2560, 1280, 1024, 512, 256, 128)),
        gates=pick(2 * d, (2048, 1024, 512, 256, 128)),
        prev=pick(d, (1024, 512, 256, 128)),
        up=pick(d_ff, (2048, 1024, 512, 256, 128)),
        down=pick(d, (512, 256, 128)),
    )


def kernel(x_prompt, x_sample, p_prompt, p_sample, state_rg_h, state_rg_conv, state_rwkv, state_shift,
           norm_mix, w_in, conv_w, conv_b, rg_wa, rg_ba, rg_wx, rg_bx, rg_lam, w_rg_o,
           mu_shift, rw_w0, rw_w2, rw_a0, rw_a2, rw_g2, rw_kk, rw_ka, rw_rk, rw_gn_w, rw_gn_b,
           w_rw_o, w_o, norm_ffn, w_up, w_down, norm_ple, w_ple_gate, w_ple, norm_f):
    bp, tp, d = x_prompt.shape
    bs, ts, _ = x_sample.shape
    depth = w_in.shape[0]
    n_heads, head = rw_rk.shape[1], rw_rk.shape[2]
    r_w, r_a, r_g = rw_w2.shape[1], rw_a2.shape[1], rw_g2.shape[1]
    d_ple = w_ple.shape[1]
    d_ff = w_up.shape[2]
    n_tap = conv_w.shape[1]
    assert head == HEAD and n_heads * HEAD == d and d % LANES == 0 and bs == LANES
    assert rg_wa.shape[2] == LANES and tp % CHUNK == 0 and n_tap == 4
    mp, ms = bp * tp, bs * ts
    nb = d // LANES
    pw, pa, pg = (_round_up(r, LANES) // LANES for r in (r_w, r_a, r_g))
    n_lora = (pw + pa + pg) * LANES
    lay = dict(u=0, g=nb, r=2 * nb, k=3 * nb, v=4 * nb, wd=0, ad=pw, gd=pw + pa, pw=pw, pa=pa, pg=pg, nb=nb)
    t = _tiles(d, d_ff)
    tm = t["tm"]
    assert mp % tm == 0 and ms % tm == 0 and mp % (ts * bs) == 0 and ms % t["tm_wide"] == 0
    assert lay["wd"] % pw == 0 and lay["ad"] % pa == 0 and lay["gd"] % pg == 0 and pw % pa == 0 and (pw + pa) % pg == 0
    assert (2 * d) % t["prev"] == 0 and (5 * d + r_w + r_a + r_g) % SUBLANES == 0
    mm = functools.partial(_mm, mp=mp, ms=ms)

    x = _Rows(x_prompt.reshape(mp, d), jnp.transpose(x_sample, (1, 0, 2)).reshape(ms, d))
    hp_l, cp_l, sp_l, xp_l, hs_l, cs_l, ss_l, xs_l = [], [], [], [], [], [], [], []
    y_p = y_s = None
    for i in range(depth):
        o_rw = 2 * d
        o_lora = o_rw + 3 * d
        o_g = o_lora + r_w + r_a + r_g
        wt = jnp.swapaxes(w_in[i], 0, 1)

        def lora_rows(a):
            return jnp.concatenate([
                _pad_rows(a[o_lora:o_lora + r_w], pw * LANES),
                _pad_rows(a[o_lora + r_w:o_lora + r_w + r_a], pa * LANES),
                _pad_rows(a[o_lora + r_w + r_a:o_g], pg * LANES)], axis=0)

        wt_lora = lora_rows(wt)
        mu_all = jnp.concatenate([jnp.zeros((o_rw, 1), F32), mu_shift[i][:, None], jnp.zeros((2 * d, 1), F32)], axis=0)
        mu_rkv = mu_shift[i][None, :3 * d]
        mu_lora = lora_rows(mu_all).reshape(1, n_lora)
        rw_params = (
            rw_w0[i][None], _pad_rows(rw_w2[i], pw * LANES).astype(BF16),
            rw_a0[i][None], _pad_rows(rw_a2[i], pa * LANES).astype(BF16),
            _pad_rows(rw_g2[i], pg * LANES).astype(BF16),
            rw_kk[i][None], rw_ka[i][None], rw_rk[i].reshape(1, d), rw_gn_w[i][None], rw_gn_b[i][None])
        rg_params = (conv_w[i], conv_b[i][None], rg_wa[i].astype(BF16), rg_ba[i][:, None, :],
                     rg_wx[i].astype(BF16), rg_bx[i][:, None, :], rg_lam[i][None])

        xn = _Rows(_rmsnorm(x, norm_mix[i], BF16, tm, mp))
        xa = x.arrays if len(x.arrays) == 2 else (x.arrays[0][:mp], x.arrays[0][mp:])
        x_last = jnp.concatenate([xa[0].reshape(bp, tp, d)[:, -1], xa[1][(ts - 1) * bs:]], axis=0)
        n_last = _round_up(bp + bs, SUBLANES)
        xn_last = _rmsnorm(_Rows(_pad_rows(x_last, n_last)), norm_mix[i], F32, n_last, n_last)
        (p_main,) = mm([xn], [wt], [], _epi_plain, [(F32, False)], tm=tm, tn=t["main"], n_out=5 * d,
                       cast_w=True, w_transposed=True, single_buffer_w=True, name="in_proj")
        (p_gate,) = mm([xn], [wt], [], _epi_plain, [(BF16, False)], tm=tm, tn=t["gates"], n_out=2 * d,
                       w_row_start=o_g, cast_w=True, w_transposed=True, single_buffer_w=True, name="in_proj_gates")
        (p_lora,) = mm([xn], [wt_lora], [], _epi_plain, [(BF16, False)], tm=tm, tn=n_lora, n_out=n_lora,
                       cast_w=True, w_transposed=True, name="in_proj_lora")
        xprev = _Rows(state_shift[i])
        (q_rkv,) = _mm([xprev], [wt], [], _epi_plain, [(F32, False)], mp=bs, ms=0, tm=bs, tn=t["prev"],
                       n_out=3 * d, w_col_off=o_rw // t["prev"], cast_w=True, w_transposed=True, name="prev_proj")
        (q_lora,) = _mm([xprev], [wt_lora], [], _epi_plain, [(BF16, False)], mp=bs, ms=0, tm=bs, tn=n_lora,
                        n_out=n_lora, cast_w=True, w_transposed=True, name="prev_proj_lora")

        cwid = 2 * LANES if nb % 2 == 0 else LANES
        hg_p, h_p, c_p = _rg_prompt(p_main, *rg_params, bp=bp, tp=tp, d=d, cb_u=lay["u"], cb_g=lay["g"], cwid=cwid)
        cbuf_t = jnp.transpose(state_rg_conv[i], (1, 0, 2))
        hg_s, h_s, c_s = _rg_sample(p_main, cbuf_t, state_rg_h[i], *rg_params, ts=ts, bs=bs, d=d, mp=mp,
                                    cb_u=lay["u"], cb_g=lay["g"], cwid=cwid)

        ob_p, s_p = _rwkv_prompt(p_main, p_lora, mu_rkv, mu_lora, rw_params, bp=bp, tp=tp, d=d, lay=lay)
        st = jnp.transpose(state_rwkv[i], (1, 2, 3, 0))
        ob_s, st_new = _rwkv_sample(p_main, p_lora, q_rkv, q_lora, mu_rkv, mu_lora, rw_params, st,
                                    ts=ts, bs=bs, d=d, mp=mp, lay=lay)
        s_s = jnp.transpose(st_new, (3, 0, 1, 2))

        tw = t["tm_wide"]
        (merged,) = mm([_Rows(hg_p, hg_s), _Rows(ob_p, ob_s)], [w_rg_o[i].astype(BF16), w_rw_o[i].astype(BF16)],
                       [("tile", _Rows(p_gate), 0), ("tile", _Rows(p_gate), 1)],
                       _epi_merge, [(BF16, False)], tm=tw, tn=d, n_out=d, single_buffer_w=True, name="merge")
        x1, xn2 = mm([_Rows(merged)], [w_o[i]], [("tile", x, 0), ("row", norm_ffn[i][None], 0)],
                     _epi_residual_norm, [(F32, False), (BF16, False)], tm=tw, tn=d, n_out=d, cast_w=True,
                     single_buffer_w=True, name="out_proj")
        (hf,) = mm([_Rows(xn2)], [w_up[i]], [], _epi_relu2, [(BF16, False)], tm=tm, tn=t["up"], n_out=d_ff,
                   cast_w=True, single_buffer_w=True, name="mlp_up")
        (x2,) = mm([_Rows(hf)], [w_down[i].astype(BF16)], [("tile", _Rows(x1), 0)], _epi_residual, [(F32, False)],
                   tm=tm, tn=t["down"], n_out=d, name="mlp_down")
        p_rows = _Rows(p_prompt[i].reshape(mp, d_ple), jnp.transpose(p_sample[i], (1, 0, 2)).reshape(ms, d_ple))
        ple_ws = [w_ple_gate[i], w_ple[i]]
        g_ple = ("row", norm_ple[i][None], 0)
        if i == depth - 1:
            y_p, y_s = mm([("norm_of", 0, 2), p_rows], ple_ws,
                          [("tile", _Rows(x2), 0), ("row", norm_f[None], 0), g_ple],
                          _epi_ple_final, [(F32, True)], tm=tw, tn=d, n_out=d, cast_w=True, single_buffer_w=True,
                          name="ple_final")
        else:
            x = _Rows(*mm([("norm_of", 0, 1), p_rows], ple_ws, [("tile", _Rows(x2), 0), g_ple], _epi_ple,
                          [(F32, False)], tm=tw, tn=d, n_out=d, cast_w=True, single_buffer_w=True, name="ple"))

        hp_l.append(h_p[:, 0])
        cp_l.append(c_p[:, SUBLANES - (n_tap - 1):])
        sp_l.append(s_p)
        xp_l.append(xn_last[:bp])
        hs_l.append(h_s)
        cs_l.append(jnp.transpose(c_s, (1, 0, 2)))
        ss_l.append(s_s)
        xs_l.append(xn_last[bp:bp + bs])

    y_prompt = y_p.reshape(bp, tp, d)
    y_sample = jnp.transpose(y_s.reshape(ts, bs, d), (1, 0, 2))
    return (y_prompt, y_sample,
            jnp.stack(hp_l), jnp.stack(cp_l), jnp.stack(sp_l), jnp.stack(xp_l),
            jnp.stack(hs_l), jnp.stack(cs_l), jnp.stack(ss_l), jnp.stack(xs_l))
```
